```python
import math
import jax, jax.numpy as jnp
from jax import lax
import numpy as np

D_MODEL = 1024
BATCH = 16
SEQ = 256
DEPTH = 2
DEC_BATCH = 8
DEC_SEQ = 4096
PAST_LEN = 256

GRID_W = 64
N_EVEN = (DEPTH + 1) // 2
N_ODD = DEPTH // 2
N_MOD = 6
ALPHA = (2.0 * DEPTH) ** 0.25
BETA = (8.0 * DEPTH) ** -0.25
LN_EPS = 1e-5
RMS_EPS = 1e-6
Q_BLOCK = 128
NA_HEADS = 8
NA_HEAD_DIM = 64
NA_WIDTH = NA_HEADS * NA_HEAD_DIM
NA_MAX_ROWS = 8
NA_WIN_COLS = 16
NA_SCALE = NA_HEAD_DIM ** -0.5
CONV_WIDTH = 512
CONV_TAPS = 3
AB_IN = 3 * NA_WIDTH + 3 * CONV_WIDTH
AB_OUT = NA_WIDTH + CONV_WIDTH
MLA_HEADS = 16
Q_LORA = 256
KV_LORA = 256
QK_NOPE = 64
QK_ROPE = 32
V_HEAD = 64
MLA_IN = Q_LORA + KV_LORA + QK_ROPE
MLA_OUT = MLA_HEADS * V_HEAD
MLA_SCALE = (QK_NOPE + QK_ROPE) ** -0.5
ROPE_THETA = 10000.0
ROPE_PAIRS_AXIS = QK_ROPE // 4
PEER_HEADS = 8
PEER_KEY_DIM = 256
PEER_HALF = PEER_KEY_DIM // 2
N_KEYS = 128
N_EXPERTS = N_KEYS * N_KEYS
PEER_TOPK = 16
PEER_BLOCK = 128

kernel_name = "hybrid_diffusion_natten_shortconv_mla_peer_step"


def layer_norm(x, g, b):
    xf = x.astype(jnp.float32)
    mu = jnp.mean(xf, axis=-1, keepdims=True)
    var = jnp.mean(jnp.square(xf - mu), axis=-1, keepdims=True)
    return ((xf - mu) * lax.rsqrt(var + LN_EPS) * g + b).astype(x.dtype)


def rms_norm(x, g):
    xf = x.astype(jnp.float32)
    return (xf * lax.rsqrt(jnp.mean(jnp.square(xf), axis=-1, keepdims=True) + RMS_EPS) * g).astype(x.dtype)


def adaln(cvec, w_ada, b_ada):
    m = jax.nn.silu(cvec) @ w_ada + b_ada
    return jnp.split(m[..., None, :], N_MOD, axis=-1)


def modulate(x, shift, scale):
    return x * (1 + scale) + shift


def split_heads(t, n):
    return t.reshape(t.shape[:-1] + (n, t.shape[-1] // n))


def merge_heads(t):
    return t.reshape(t.shape[:-2] + (-1,))


def blocked_attention(q, k, v, scale):
    B, Nq, H, dq = q.shape
    nb = Nq // Q_BLOCK
    qb = jnp.moveaxis(q.reshape(B, nb, Q_BLOCK, H, dq), 1, 0)

    def block(qi):
        s = jnp.einsum('bqhd,bkhd->bhqk', qi, k).astype(jnp.float32) * scale
        p = jax.nn.softmax(s, axis=-1).astype(v.dtype)
        return jnp.einsum('bhqk,bkhd->bqhd', p, v)

    o = lax.map(block, qb)
    return jnp.moveaxis(o, 0, 1).reshape(B, Nq, H, v.shape[-1])


def short_conv(u, w):
    L = u.shape[1]
    p = CONV_TAPS // 2
    up = jnp.pad(u, ((0, 0), (p, p), (0, 0)))
    return sum(up[:, t:t + L] * w[t] for t in range(CONV_TAPS))


def neighbourhood_attention(q, k, v, k_ctx, v_ctx, rpb):
    B, N, H, dh = q.shape
    rows = N // GRID_W
    wr = min(NA_MAX_ROWS, rows)
    wc = NA_WIN_COLS
    qg = q.reshape(B, rows, GRID_W, H, dh)
    kg = k.reshape(B, rows, GRID_W, H, dh)
    vg = v.reshape(B, rows, GRID_W, H, dh)
    col = jnp.arange(GRID_W)
    cs = jnp.clip(col - wc // 2, 0, GRID_W - wc)
    col_idx = cs[:, None] + jnp.arange(wc)[None, :]
    dc = col_idx - col[:, None] + (NA_WIN_COLS - 1)

    def row_fn(r):
        rs = jnp.clip(r - wr // 2, 0, rows - wr)
        q_r = lax.dynamic_index_in_dim(qg, r, axis=1, keepdims=False)
        k_band = lax.dynamic_slice_in_dim(kg, rs, wr, axis=1)
        v_band = lax.dynamic_slice_in_dim(vg, rs, wr, axis=1)
        k_win = k_band[:, :, col_idx]
        v_win = v_band[:, :, col_idx]
        dr = rs + jnp.arange(wr) - r + (NA_MAX_ROWS - 1)
        bias = rpb[:, dr[None, :, None], dc[:, None, :]]
        s_loc = jnp.einsum('bchd,brcwhd->bhcrw', q_r, k_win).astype(jnp.float32) * NA_SCALE + bias
        s_ctx = jnp.einsum('bchd,blhd->bhcl', q_r, k_ctx).astype(jnp.float32) * NA_SCALE
        s = jnp.concatenate([s_loc.reshape(B, H, GRID_W, wr * wc), s_ctx], axis=-1)
        p = jax.nn.softmax(s, axis=-1).astype(v.dtype)
        p_loc = p[..., :wr * wc].reshape(B, H, GRID_W, wr, wc)
        p_ctx = p[..., wr * wc:]
        return (jnp.einsum('bhcrw,brcwhd->bchd', p_loc, v_win)
                + jnp.einsum('bhcl,blhd->bchd', p_ctx, v_ctx))

    o = lax.map(row_fn, jnp.arange(rows))
    return jnp.moveaxis(o, 0, 1).reshape(B, N, H * dh)


def ab_split(h, w_in):
    cuts = [NA_WIDTH, 2 * NA_WIDTH, 3 * NA_WIDTH, 3 * NA_WIDTH + CONV_WIDTH, 3 * NA_WIDTH + 2 * CONV_WIDTH]
    q, k, v, bg, cg, hin = jnp.split(h @ w_in, cuts, axis=-1)
    return split_heads(q, NA_HEADS), split_heads(k, NA_HEADS), split_heads(v, NA_HEADS), bg, cg, hin


def ab_mixer_context(h, w_in, conv_w, w_out):
    q, k, v, bg, cg, hin = ab_split(h, w_in)
    att = merge_heads(blocked_attention(q, k, v, NA_SCALE))
    conv = bg * short_conv(cg * hin, conv_w)
    return jnp.concatenate([att, conv], axis=-1) @ w_out, k, v


def ab_mixer_latent(h, k_ctx, v_ctx, w_in, conv_w, rpb, w_out):
    q, k, v, bg, cg, hin = ab_split(h, w_in)
    att = neighbourhood_attention(q, k, v, k_ctx, v_ctx, rpb)
    conv = bg * short_conv(cg * hin, conv_w)
    return jnp.concatenate([att, conv], axis=-1) @ w_out


def axial_rope_tables(n):
    t = jnp.arange(n)
    row = (t // GRID_W).astype(jnp.float32)
    col = (t % GRID_W).astype(jnp.float32)
    inv = ROPE_THETA ** (-jnp.arange(ROPE_PAIRS_AXIS, dtype=jnp.float32) / ROPE_PAIRS_AXIS)
    ang = jnp.concatenate([row[:, None] * inv, col[:, None] * inv], axis=-1)
    return jnp.cos(ang), jnp.sin(ang)


def apply_rope(x, cos, sin):
    x1, x2 = jnp.split(x, 2, axis=-1)
    return jnp.concatenate([x1 * cos - x2 * sin, x1 * sin + x2 * cos], axis=-1).astype(x.dtype)


def mla_project(h, w_in, q_g, w_uq, kv_g):
    cq, ckv, kr = jnp.split(h @ w_in, [Q_LORA, Q_LORA + KV_LORA], axis=-1)
    q = split_heads(rms_norm(cq, q_g) @ w_uq, MLA_HEADS)
    return q, rms_norm(ckv, kv_g), kr


def mla_attend(q, ckv, kr, w_ukv, w_out):
    kv = split_heads(ckv @ w_ukv, MLA_HEADS)
    k_nope, v = kv[..., :QK_NOPE], kv[..., QK_NOPE:]
    k = jnp.concatenate([k_nope, jnp.broadcast_to(kr[:, :, None, :], k_nope.shape[:3] + (QK_ROPE,))], axis=-1)
    return merge_heads(blocked_attention(q, k, v, MLA_SCALE)) @ w_out


def mla_context(h, w_in, q_g, w_uq, kv_g, w_ukv, w_out):
    q, ckv, kr = mla_project(h, w_in, q_g, w_uq, kv_g)
    return mla_attend(q, ckv, kr, w_ukv, w_out), ckv, kr


def mla_latent(h, ckv_ctx, kr_ctx, cos, sin, w_in, q_g, w_uq, kv_g, w_ukv, w_out):
    q, ckv, kr = mla_project(h, w_in, q_g, w_uq, kv_g)
    q = jnp.concatenate([q[..., :QK_NOPE], apply_rope(q[..., QK_NOPE:], cos[:, None, :], sin[:, None, :])], axis=-1)
    kr = apply_rope(kr, cos, sin)
    ckv_all = jnp.concatenate([ckv_ctx, ckv], axis=1)
    kr_all = jnp.concatenate([kr_ctx, kr], axis=1)
    return mla_attend(q, ckv_all, kr_all, w_ukv, w_out)


def peer(h, w_q, subkeys, u_tab, v_tab):
    B, L, D = h.shape
    T = B * L
    x = h.reshape(T, D)
    q = (x @ w_q).reshape(T, PEER_HEADS, 2, PEER_HALF)
    s = jnp.einsum('thpd,hpkd->thpk', q, subkeys)
    sv, si = lax.top_k(s, PEER_TOPK)
    cand = (sv[:, :, 0, :, None] + sv[:, :, 1, None, :]).reshape(T, PEER_HEADS, PEER_TOPK * PEER_TOPK)
    cidx = (si[:, :, 0, :, None] * N_KEYS + si[:, :, 1, None, :]).reshape(T, PEER_HEADS, PEER_TOPK * PEER_TOPK)
    top_s, pos = lax.top_k(cand, PEER_TOPK)
    experts = jnp.take_along_axis(cidx, pos, axis=-1)
    g = jax.nn.softmax(top_s.astype(jnp.float32), axis=-1).astype(h.dtype)
    nb = T // PEER_BLOCK

    def block(args):
        xb, eb, gb = args
        a = jax.nn.gelu(jnp.einsum('td,thkd->thk', xb, u_tab[eb])) * gb
        return jnp.einsum('thk,thkd->td', a, v_tab[eb])

    out = lax.map(block, (x.reshape(nb, PEER_BLOCK, D),
                          experts.reshape(nb, PEER_BLOCK, PEER_HEADS, PEER_TOPK),
                          g.reshape(nb, PEER_BLOCK, PEER_HEADS, PEER_TOPK)))
    return out.reshape(B, L, D)


def setup_inputs(seed: int = 0) -> dict:
    key = jax.random.key(seed)
    ks = jax.random.split(key, 32)
    D = D_MODEL

    def nrm(k, shape, scale):
        return jax.random.normal(k, shape, jnp.float32) * scale

    def gain(k, shape):
        return 1.0 + nrm(k, shape, 0.01)

    return {
        "x_prompt": nrm(ks[0], (BATCH, SEQ, D), 1.0),
        "x_sample": nrm(ks[1], (DEC_BATCH, DEC_SEQ, D), 1.0),
        "cache_na_k": nrm(ks[2], (DEC_BATCH, N_EVEN, PAST_LEN, NA_HEADS, NA_HEAD_DIM), 1.0),
        "cache_na_v": nrm(ks[3], (DEC_BATCH, N_EVEN, PAST_LEN, NA_HEADS, NA_HEAD_DIM), 1.0),
        "cache_mla_ckv": nrm(ks[4], (DEC_BATCH, N_ODD, PAST_LEN, KV_LORA), 1.0),
        "cache_mla_krope": nrm(ks[5], (DEC_BATCH, N_ODD, PAST_LEN, QK_ROPE), 1.0),
        "c": nrm(ks[6], (DEC_BATCH, D), 1.0),
        "c_ctx": nrm(ks[7], (D,), 1.0),
        "w_ada": nrm(ks[8], (DEPTH, D, N_MOD * D), 0.5 * D ** -0.5),
        "b_ada": nrm(ks[9], (DEPTH, N_MOD * D), 0.01),
        "ln_mix_g": gain(ks[10], (DEPTH, D)),
        "ln_mix_b": nrm(ks[11], (DEPTH, D), 0.01),
        "ln_ffn_g": gain(ks[12], (DEPTH, D)),
        "ln_ffn_b": nrm(ks[13], (DEPTH, D), 0.01),
        "w_in_ab": nrm(ks[14], (N_EVEN, D, AB_IN), D ** -0.5),
        "conv_w": nrm(ks[15], (N_EVEN, CONV_TAPS, CONV_WIDTH), CONV_TAPS ** -0.5),
        "na_rpb": nrm(ks[16], (N_EVEN, NA_HEADS, 2 * NA_MAX_ROWS - 1, 2 * NA_WIN_COLS - 1), 0.02),
        "w_out_ab": nrm(ks[17], (N_EVEN, AB_OUT, D), BETA * AB_OUT ** -0.5),
        "w_in_mla": nrm(ks[18], (N_ODD, D, MLA_IN), D ** -0.5),
        "q_norm_g": gain(ks[19], (N_ODD, Q_LORA)),
        "w_uq": nrm(ks[20], (N_ODD, Q_LORA, MLA_HEADS * (QK_NOPE + QK_ROPE)), Q_LORA ** -0.5),
        "kv_norm_g": gain(ks[21], (N_ODD, KV_LORA)),
        "w_ukv": nrm(ks[22], (N_ODD, KV_LORA, MLA_HEADS * (QK_NOPE + V_HEAD)), KV_LORA ** -0.5),
        "w_out_mla": nrm(ks[23], (N_ODD, MLA_OUT, D), BETA * MLA_OUT ** -0.5),
        "peer_w_q": nrm(ks[24], (DEPTH, D, PEER_HEADS * PEER_KEY_DIM), D ** -0.5),
        "peer_subkeys": nrm(ks[25], (DEPTH, PEER_HEADS, 2, N_KEYS, PEER_HALF), PEER_HALF ** -0.5),
        "peer_u": nrm(ks[26], (DEPTH, N_EXPERTS, D), D ** -0.5),
        "peer_v": nrm(ks[27], (DEPTH, N_EXPERTS, D), BETA),
    }


def reference(x_prompt, x_sample, cache_na_k, cache_na_v, cache_mla_ckv, cache_mla_krope, c, c_ctx,
              w_ada, b_ada, ln_mix_g, ln_mix_b, ln_ffn_g, ln_ffn_b,
              w_in_ab, conv_w, na_rpb, w_out_ab,
              w_in_mla, q_norm_g, w_uq, kv_norm_g, w_ukv, w_out_mla,
              peer_w_q, peer_subkeys, peer_u, peer_v):
    xp, xs = x_prompt, x_sample
    cos, sin = axial_rope_tables(xs.shape[1])
    na_k_out, na_v_out, ckv_out, kr_out = [], [], [], []
    for i in range(DEPTH):
        j = i // 2
        mp = adaln(c_ctx, w_ada[i], b_ada[i])
        ms = adaln(c, w_ada[i], b_ada[i])
        hp = modulate(xp, mp[0], mp[1])
        hs = modulate(xs, ms[0], ms[1])
        if i % 2 == 0:
            op, k_ctx, v_ctx = ab_mixer_context(hp, w_in_ab[j], conv_w[j], w_out_ab[j])
            na_k_out.append(k_ctx)
            na_v_out.append(v_ctx)
            os_ = ab_mixer_latent(hs, cache_na_k[:, j], cache_na_v[:, j], w_in_ab[j], conv_w[j], na_rpb[j], w_out_ab[j])
        else:
            op, ckv_ctx, kr_ctx = mla_context(hp, w_in_mla[j], q_norm_g[j], w_uq[j], kv_norm_g[j], w_ukv[j], w_out_mla[j])
            ckv_out.append(ckv_ctx)
            kr_out.append(kr_ctx)
            os_ = mla_latent(hs, cache_mla_ckv[:, j], cache_mla_krope[:, j], cos, sin,
                             w_in_mla[j], q_norm_g[j], w_uq[j], kv_norm_g[j], w_ukv[j], w_out_mla[j])
        xp = layer_norm(ALPHA * xp + mp[2] * op, ln_mix_g[i], ln_mix_b[i])
        xs = layer_norm(ALPHA * xs + ms[2] * os_, ln_mix_g[i], ln_mix_b[i])
        fp = peer(modulate(xp, mp[3], mp[4]), peer_w_q[i], peer_subkeys[i], peer_u[i], peer_v[i])
        fs = peer(modulate(xs, ms[3], ms[4]), peer_w_q[i], peer_subkeys[i], peer_u[i], peer_v[i])
        xp = layer_norm(ALPHA * xp + mp[5] * fp, ln_ffn_g[i], ln_ffn_b[i])
        xs = layer_norm(ALPHA * xs + ms[5] * fs, ln_ffn_g[i], ln_ffn_b[i])
    new_na_k = jnp.stack(na_k_out, axis=1)
    new_na_v = jnp.stack(na_v_out, axis=1)
    new_mla_ckv = jnp.stack(ckv_out, axis=1)
    new_mla_krope = jnp.stack(kr_out, axis=1)
    return (xp, xs, new_na_k, new_na_v, new_mla_ckv, new_mla_krope)
```

```python
import functools

import jax
import jax.numpy as jnp
from jax import lax
from jax.experimental import pallas as pl
from jax.experimental.pallas import tpu as pltpu

F32 = jnp.float32
BF16 = jnp.bfloat16

D_MODEL = 1024
DEPTH = 2
GRID_W = 64
N_MOD = 6
ALPHA = (2.0 * DEPTH) ** 0.25
LN_EPS = 1e-5
RMS_EPS = 1e-6
NA_HEADS = 8
NA_HEAD_DIM = 64
NA_WIDTH = NA_HEADS * NA_HEAD_DIM
NA_MAX_ROWS = 8
NA_WIN_COLS = 16
NA_SCALE = NA_HEAD_DIM ** -0.5
CONV_WIDTH = 512
CONV_TAPS = 3
MLA_HEADS = 16
Q_LORA = 256
KV_LORA = 256
QK_NOPE = 64
QK_ROPE = 32
V_HEAD = 64
MLA_SCALE = (QK_NOPE + QK_ROPE) ** -0.5
ROPE_THETA = 10000.0
ROPE_PAIRS_AXIS = QK_ROPE // 4
PEER_HEADS = 8
PEER_HALF = 128
N_KEYS = 128
N_EXPERTS = N_KEYS * N_KEYS
PEER_TOPK = 16

LANES = 128
MOD_ROWS = 8
HEAD_SLAB = 128
MASK_BIAS = -1e30
VMEM_LIMIT = 52 * 1024 * 1024


def _cparams(sem):
    return pltpu.CompilerParams(dimension_semantics=sem, vmem_limit_bytes=VMEM_LIMIT)


def _nt_dot(a, b):
    return lax.dot_general(a, b, (((1,), (1,)), ((), ())), preferred_element_type=F32)


def _dot(a, b):
    return jnp.dot(a, b, preferred_element_type=F32)


def _split_bf16(a):
    hi = a.astype(BF16)
    lo = (a - hi.astype(F32)).astype(BF16)
    return hi, lo


def _layer_norm(z, g, b):
    mu = jnp.mean(z, axis=-1, keepdims=True)
    zc = z - mu
    var = jnp.mean(zc * zc, axis=-1, keepdims=True)
    return zc * lax.rsqrt(var + LN_EPS) * g + b


def _gelu_tanh(x):
    return 0.5 * x * (1.0 + jnp.tanh(0.7978845608028654 * (x + 0.044715 * (x * x * x))))


def _ada_kernel(c_ref, w_ref, b_ref, o_ref):
    c = c_ref[...]
    a = c * (1.0 / (1.0 + jnp.exp(-c)))
    a_hi, a_lo = _split_bf16(a)
    w_hi, w_lo = _split_bf16(w_ref[...])
    o_ref[...] = _dot(a_hi, w_hi) + _dot(a_hi, w_lo) + _dot(a_lo, w_hi) + b_ref[...]


def ada_modulation(cvec, w_ada, b_ada, tn=512):
    r, d = cvec.shape
    n = w_ada.shape[1]
    return pl.pallas_call(
        _ada_kernel,
        grid=(n // tn,),
        in_specs=[pl.BlockSpec((r, d), lambda j: (0, 0)),
                  pl.BlockSpec((d, tn), lambda j: (0, j)),
                  pl.BlockSpec((1, tn), lambda j: (0, j))],
        out_specs=pl.BlockSpec((r, tn), lambda j: (0, j)),
        out_shape=jax.ShapeDtypeStruct((r, n), F32),
        compiler_params=_cparams(("arbitrary",)),
        name="ada_modulation",
    )(cvec, w_ada, b_ada.reshape(1, n))


def _inproj_ab_kernel(x_ref, mod_ref, w_ref, qkv_ref, gu_ref, *kv_ref):
    m = mod_ref[0]
    h = x_ref[...] * (1.0 + m[1:2]) + m[0:1]
    y = _dot(h.astype(BF16), w_ref[...])
    w3 = 3 * NA_WIDTH
    qkv_ref[:, :NA_WIDTH] = (y[:, :NA_WIDTH] * NA_SCALE).astype(BF16)
    qkv_ref[:, NA_WIDTH:] = y[:, NA_WIDTH:w3].astype(BF16)
    if kv_ref:
        kv_ref[0][...] = y[:, NA_WIDTH:w3]
    gu_ref[:, :CONV_WIDTH] = y[:, w3:w3 + CONV_WIDTH].astype(BF16)
    gu_ref[:, CONV_WIDTH:] = (y[:, w3 + CONV_WIDTH:w3 + 2 * CONV_WIDTH]
                              * y[:, w3 + 2 * CONV_WIDTH:]).astype(BF16)


def inproj_ab(x, mod, w_bf16, tokens_per_group, emit_kv, tm=512):
    n, d = x.shape
    tm = min(tm, n)
    out_shape = [jax.ShapeDtypeStruct((n, 3 * NA_WIDTH), BF16),
                 jax.ShapeDtypeStruct((n, 2 * CONV_WIDTH), BF16)]
    out_specs = [pl.BlockSpec((tm, 3 * NA_WIDTH), lambda i: (i, 0)),
                 pl.BlockSpec((tm, 2 * CONV_WIDTH), lambda i: (i, 0))]
    if emit_kv:
        out_shape.append(jax.ShapeDtypeStruct((n, 2 * NA_WIDTH), F32))
        out_specs.append(pl.BlockSpec((tm, 2 * NA_WIDTH), lambda i: (i, 0)))
    return pl.pallas_call(
        _inproj_ab_kernel,
        grid=(n // tm,),
        in_specs=[pl.BlockSpec((tm, d), lambda i: (i, 0)),
                  pl.BlockSpec((1, MOD_ROWS, d), lambda i: ((i * tm) // tokens_per_group, 0, 0)),
                  pl.BlockSpec(w_bf16.shape, lambda i: (0, 0))],
        out_specs=out_specs,
        out_shape=out_shape,
        compiler_params=_cparams(("parallel",)),
        name="inproj_ab",
    )(x, mod, w_bf16)


def _pair_attention(q_pair, keys, values, biases):
    lane = lax.broadcasted_iota(jnp.int32, q_pair.shape, 1)
    qf = q_pair.astype(F32)
    outs = []
    for j in range(2):
        in_head = (lane >= j * NA_HEAD_DIM) & (lane < (j + 1) * NA_HEAD_DIM)
        qm = jnp.where(in_head, qf, 0.0).astype(BF16)
        scores = []
        for t, k in enumerate(keys):
            s = _nt_dot(qm, k)
            if biases[j][t] is not None:
                s = s + biases[j][t]
            scores.append(s)
        mx = functools.reduce(jnp.maximum, [jnp.max(s, axis=-1, keepdims=True) for s in scores])
        es = [jnp.exp(s - mx) for s in scores]
        den = functools.reduce(jnp.add, [jnp.sum(e, axis=-1, keepdims=True) for e in es])
        o = functools.reduce(jnp.add, [_dot(e.astype(BF16), v) for e, v in zip(es, values)])
        outs.append(o / den)
    return jnp.where(lane < NA_HEAD_DIM, outs[0], outs[1])


def _na_attn_kernel(q_ref, k_ref, v_ref, kc_ref, vc_ref, bias_ref, o_ref, *, rows):
    r = pl.program_id(1)
    rs = jnp.clip(r - NA_MAX_ROWS // 2, 0, rows - NA_MAX_ROWS)
    start = pl.multiple_of(rs * GRID_W, GRID_W)
    band = NA_MAX_ROWS * GRID_W
    for p in range(NA_HEADS // 2):
        sl = slice(p * LANES, (p + 1) * LANES)
        keys = [k_ref[pl.ds(start, band), sl], kc_ref[:, sl]]
        values = [v_ref[pl.ds(start, band), sl], vc_ref[:, sl]]
        biases = [[bias_ref[0, 2 * p + j], None] for j in range(2)]
        o_ref[:, sl] = _pair_attention(q_ref[:, sl], keys, values, biases).astype(BF16)


def na_attention(qkv, kc, vc, bias, n_batch, rows):
    n = qkv.shape[0]
    seq = rows * GRID_W
    lc = kc.shape[0] // n_batch
    half = NA_MAX_ROWS // 2

    def bias_idx(b, r):
        return (r - jnp.clip(r - half, 0, rows - NA_MAX_ROWS), 0, 0, 0)

    return pl.pallas_call(
        functools.partial(_na_attn_kernel, rows=rows),
        grid=(n_batch, rows),
        in_specs=[pl.BlockSpec((GRID_W, NA_WIDTH), lambda b, r: (b * rows + r, 0)),
                  pl.BlockSpec((seq, NA_WIDTH), lambda b, r: (b, 1)),
                  pl.BlockSpec((seq, NA_WIDTH), lambda b, r: (b, 2)),
                  pl.BlockSpec((lc, NA_WIDTH), lambda b, r: (b, 0)),
                  pl.BlockSpec((lc, NA_WIDTH), lambda b, r: (b, 0)),
                  pl.BlockSpec((1, NA_HEADS, GRID_W, NA_MAX_ROWS * GRID_W), bias_idx)],
        out_specs=pl.BlockSpec((GRID_W, NA_WIDTH), lambda b, r: (b * rows + r, 0)),
        out_shape=jax.ShapeDtypeStruct((n, NA_WIDTH), BF16),
        compiler_params=_cparams(("parallel", "arbitrary")),
        name="na_attention",
    )(qkv, qkv, qkv, kc, vc, bias)


def na_bias_table(rpb):
    off = jnp.arange(NA_MAX_ROWS)[:, None, None, None]
    c = jnp.arange(GRID_W)[None, :, None, None]
    kr = jnp.arange(NA_MAX_ROWS)[None, None, :, None]
    kc = jnp.arange(GRID_W)[None, None, None, :]
    cs = jnp.clip(c - NA_WIN_COLS // 2, 0, GRID_W - NA_WIN_COLS)
    valid = (kc >= cs) & (kc < cs + NA_WIN_COLS)
    dr = kr - off + (NA_MAX_ROWS - 1)
    dc = jnp.clip(kc - c + (NA_WIN_COLS - 1), 0, 2 * NA_WIN_COLS - 2)
    dr, dc, valid = jnp.broadcast_arrays(dr, dc, valid)
    b = rpb[:, dr, dc]
    b = jnp.where(valid[None], b, MASK_BIAS)
    b = jnp.moveaxis(b, 0, 1)
    return b.reshape(NA_MAX_ROWS, NA_HEADS, GRID_W, NA_MAX_ROWS * GRID_W).astype(F32)


def _ctx_attn_kernel(q_ref, k_ref, v_ref, o_ref):
    for p in range(NA_HEADS // 2):
        sl = slice(p * LANES, (p + 1) * LANES)
        o_ref[:, sl] = _pair_attention(q_ref[:, sl], [k_ref[:, sl]], [v_ref[:, sl]],
                                       [[None], [None]]).astype(BF16)


def ctx_attention(qkv, seq):
    n = qkv.shape[0]
    return pl.pallas_call(
        _ctx_attn_kernel,
        grid=(n // seq,),
        in_specs=[pl.BlockSpec((seq, NA_WIDTH), lambda b: (b, 0)),
                  pl.BlockSpec((seq, NA_WIDTH), lambda b: (b, 1)),
                  pl.BlockSpec((seq, NA_WIDTH), lambda b: (b, 2))],
        out_specs=pl.BlockSpec((seq, NA_WIDTH), lambda b: (b, 0)),
        out_shape=jax.ShapeDtypeStruct((n, NA_WIDTH), BF16),
        compiler_params=_cparams(("parallel",)),
        name="ctx_attention",
    )(qkv, qkv, qkv)


HALO = 16


def _outproj_ab_kernel(att_ref, gu_ref, prev_ref, next_ref, x_ref, mod_ref, w_ref, cw_ref, g_ref, b_ref,
                       o_ref, *, tiles_per_seq):
    i = pl.program_id(0)
    tm = x_ref.shape[0]
    m = mod_ref[0]
    gu = gu_ref[...].astype(F32)
    bg = gu[:, :CONV_WIDTH]
    u = gu[:, CONV_WIDTH:]
    has_prev = (i % tiles_per_seq != 0).astype(F32)
    has_next = (i % tiles_per_seq != tiles_per_seq - 1).astype(F32)
    prev_row = prev_ref[HALO - 1:HALO, CONV_WIDTH:].astype(F32) * has_prev
    next_row = next_ref[0:1, CONV_WIDTH:].astype(F32) * has_next
    row = lax.broadcasted_iota(jnp.int32, u.shape, 0)
    u_m1 = jnp.where(row == 0, prev_row, pltpu.roll(u, 1, axis=0))
    u_p1 = jnp.where(row == tm - 1, next_row, pltpu.roll(u, tm - 1, axis=0))
    cw = cw_ref[...]
    conv = bg * (u_m1 * cw[0:1] + u * cw[1:2] + u_p1 * cw[2:3])
    y = _dot(att_ref[...], w_ref[:NA_WIDTH, :]) + _dot(conv.astype(BF16), w_ref[NA_WIDTH:, :])
    z = ALPHA * x_ref[...] + m[2:3] * y
    o_ref[...] = _layer_norm(z, g_ref[...], b_ref[...])


def outproj_ab(att, gu, x, mod, w_bf16, conv_w8, ln_g, ln_b, tokens_per_group, seq, tm=256):
    n, d = x.shape
    tm = min(tm, seq)
    hb = tm // HALO
    last_halo = n // HALO - 1
    return pl.pallas_call(
        functools.partial(_outproj_ab_kernel, tiles_per_seq=seq // tm),
        grid=(n // tm,),
        in_specs=[pl.BlockSpec((tm, NA_WIDTH), lambda i: (i, 0)),
                  pl.BlockSpec((tm, 2 * CONV_WIDTH), lambda i: (i, 0)),
                  pl.BlockSpec((HALO, 2 * CONV_WIDTH), lambda i: (jnp.maximum(i * hb - 1, 0), 0)),
                  pl.BlockSpec((HALO, 2 * CONV_WIDTH), lambda i: (jnp.minimum((i + 1) * hb, last_halo), 0)),
                  pl.BlockSpec((tm, d), lambda i: (i, 0)),
                  pl.BlockSpec((1, MOD_ROWS, d), lambda i: ((i * tm) // tokens_per_group, 0, 0)),
                  pl.BlockSpec(w_bf16.shape, lambda i: (0, 0)),
                  pl.BlockSpec(conv_w8.shape, lambda i: (0, 0)),
                  pl.BlockSpec((1, d), lambda i: (0, 0)),
                  pl.BlockSpec((1, d), lambda i: (0, 0))],
        out_specs=pl.BlockSpec((tm, d), lambda i: (i, 0)),
        out_shape=jax.ShapeDtypeStruct((n, d), F32),
        compiler_params=_cparams(("parallel",)),
        name="outproj_ab",
    )(att, gu, gu, gu, x, mod, w_bf16, conv_w8, ln_g, ln_b)


def _outproj_kernel(att_ref, x_ref, mod_ref, w_ref, g_ref, b_ref, o_ref):
    m = mod_ref[0]
    y = _dot(att_ref[...], w_ref[...])
    z = ALPHA * x_ref[...] + m[2:3] * y
    o_ref[...] = _layer_norm(z, g_ref[...], b_ref[...])


def outproj(att, x, mod, w_bf16, ln_g, ln_b, tokens_per_group, tm=512):
    n, d = x.shape
    k = att.shape[1]
    return pl.pallas_call(
        _outproj_kernel,
        grid=(n // tm,),
        in_specs=[pl.BlockSpec((tm, k), lambda i: (i, 0)),
                  pl.BlockSpec((tm, d), lambda i: (i, 0)),
                  pl.BlockSpec((1, MOD_ROWS, d), lambda i: ((i * tm) // tokens_per_group, 0, 0)),
                  pl.BlockSpec(w_bf16.shape, lambda i: (0, 0)),
                  pl.BlockSpec((1, d), lambda i: (0, 0)),
                  pl.BlockSpec((1, d), lambda i: (0, 0))],
        out_specs=pl.BlockSpec((tm, d), lambda i: (i, 0)),
        out_shape=jax.ShapeDtypeStruct((n, d), F32),
        compiler_params=_cparams(("parallel",)),
        name="outproj",
    )(att, x, mod, w_bf16, ln_g, ln_b)


def _peer_scores_kernel(x_ref, mod_ref, wq_ref, sk_ref, hm_ref, st_ref):
    m = mod_ref[0]
    hb = (x_ref[...] * (1.0 + m[4:5]) + m[3:4]).astype(BF16)
    hm_ref[...] = hb
    q = _dot(hb, wq_ref[...])
    for c in range(2 * PEER_HEADS):
        qc = q[:, c * PEER_HALF:(c + 1) * PEER_HALF].astype(BF16)
        st_ref[c] = _nt_dot(sk_ref[c], qc)


def peer_scores(x, mod, wq_bf16, subkeys_bf16, tokens_per_group, tm=512):
    n, d = x.shape
    nc = 2 * PEER_HEADS
    return pl.pallas_call(
        _peer_scores_kernel,
        grid=(n // tm,),
        in_specs=[pl.BlockSpec((tm, d), lambda i: (i, 0)),
                  pl.BlockSpec((1, MOD_ROWS, d), lambda i: ((i * tm) // tokens_per_group, 0, 0)),
                  pl.BlockSpec(wq_bf16.shape, lambda i: (0, 0)),
                  pl.BlockSpec(subkeys_bf16.shape, lambda i: (0, 0, 0))],
        out_specs=[pl.BlockSpec((tm, d), lambda i: (i, 0)),
                   pl.BlockSpec((nc, N_KEYS, tm), lambda i: (0, 0, i))],
        out_shape=[jax.ShapeDtypeStruct((n, d), BF16),
                   jax.ShapeDtypeStruct((nc, N_KEYS, n), F32)],
        compiler_params=_cparams(("parallel",)),
        name="peer_scores",
    )(x, mod, wq_bf16, subkeys_bf16)


def _distinct_top(s, k):
    vals, cnts = [], []
    work = s
    for _ in range(k):
        mx = jnp.max(work, axis=0, keepdims=True)
        eq = work == mx
        cnt = jnp.sum(jnp.where(eq, 1.0, 0.0), axis=0, keepdims=True)
        vals.append(mx)
        cnts.append(jnp.where(mx > -jnp.inf, cnt, 0.0))
        work = jnp.where(eq, -jnp.inf, work)
    return vals, cnts


def _rows_to_block(rows_list):
    k = len(rows_list)
    t = rows_list[0].shape[1]
    ridx = lax.broadcasted_iota(jnp.int32, (k, t), 0)
    out = jnp.broadcast_to(rows_list[0], (k, t))
    for r in range(1, k):
        out = jnp.where(ridx == r, rows_list[r], out)
    return out


def _peer_select_kernel(st_ref, thr_ref, lse_ref):
    k = PEER_TOPK
    for h in range(PEER_HEADS):
        v1, c1 = _distinct_top(st_ref[2 * h], k)
        v2, c2 = _distinct_top(st_ref[2 * h + 1], k)
        v2b = _rows_to_block(v2)
        c2b = _rows_to_block(c2)
        cand0 = jnp.concatenate([v1[a] + v2b for a in range(k)], axis=0)
        mult = jnp.concatenate([c1[a] * c2b for a in range(k)], axis=0)
        top = v1[0] + v2[0]
        cand = cand0
        thr = jnp.full_like(top, -jnp.inf)
        seen = jnp.zeros_like(top)
        for _ in range(k):
            mx = jnp.max(cand, axis=0, keepdims=True)
            eq = cand == mx
            thr = jnp.where(seen < k, mx, thr)
            seen = seen + jnp.sum(jnp.where(eq, mult, 0.0), axis=0, keepdims=True)
            cand = jnp.where(eq, -jnp.inf, cand)
        z = jnp.sum(jnp.where(cand0 >= thr, mult * jnp.exp(cand0 - top), 0.0), axis=0, keepdims=True)
        thr_ref[h:h + 1, :] = thr
        lse_ref[h:h + 1, :] = top + jnp.log(z)


def peer_select(st, tl=256):
    nc, nk, n = st.shape
    return pl.pallas_call(
        _peer_select_kernel,
        grid=(n // tl,),
        in_specs=[pl.BlockSpec((nc, nk, tl), lambda i: (0, 0, i))],
        out_specs=[pl.BlockSpec((PEER_HEADS, tl), lambda i: (0, i)),
                   pl.BlockSpec((PEER_HEADS, tl), lambda i: (0, i))],
        out_shape=[jax.ShapeDtypeStruct((PEER_HEADS, n), F32),
                   jax.ShapeDtypeStruct((PEER_HEADS, n), F32)],
        compiler_params=_cparams(("parallel",)),
        name="peer_select",
    )(st)


PEER_LANE_CHUNK = 256


def _peer_dense_kernel(hm_ref, u_ref, vt_ref, st_ref, thr_ref, lse_ref, x_ref, mod_ref, g_ref, b_ref,
                       o_ref, acc_ref, ht_ref, at_ref, *, gate_row):
    e = pl.program_id(1)
    te = u_ref.shape[0]
    tm = hm_ref.shape[0]
    keys_per_step = te // N_KEYS

    @pl.when(e == 0)
    def _():
        acc_ref[...] = jnp.zeros_like(acc_ref)

    ht_ref[...] = _nt_dot(u_ref[...], hm_ref[...])

    def per_key(ii, carry):
        i_glob = e * keys_per_step + ii
        row0 = pl.multiple_of(ii * N_KEYS, N_KEYS)
        for c in range(tm // PEER_LANE_CHUNK):
            ls = slice(c * PEER_LANE_CHUNK, (c + 1) * PEER_LANE_CHUNK)
            w = jnp.zeros((N_KEYS, PEER_LANE_CHUNK), F32)
            for h in range(PEER_HEADS):
                s = st_ref[2 * h, pl.ds(i_glob, 1), ls] + st_ref[2 * h + 1, :, ls]
                w = w + jnp.where(s >= thr_ref[h:h + 1, ls], jnp.exp(s - lse_ref[h:h + 1, ls]), 0.0)
            act = _gelu_tanh(ht_ref[pl.ds(row0, N_KEYS), ls])
            at_ref[pl.ds(row0, N_KEYS), ls] = (act * w).astype(BF16)
        return carry

    lax.fori_loop(0, keys_per_step, per_key, 0)
    acc_ref[...] += _dot(vt_ref[...], at_ref[...])

    @pl.when(e == pl.num_programs(1) - 1)
    def _():
        m = mod_ref[0]
        z = ALPHA * x_ref[...] + m[gate_row:gate_row + 1] * acc_ref[...].T
        o_ref[...] = _layer_norm(z, g_ref[...], b_ref[...])


def peer_dense(hm, u_bf16, vt_bf16, st, thr, lse, x, mod, ln_g, ln_b, tokens_per_group, tm=512, te=1024):
    n, d = x.shape
    ne = u_bf16.shape[0]
    nc = st.shape[0]
    return pl.pallas_call(
        functools.partial(_peer_dense_kernel, gate_row=5),
        grid=(n // tm, ne // te),
        in_specs=[pl.BlockSpec((tm, d), lambda i, e: (i, 0)),
                  pl.BlockSpec((te, d), lambda i, e: (e, 0)),
                  pl.BlockSpec((d, te), lambda i, e: (0, e)),
                  pl.BlockSpec((nc, N_KEYS, tm), lambda i, e: (0, 0, i)),
                  pl.BlockSpec((PEER_HEADS, tm), lambda i, e: (0, i)),
                  pl.BlockSpec((PEER_HEADS, tm), lambda i, e: (0, i)),
                  pl.BlockSpec((tm, d), lambda i, e: (i, 0)),
                  pl.BlockSpec((1, MOD_ROWS, d), lambda i, e: ((i * tm) // tokens_per_group, 0, 0)),
                  pl.BlockSpec((1, d), lambda i, e: (0, 0)),
                  pl.BlockSpec((1, d), lambda i, e: (0, 0))],
        out_specs=pl.BlockSpec((tm, d), lambda i, e: (i, 0)),
        out_shape=jax.ShapeDtypeStruct((n, d), F32),
        scratch_shapes=[pltpu.VMEM((d, tm), F32),
                        pltpu.VMEM((te, tm), F32),
                        pltpu.VMEM((te, tm), BF16)],
        compiler_params=_cparams(("parallel", "arbitrary")),
        name="peer_dense",
    )(hm, u_bf16, vt_bf16, st, thr, lse, x, mod, ln_g, ln_b)


def peer_block(x, mod, wq, sk, u, vt, ln_g, ln_b, tokens_per_group):
    hm, st = peer_scores(x, mod, wq, sk, tokens_per_group)
    thr, lse = peer_select(st)
    return peer_dense(hm, u, vt, st, thr, lse, x, mod, ln_g, ln_b, tokens_per_group)


def _rope_slab(t, cos_t, sin_fwd, sin_bwd):
    return t * cos_t + pltpu.roll(t, QK_ROPE // 2, axis=1) * sin_fwd \
        + pltpu.roll(t, HEAD_SLAB - QK_ROPE // 2, axis=1) * sin_bwd


def _mla_inproj_kernel(x_ref, mod_ref, win_ref, qg_ref, kvg_ref, wuq_ref, *rest, use_rope):
    if use_rope:
        cos_ref, sf_ref, sb_ref, q_ref, ckv_ref, slab_ref = rest
    else:
        q_ref, ckv_ref, slab_ref = rest
    m = mod_ref[0]
    h = x_ref[...] * (1.0 + m[1:2]) + m[0:1]
    y = _dot(h.astype(BF16), win_ref[...])
    cq = y[:, :Q_LORA]
    ckv = y[:, Q_LORA:Q_LORA + KV_LORA]
    slab = y[:, Q_LORA + KV_LORA:]
    cqn = cq * lax.rsqrt(jnp.mean(cq * cq, axis=-1, keepdims=True) + RMS_EPS) * qg_ref[...]
    ckv_ref[...] = ckv * lax.rsqrt(jnp.mean(ckv * ckv, axis=-1, keepdims=True) + RMS_EPS) * kvg_ref[...]
    q = _dot(cqn.astype(BF16), wuq_ref[...]) * MLA_SCALE
    if use_rope:
        cos_t, sf, sb = cos_ref[...], sf_ref[...], sb_ref[...]
        slab = _rope_slab(slab, cos_t, sf, sb)
        for hh in range(MLA_HEADS):
            sl = slice(hh * HEAD_SLAB, (hh + 1) * HEAD_SLAB)
            q_ref[:, sl] = _rope_slab(q[:, sl], cos_t, sf, sb).astype(BF16)
    else:
        q_ref[...] = q.astype(BF16)
    slab_ref[...] = slab


def mla_inproj(x, mod, win_bf16, q_g, kv_g, wuq_bf16, rope, tokens_per_group, seq, tm=256):
    n, d = x.shape
    use_rope = rope is not None
    in_specs = [pl.BlockSpec((tm, d), lambda i: (i, 0)),
                pl.BlockSpec((1, MOD_ROWS, d), lambda i: ((i * tm) // tokens_per_group, 0, 0)),
                pl.BlockSpec(win_bf16.shape, lambda i: (0, 0)),
                pl.BlockSpec((1, Q_LORA), lambda i: (0, 0)),
                pl.BlockSpec((1, KV_LORA), lambda i: (0, 0)),
                pl.BlockSpec(wuq_bf16.shape, lambda i: (0, 0))]
    args = [x, mod, win_bf16, q_g, kv_g, wuq_bf16]
    if use_rope:
        tps = seq // tm
        in_specs += [pl.BlockSpec((tm, HEAD_SLAB), lambda i: (i % tps, 0))] * 3
        args += list(rope)
    return pl.pallas_call(
        functools.partial(_mla_inproj_kernel, use_rope=use_rope),
        grid=(n // tm,),
        in_specs=in_specs,
        out_specs=[pl.BlockSpec((tm, MLA_HEADS * HEAD_SLAB), lambda i: (i, 0)),
                   pl.BlockSpec((tm, KV_LORA), lambda i: (i, 0)),
                   pl.BlockSpec((tm, HEAD_SLAB), lambda i: (i, 0))],
        out_shape=[jax.ShapeDtypeStruct((n, MLA_HEADS * HEAD_SLAB), BF16),
                   jax.ShapeDtypeStruct((n, KV_LORA), F32),
                   jax.ShapeDtypeStruct((n, HEAD_SLAB), F32)],
        compiler_params=_cparams(("parallel",)),
        name="mla_inproj",
    )(*args)


def _mla_kv_kernel(ckv_ref, slab_ref, wk_ref, wv_ref, k_ref, v_ref):
    c = ckv_ref[...].astype(BF16)
    kn = _dot(c, wk_ref[...])
    slab = slab_ref[...]
    for hh in range(MLA_HEADS):
        sl = slice(hh * HEAD_SLAB, (hh + 1) * HEAD_SLAB)
        k_ref[:, sl] = (kn[:, sl] + slab).astype(BF16)
    v_ref[...] = _dot(c, wv_ref[...]).astype(BF16)


def mla_kv(ckv, slab, wk_bf16, wv_bf16, tm=256):
    n = ckv.shape[0]
    tm = min(tm, n)
    return pl.pallas_call(
        _mla_kv_kernel,
        grid=(n // tm,),
        in_specs=[pl.BlockSpec((tm, KV_LORA), lambda i: (i, 0)),
                  pl.BlockSpec((tm, HEAD_SLAB), lambda i: (i, 0)),
                  pl.BlockSpec(wk_bf16.shape, lambda i: (0, 0)),
                  pl.BlockSpec(wv_bf16.shape, lambda i: (0, 0))],
        out_specs=[pl.BlockSpec((tm, MLA_HEADS * HEAD_SLAB), lambda i: (i, 0)),
                   pl.BlockSpec((tm, MLA_HEADS * V_HEAD), lambda i: (i, 0))],
        out_shape=[jax.ShapeDtypeStruct((n, MLA_HEADS * HEAD_SLAB), BF16),
                   jax.ShapeDtypeStruct((n, MLA_HEADS * V_HEAD), BF16)],
        compiler_params=_cparams(("parallel",)),
        name="mla_kv",
    )(ckv, slab, wk_bf16, wv_bf16)


def _mla_attn_kernel(q_ref, *rest, n_sources):
    k_refs = rest[:n_sources]
    v_refs = rest[n_sources:2 * n_sources]
    o_ref = rest[2 * n_sources]
    tq = q_ref.shape[0]
    lane = lax.broadcasted_iota(jnp.int32, (tq, LANES), 1)
    outs = []
    for j in range(2):
        sl = slice(j * HEAD_SLAB, (j + 1) * HEAD_SLAB)
        q = q_ref[:, sl]
        scores = [_nt_dot(q, k[:, sl]) for k in k_refs]
        mx = functools.reduce(jnp.maximum, [jnp.max(s, axis=-1, keepdims=True) for s in scores])
        es = [jnp.exp(s - mx) for s in scores]
        den = functools.reduce(jnp.add, [jnp.sum(e, axis=-1, keepdims=True) for e in es])
        o = functools.reduce(jnp.add, [_dot(e.astype(BF16), v[...]) for e, v in zip(es, v_refs)])
        outs.append(o / den)
    o_ref[...] = jnp.where(lane < V_HEAD, outs[0], outs[1]).astype(BF16)


def mla_attention(q, k_new, v_new, k_ctx, v_ctx, n_batch, tq=256):
    n = q.shape[0]
    seq = n // n_batch
    tq = min(tq, seq)
    qt = seq // tq
    pairs = MLA_HEADS // 2
    srcs_k, srcs_v = [k_new], [v_new]
    lens = [seq]
    if k_ctx is not None:
        srcs_k.append(k_ctx)
        srcs_v.append(v_ctx)
        lens.append(k_ctx.shape[0] // n_batch)
    in_specs = [pl.BlockSpec((tq, 2 * HEAD_SLAB), lambda b, p, t: (b * qt + t, p))]
    in_specs += [pl.BlockSpec((ln, 2 * HEAD_SLAB), lambda b, p, t: (b, p)) for ln in lens]
    in_specs += [pl.BlockSpec((ln, 2 * V_HEAD), lambda b, p, t: (b, p)) for ln in lens]
    return pl.pallas_call(
        functools.partial(_mla_attn_kernel, n_sources=len(lens)),
        grid=(n_batch, pairs, qt),
        in_specs=in_specs,
        out_specs=pl.BlockSpec((tq, 2 * V_HEAD), lambda b, p, t: (b * qt + t, p)),
        out_shape=jax.ShapeDtypeStruct((n, MLA_HEADS * V_HEAD), BF16),
        compiler_params=_cparams(("parallel", "parallel", "arbitrary")),
        name="mla_attention",
    )(q, *srcs_k, *srcs_v)


def rope_lane_tables(n):
    t = jnp.arange(n)
    row = (t // GRID_W).astype(F32)
    col = (t % GRID_W).astype(F32)
    inv = ROPE_THETA ** (-jnp.arange(ROPE_PAIRS_AXIS, dtype=F32) / ROPE_PAIRS_AXIS)
    ang = jnp.concatenate([row[:, None] * inv, col[:, None] * inv], axis=-1)
    cos, sin = jnp.cos(ang), jnp.sin(ang)
    ones = jnp.ones((n, QK_NOPE), F32)
    zeros = jnp.zeros((n, QK_NOPE), F32)
    tail1 = jnp.ones((n, HEAD_SLAB - QK_NOPE - QK_ROPE), F32)
    tail0 = jnp.zeros((n, HEAD_SLAB - QK_NOPE - QK_ROPE), F32)
    z16 = jnp.zeros_like(sin)
    cos_t = jnp.concatenate([ones, cos, cos, tail1], axis=-1)
    sin_fwd = jnp.concatenate([zeros, z16, sin, tail0], axis=-1)
    sin_bwd = jnp.concatenate([zeros, -sin, z16, tail0], axis=-1)
    return cos_t, sin_fwd, sin_bwd


def _pad_mod(m, groups):
    m = m.reshape(groups, N_MOD, D_MODEL)
    return jnp.pad(m, ((0, 0), (0, MOD_ROWS - N_MOD), (0, 0)))


def _mla_weights(w_in, w_uq, w_ukv):
    pad_tail = HEAD_SLAB - QK_NOPE - QK_ROPE
    slab_cols = jnp.pad(w_in[:, Q_LORA + KV_LORA:], ((0, 0), (QK_NOPE, pad_tail)))
    win = jnp.concatenate([w_in[:, :Q_LORA + KV_LORA], slab_cols], axis=1).astype(BF16)
    wuq = jnp.pad(w_uq.reshape(Q_LORA, MLA_HEADS, QK_NOPE + QK_ROPE), ((0, 0), (0, 0), (0, pad_tail)))
    wuq = wuq.reshape(Q_LORA, MLA_HEADS * HEAD_SLAB).astype(BF16)
    wkv = w_ukv.reshape(KV_LORA, MLA_HEADS, QK_NOPE + V_HEAD)
    wk = jnp.pad(wkv[:, :, :QK_NOPE], ((0, 0), (0, 0), (0, HEAD_SLAB - QK_NOPE)))
    wk = wk.reshape(KV_LORA, MLA_HEADS * HEAD_SLAB).astype(BF16)
    wv = wkv[:, :, QK_NOPE:].reshape(KV_LORA, MLA_HEADS * V_HEAD).astype(BF16)
    return win, wuq, wk, wv


def kernel(x_prompt, x_sample, cache_na_k, cache_na_v, cache_mla_ckv, cache_mla_krope, c, c_ctx, w_ada, b_ada, ln_mix_g, ln_mix_b, ln_ffn_g, ln_ffn_b, w_in_ab, conv_w, na_rpb, w_out_ab, w_in_mla, q_norm_g, w_uq, kv_norm_g, w_ukv, w_out_mla, peer_w_q, peer_subkeys, peer_u, peer_v):
    nb_p, seq_p, d = x_prompt.shape
    nb_s, seq_s, _ = x_sample.shape
    past = cache_na_k.shape[2]
    rows = seq_s // GRID_W
    n_p = nb_p * seq_p
    xp = x_prompt.reshape(n_p, d)
    xs = x_sample.reshape(nb_s * seq_s, d)

    n_cond = 1 + nb_s
    cond_rows = -(-n_cond // 8) * 8
    cvec = jnp.concatenate([c_ctx[None], c, jnp.zeros((cond_rows - n_cond, d), F32)], axis=0)

    outs = {}
    for i in range(DEPTH):
        j = i // 2
        mods = ada_modulation(cvec, w_ada[i], b_ada[i])
        mod_p = _pad_mod(mods[0:1], 1)
        mod_s = _pad_mod(mods[1:n_cond], nb_s)
        lg = ln_mix_g[i].reshape(1, d)
        lb = ln_mix_b[i].reshape(1, d)
        if i % 2 == 0:
            w_in = w_in_ab[j].astype(BF16)
            w_out = w_out_ab[j].astype(BF16)
            cw = jnp.pad(conv_w[j], ((0, 8 - CONV_TAPS), (0, 0)))
            qkv_p, gu_p, kv_p = inproj_ab(xp, mod_p, w_in, n_p, True)
            qkv_s, gu_s = inproj_ab(xs, mod_s, w_in, seq_s, False)
            outs["na_k"] = kv_p[:, :NA_WIDTH].reshape(nb_p, 1, seq_p, NA_HEADS, NA_HEAD_DIM)
            outs["na_v"] = kv_p[:, NA_WIDTH:].reshape(nb_p, 1, seq_p, NA_HEADS, NA_HEAD_DIM)
            att_p = ctx_attention(qkv_p, seq_p)
            kc = cache_na_k[:, j].reshape(nb_s * past, NA_WIDTH).astype(BF16)
            vc = cache_na_v[:, j].reshape(nb_s * past, NA_WIDTH).astype(BF16)
            att_s = na_attention(qkv_s, kc, vc, na_bias_table(na_rpb[j]), nb_s, rows)
            xp = outproj_ab(att_p, gu_p, xp, mod_p, w_out, cw, lg, lb, n_p, seq_p)
            xs = outproj_ab(att_s, gu_s, xs, mod_s, w_out, cw, lg, lb, seq_s, seq_s)
        else:
            win, wuq, wk, wv = _mla_weights(w_in_mla[j], w_uq[j], w_ukv[j])
            w_out = w_out_mla[j].astype(BF16)
            qg = q_norm_g[j].reshape(1, Q_LORA)
            kvg = kv_norm_g[j].reshape(1, KV_LORA)
            q_p, ckv_p, slab_p = mla_inproj(xp, mod_p, win, qg, kvg, wuq, None, n_p, seq_p)
            q_s, ckv_s, slab_s = mla_inproj(xs, mod_s, win, qg, kvg, wuq, rope_lane_tables(seq_s), seq_s, seq_s)
            outs["ckv"] = ckv_p.reshape(nb_p, 1, seq_p, KV_LORA)
            outs["kr"] = slab_p[:, QK_NOPE:QK_NOPE + QK_ROPE].reshape(nb_p, 1, seq_p, QK_ROPE)
            k_p, v_p = mla_kv(ckv_p, slab_p, wk, wv)
            k_s, v_s = mla_kv(ckv_s, slab_s, wk, wv)
            ckv_c = cache_mla_ckv[:, j].reshape(nb_s * past, KV_LORA)
            slab_c = jnp.pad(cache_mla_krope[:, j].reshape(nb_s * past, QK_ROPE),
                             ((0, 0), (QK_NOPE, HEAD_SLAB - QK_NOPE - QK_ROPE)))
            k_c, v_c = mla_kv(ckv_c, slab_c, wk, wv)
            att_p = mla_attention(q_p, k_p, v_p, None, None, nb_p)
            att_s = mla_attention(q_s, k_s, v_s, k_c, v_c, nb_s)
            xp = outproj(att_p, xp, mod_p, w_out, lg, lb, n_p)
            xs = outproj(att_s, xs, mod_s, w_out, lg, lb, seq_s)
        wq = peer_w_q[i].astype(BF16)
        sk = peer_subkeys[i].reshape(2 * PEER_HEADS, N_KEYS, PEER_HALF).astype(BF16)
        u = peer_u[i].astype(BF16)
        vt = peer_v[i].T.astype(BF16)
        fg = ln_ffn_g[i].reshape(1, d)
        fb = ln_ffn_b[i].reshape(1, d)
        xp = peer_block(xp, mod_p, wq, sk, u, vt, fg, fb, n_p)
        xs = peer_block(xs, mod_s, wq, sk, u, vt, fg, fb, seq_s)
    return (xp.reshape(nb_p, seq_p, d), xs.reshape(nb_s, seq_s, d),
            outs["na_k"], outs["na_v"], outs["ckv"], outs["kr"])
```

```python
import functools

import jax
import jax.numpy as jnp
import numpy as np
from jax import lax
from jax.experimental import pallas as pl
from jax.experimental.pallas import tpu as pltpu

F32 = jnp.float32
BF16 = jnp.bfloat16

D_MODEL = 1024
DEPTH = 2
GRID_W = 64
N_MOD = 6
ALPHA = (2.0 * DEPTH) ** 0.25
LN_EPS = 1e-5
RMS_EPS = 1e-6
NA_HEADS = 8
NA_HEAD_DIM = 64
NA_WIDTH = NA_HEADS * NA_HEAD_DIM
NA_MAX_ROWS = 8
NA_WIN_COLS = 16
NA_SCALE = NA_HEAD_DIM ** -0.5
CONV_WIDTH = 512
CONV_TAPS = 3
MLA_HEADS = 16
Q_LORA = 256
KV_LORA = 256
QK_NOPE = 64
QK_ROPE = 32
V_HEAD = 64
MLA_SCALE = (QK_NOPE + QK_ROPE) ** -0.5
ROPE_THETA = 10000.0
ROPE_PAIRS_AXIS = QK_ROPE // 4
PEER_HEADS = 8
PEER_HALF = 128
N_KEYS = 128
N_EXPERTS = N_KEYS * N_KEYS
PEER_TOPK = 16

LANES = 128
MOD_ROWS = 8
HEAD_SLAB = 128
MASK_BIAS = -1e30
VMEM_LIMIT = 52 * 1024 * 1024


def _cparams(sem, flags=None):
    return pltpu.CompilerParams(dimension_semantics=sem, vmem_limit_bytes=VMEM_LIMIT, flags=flags)


def _nt_dot(a, b):
    return lax.dot_general(a, b, (((1,), (1,)), ((), ())), preferred_element_type=F32)


def _dot(a, b):
    return jnp.dot(a, b, preferred_element_type=F32)


def _split_bf16(a):
    hi = a.astype(BF16)
    lo = (a - hi.astype(F32)).astype(BF16)
    return hi, lo


def _layer_norm(z, g, b):
    mu = jnp.mean(z, axis=-1, keepdims=True)
    zc = z - mu
    var = jnp.mean(zc * zc, axis=-1, keepdims=True)
    return zc * lax.rsqrt(var + LN_EPS) * g + b


LOG2E = 1.4426950408889634
LN2 = 0.6931471805599453
_GELU_A = -2.0 * LOG2E * 0.7978845608028654
_GELU_B = _GELU_A * 0.044715


def _gelu_tanh(x):
    return x / (1.0 + jnp.exp2(x * (_GELU_A + _GELU_B * (x * x))))


def _ada_kernel(c_ref, w_ref, b_ref, o_ref):
    c = c_ref[...]
    a = c * (1.0 / (1.0 + jnp.exp(-c)))
    a_hi, a_lo = _split_bf16(a)
    w_hi, w_lo = _split_bf16(w_ref[...])
    o_ref[...] = _dot(a_hi, w_hi) + _dot(a_hi, w_lo) + _dot(a_lo, w_hi) + b_ref[...]


def ada_modulation(cvec, w_ada, b_ada, tn=512):
    r, d = cvec.shape
    n = w_ada.shape[1]
    return pl.pallas_call(
        _ada_kernel,
        grid=(n // tn,),
        in_specs=[pl.BlockSpec((r, d), lambda j: (0, 0)),
                  pl.BlockSpec((d, tn), lambda j: (0, j)),
                  pl.BlockSpec((1, tn), lambda j: (0, j))],
        out_specs=pl.BlockSpec((r, tn), lambda j: (0, j)),
        out_shape=jax.ShapeDtypeStruct((r, n), F32),
        compiler_params=_cparams(("arbitrary",)),
        name="ada_modulation",
    )(cvec, w_ada, b_ada.reshape(1, n))


def _inproj_ab_kernel(x_ref, mod_ref, w_ref, qkv_ref, gu_ref, *kv_ref):
    m = mod_ref[0]
    h = x_ref[...] * (1.0 + m[1:2]) + m[0:1]
    y = _dot(h.astype(BF16), w_ref[...])
    w3 = 3 * NA_WIDTH
    qkv_ref[:, :NA_WIDTH] = (y[:, :NA_WIDTH] * NA_SCALE).astype(BF16)
    qkv_ref[:, NA_WIDTH:] = y[:, NA_WIDTH:w3].astype(BF16)
    if kv_ref:
        kv_ref[0][...] = y[:, NA_WIDTH:w3]
    gu_ref[:, :CONV_WIDTH] = y[:, w3:w3 + CONV_WIDTH].astype(BF16)
    gu_ref[:, CONV_WIDTH:] = (y[:, w3 + CONV_WIDTH:w3 + 2 * CONV_WIDTH]
                              * y[:, w3 + 2 * CONV_WIDTH:]).astype(BF16)


def inproj_ab(x, mod, w_bf16, tokens_per_group, emit_kv, tm=512):
    n, d = x.shape
    tm = min(tm, n)
    out_shape = [jax.ShapeDtypeStruct((n, 3 * NA_WIDTH), BF16),
                 jax.ShapeDtypeStruct((n, 2 * CONV_WIDTH), BF16)]
    out_specs = [pl.BlockSpec((tm, 3 * NA_WIDTH), lambda i: (i, 0)),
                 pl.BlockSpec((tm, 2 * CONV_WIDTH), lambda i: (i, 0))]
    if emit_kv:
        out_shape.append(jax.ShapeDtypeStruct((n, 2 * NA_WIDTH), F32))
        out_specs.append(pl.BlockSpec((tm, 2 * NA_WIDTH), lambda i: (i, 0)))
    return pl.pallas_call(
        _inproj_ab_kernel,
        grid=(n // tm,),
        in_specs=[pl.BlockSpec((tm, d), lambda i: (i, 0)),
                  pl.BlockSpec((1, MOD_ROWS, d), lambda i: ((i * tm) // tokens_per_group, 0, 0)),
                  pl.BlockSpec(w_bf16.shape, lambda i: (0, 0))],
        out_specs=out_specs,
        out_shape=out_shape,
        compiler_params=_cparams(("parallel",)),
        name="inproj_ab",
    )(x, mod, w_bf16)


def _pair_attention(q_pair, keys, values, biases):
    lane = lax.broadcasted_iota(jnp.int32, q_pair.shape, 1)
    qf = q_pair.astype(F32)
    outs = []
    for j in range(2):
        in_head = (lane >= j * NA_HEAD_DIM) & (lane < (j + 1) * NA_HEAD_DIM)
        qm = jnp.where(in_head, qf, 0.0).astype(BF16)
        scores = []
        for t, k in enumerate(keys):
            s = _nt_dot(qm, k)
            if biases[j][t] is not None:
                s = s + biases[j][t]
            scores.append(s)
        mx = functools.reduce(jnp.maximum, [jnp.max(s, axis=-1, keepdims=True) for s in scores])
        es = [jnp.exp(s - mx) for s in scores]
        den = functools.reduce(jnp.add, [jnp.sum(e, axis=-1, keepdims=True) for e in es])
        o = functools.reduce(jnp.add, [_dot(e.astype(BF16), v) for e, v in zip(es, values)])
        outs.append(o / den)
    return jnp.where(lane < NA_HEAD_DIM, outs[0], outs[1])


def _na_attn_kernel(q_ref, k_ref, v_ref, kc_ref, vc_ref, bias_ref, o_ref, *, rows):
    r = pl.program_id(1)
    rs = jnp.clip(r - NA_MAX_ROWS // 2, 0, rows - NA_MAX_ROWS)
    start = pl.multiple_of(rs * GRID_W, GRID_W)
    band = NA_MAX_ROWS * GRID_W
    for p in range(NA_HEADS // 2):
        sl = slice(p * LANES, (p + 1) * LANES)
        keys = [k_ref[pl.ds(start, band), sl], kc_ref[:, sl]]
        values = [v_ref[pl.ds(start, band), sl], vc_ref[:, sl]]
        biases = [[bias_ref[0, 2 * p + j], None] for j in range(2)]
        o_ref[:, sl] = _pair_attention(q_ref[:, sl], keys, values, biases).astype(BF16)


def na_attention(qkv, kc, vc, bias, n_batch, rows):
    n = qkv.shape[0]
    seq = rows * GRID_W
    lc = kc.shape[0] // n_batch
    half = NA_MAX_ROWS // 2

    def bias_idx(b, r):
        return (r - jnp.clip(r - half, 0, rows - NA_MAX_ROWS), 0, 0, 0)

    return pl.pallas_call(
        functools.partial(_na_attn_kernel, rows=rows),
        grid=(n_batch, rows),
        in_specs=[pl.BlockSpec((GRID_W, NA_WIDTH), lambda b, r: (b * rows + r, 0)),
                  pl.BlockSpec((seq, NA_WIDTH), lambda b, r: (b, 1)),
                  pl.BlockSpec((seq, NA_WIDTH), lambda b, r: (b, 2)),
                  pl.BlockSpec((lc, NA_WIDTH), lambda b, r: (b, 0)),
                  pl.BlockSpec((lc, NA_WIDTH), lambda b, r: (b, 0)),
                  pl.BlockSpec((1, NA_HEADS, GRID_W, NA_MAX_ROWS * GRID_W), bias_idx)],
        out_specs=pl.BlockSpec((GRID_W, NA_WIDTH), lambda b, r: (b * rows + r, 0)),
        out_shape=jax.ShapeDtypeStruct((n, NA_WIDTH), BF16),
        compiler_params=_cparams(("parallel", "arbitrary")),
        name="na_attention",
    )(qkv, qkv, qkv, kc, vc, bias)


def na_bias_table(rpb):
    c = np.arange(GRID_W)[:, None]
    kc = np.arange(GRID_W)[None, :]
    cs = np.clip(c - NA_WIN_COLS // 2, 0, GRID_W - NA_WIN_COLS)
    valid = (kc >= cs) & (kc < cs + NA_WIN_COLS)
    dc = kc - c + (NA_WIN_COLS - 1)
    cc, kk = np.nonzero(valid)
    place = np.zeros((2 * NA_WIN_COLS - 1, GRID_W, GRID_W), np.float32)
    place[dc[cc, kk], cc, kk] = 1.0
    t = jnp.einsum("hrd,dck->hrck", rpb, place, precision=lax.Precision.HIGHEST)
    t = jnp.where(valid[None, None], t, MASK_BIAS)
    bands = [t[:, NA_MAX_ROWS - 1 - off:2 * NA_MAX_ROWS - 1 - off] for off in range(NA_MAX_ROWS)]
    b = jnp.stack(bands, axis=0)
    b = jnp.transpose(b, (0, 1, 3, 2, 4))
    return b.reshape(NA_MAX_ROWS, NA_HEADS, GRID_W, NA_MAX_ROWS * GRID_W).astype(F32)


def _ctx_attn_kernel(q_ref, k_ref, v_ref, o_ref):
    for p in range(NA_HEADS // 2):
        sl = slice(p * LANES, (p + 1) * LANES)
        o_ref[:, sl] = _pair_attention(q_ref[:, sl], [k_ref[:, sl]], [v_ref[:, sl]],
                                       [[None], [None]]).astype(BF16)


def ctx_attention(qkv, seq):
    n = qkv.shape[0]
    return pl.pallas_call(
        _ctx_attn_kernel,
        grid=(n // seq,),
        in_specs=[pl.BlockSpec((seq, NA_WIDTH), lambda b: (b, 0)),
                  pl.BlockSpec((seq, NA_WIDTH), lambda b: (b, 1)),
                  pl.BlockSpec((seq, NA_WIDTH), lambda b: (b, 2))],
        out_specs=pl.BlockSpec((seq, NA_WIDTH), lambda b: (b, 0)),
        out_shape=jax.ShapeDtypeStruct((n, NA_WIDTH), BF16),
        compiler_params=_cparams(("parallel",)),
        name="ctx_attention",
    )(qkv, qkv, qkv)


HALO = 16


def _outproj_ab_kernel(att_ref, gu_ref, prev_ref, next_ref, x_ref, mod_ref, w_ref, cw_ref, g_ref, b_ref,
                       o_ref, *, tiles_per_seq):
    i = pl.program_id(0)
    tm = x_ref.shape[0]
    m = mod_ref[0]
    gu = gu_ref[...].astype(F32)
    bg = gu[:, :CONV_WIDTH]
    u = gu[:, CONV_WIDTH:]
    has_prev = (i % tiles_per_seq != 0).astype(F32)
    has_next = (i % tiles_per_seq != tiles_per_seq - 1).astype(F32)
    prev_row = prev_ref[HALO - 1:HALO, CONV_WIDTH:].astype(F32) * has_prev
    next_row = next_ref[0:1, CONV_WIDTH:].astype(F32) * has_next
    row = lax.broadcasted_iota(jnp.int32, u.shape, 0)
    u_m1 = jnp.where(row == 0, prev_row, pltpu.roll(u, 1, axis=0))
    u_p1 = jnp.where(row == tm - 1, next_row, pltpu.roll(u, tm - 1, axis=0))
    cw = cw_ref[...]
    conv = bg * (u_m1 * cw[0:1] + u * cw[1:2] + u_p1 * cw[2:3])
    y = _dot(att_ref[...], w_ref[:NA_WIDTH, :]) + _dot(conv.astype(BF16), w_ref[NA_WIDTH:, :])
    z = ALPHA * x_ref[...] + m[2:3] * y
    o_ref[...] = _layer_norm(z, g_ref[...], b_ref[...])


def outproj_ab(att, gu, x, mod, w_bf16, conv_w8, ln_g, ln_b, tokens_per_group, seq, tm=256):
    n, d = x.shape
    tm = min(tm, seq)
    hb = tm // HALO
    last_halo = n // HALO - 1
    return pl.pallas_call(
        functools.partial(_outproj_ab_kernel, tiles_per_seq=seq // tm),
        grid=(n // tm,),
        in_specs=[pl.BlockSpec((tm, NA_WIDTH), lambda i: (i, 0)),
                  pl.BlockSpec((tm, 2 * CONV_WIDTH), lambda i: (i, 0)),
                  pl.BlockSpec((HALO, 2 * CONV_WIDTH), lambda i: (jnp.maximum(i * hb - 1, 0), 0)),
                  pl.BlockSpec((HALO, 2 * CONV_WIDTH), lambda i: (jnp.minimum((i + 1) * hb, last_halo), 0)),
                  pl.BlockSpec((tm, d), lambda i: (i, 0)),
                  pl.BlockSpec((1, MOD_ROWS, d), lambda i: ((i * tm) // tokens_per_group, 0, 0)),
                  pl.BlockSpec(w_bf16.shape, lambda i: (0, 0)),
                  pl.BlockSpec(conv_w8.shape, lambda i: (0, 0)),
                  pl.BlockSpec((1, d), lambda i: (0, 0)),
                  pl.BlockSpec((1, d), lambda i: (0, 0))],
        out_specs=pl.BlockSpec((tm, d), lambda i: (i, 0)),
        out_shape=jax.ShapeDtypeStruct((n, d), F32),
        compiler_params=_cparams(("parallel",)),
        name="outproj_ab",
    )(att, gu, gu, gu, x, mod, w_bf16, conv_w8, ln_g, ln_b)


def _outproj_kernel(att_ref, x_ref, mod_ref, w_ref, g_ref, b_ref, o_ref):
    m = mod_ref[0]
    y = _dot(att_ref[...], w_ref[...])
    z = ALPHA * x_ref[...] + m[2:3] * y
    o_ref[...] = _layer_norm(z, g_ref[...], b_ref[...])


def outproj(att, x, mod, w_bf16, ln_g, ln_b, tokens_per_group, tm=512):
    n, d = x.shape
    k = att.shape[1]
    return pl.pallas_call(
        _outproj_kernel,
        grid=(n // tm,),
        in_specs=[pl.BlockSpec((tm, k), lambda i: (i, 0)),
                  pl.BlockSpec((tm, d), lambda i: (i, 0)),
                  pl.BlockSpec((1, MOD_ROWS, d), lambda i: ((i * tm) // tokens_per_group, 0, 0)),
                  pl.BlockSpec(w_bf16.shape, lambda i: (0, 0)),
                  pl.BlockSpec((1, d), lambda i: (0, 0)),
                  pl.BlockSpec((1, d), lambda i: (0, 0))],
        out_specs=pl.BlockSpec((tm, d), lambda i: (i, 0)),
        out_shape=jax.ShapeDtypeStruct((n, d), F32),
        compiler_params=_cparams(("parallel",)),
        name="outproj",
    )(att, x, mod, w_bf16, ln_g, ln_b)


def _peer_scores_kernel(x_ref, mod_ref, wq_ref, sk_ref, hm_ref, st_ref):
    m = mod_ref[0]
    hb = (x_ref[...] * (1.0 + m[4:5]) + m[3:4]).astype(BF16)
    hm_ref[...] = hb
    q = _dot(hb, wq_ref[...])
    for c in range(2 * PEER_HEADS):
        qc = q[:, c * PEER_HALF:(c + 1) * PEER_HALF].astype(BF16)
        st_ref[c] = _nt_dot(sk_ref[c], qc) * LOG2E


def peer_scores(x, mod, wq_bf16, subkeys_bf16, tokens_per_group, tm=512):
    n, d = x.shape
    nc = 2 * PEER_HEADS
    return pl.pallas_call(
        _peer_scores_kernel,
        grid=(n // tm,),
        in_specs=[pl.BlockSpec((tm, d), lambda i: (i, 0)),
                  pl.BlockSpec((1, MOD_ROWS, d), lambda i: ((i * tm) // tokens_per_group, 0, 0)),
                  pl.BlockSpec(wq_bf16.shape, lambda i: (0, 0)),
                  pl.BlockSpec(subkeys_bf16.shape, lambda i: (0, 0, 0))],
        out_specs=[pl.BlockSpec((tm, d), lambda i: (i, 0)),
                   pl.BlockSpec((nc, N_KEYS, tm), lambda i: (0, 0, i))],
        out_shape=[jax.ShapeDtypeStruct((n, d), BF16),
                   jax.ShapeDtypeStruct((nc, N_KEYS, n), F32)],
        compiler_params=_cparams(("parallel",)),
        name="peer_scores",
    )(x, mod, wq_bf16, subkeys_bf16)


SUBLANES = 8
N_SLABS = N_KEYS // SUBLANES


def _sort_network(n):
    pairs = []
    p = 1
    while p < n:
        k = p
        while k >= 1:
            for j in range(k % p, n - k, 2 * k):
                for i in range(min(k, n - j - k)):
                    if (i + j) // (2 * p) == (i + j + k) // (2 * p):
                        pairs.append((i + j, i + j + k))
            k //= 2
        p *= 2
    return pairs


_SLAB_SORT = _sort_network(N_SLABS)


def _top_values(s_ref, c, n_top):
    x = [s_ref[c, r] for r in range(N_SLABS)]
    for a, b in _SLAB_SORT:
        x[a], x[b] = jnp.maximum(x[a], x[b]), jnp.minimum(x[a], x[b])
    sub = lax.broadcasted_iota(jnp.int32, x[0].shape, 0).astype(F32)
    neg = jnp.full(x[0].shape, -jnp.inf, F32)
    x.append(neg)
    vals = []
    for k in range(n_top):
        mx = jnp.max(x[0], axis=0, keepdims=True)
        vals.append(mx)
        if k == n_top - 1:
            break
        first = jnp.min(jnp.where(x[0] == mx, sub, float(SUBLANES)), axis=0, keepdims=True)
        pop = sub == first
        for r in range(n_top - 1 - k):
            x[r] = jnp.where(pop, x[r + 1], x[r])
    return vals


def _rows_to_slab(rows_list):
    t = rows_list[0].shape[1]
    ridx = lax.broadcasted_iota(jnp.int32, (SUBLANES, t), 0)
    out = jnp.full((SUBLANES, t), -jnp.inf, F32)
    for r, row in enumerate(rows_list):
        out = jnp.where(ridx == r, row, out)
    return out


def _peer_select_kernel(st_ref, thr_ref, lse_ref, m2_ref):
    k = PEER_TOPK
    for h in range(PEER_HEADS):
        v1 = _top_values(st_ref, 2 * h, k + 1)
        v2 = _top_values(st_ref, 2 * h + 1, k + 1)
        v1_lo, v1_hi = _rows_to_slab(v1[0:8]), _rows_to_slab(v1[8:16])
        v2_lo, v2_hi = _rows_to_slab(v2[0:8]), _rows_to_slab(v2[8:16])
        ridx = lax.broadcasted_iota(jnp.int32, v1_lo.shape, 0)
        v1_mid = jnp.where(ridx >= 4, v1_lo, -jnp.inf)
        cands = [v1[0] + v2_lo, v1[0] + v2_hi, v1[1] + v2_lo, v1[2] + v2_lo, v1[3] + v2_lo,
                 v1_hi + v2[0], v1_mid + v2[0], v1_mid + v2[1], v1_mid + v2[2],
                 _rows_to_slab([v1[k] + v2[0], v1[0] + v2[k]])]
        top = v1[0] + v2[0]
        work = list(cands)
        thr = jnp.full_like(top, -jnp.inf)
        nxt = jnp.full_like(top, -jnp.inf)
        seen = jnp.zeros_like(top)
        for _ in range(k + 1):
            mx = jnp.max(functools.reduce(jnp.maximum, work), axis=0, keepdims=True)
            eqs = [w == mx for w in work]
            cnt = functools.reduce(jnp.add, [jnp.where(e, 1.0, 0.0) for e in eqs])
            thr = jnp.where(seen < k, mx, thr)
            nxt = jnp.where(seen < k + 1, mx, nxt)
            seen = seen + jnp.sum(cnt, axis=0, keepdims=True)
            work = [jnp.where(e, -jnp.inf, w) for e, w in zip(eqs, work)]
        z = functools.reduce(jnp.add, [jnp.where(cd >= thr, jnp.exp2(cd - top), 0.0) for cd in cands])
        z = jnp.sum(z, axis=0, keepdims=True)
        thr_ref[h:h + 1, :] = 0.5 * (thr + nxt)
        lse_ref[h:h + 1, :] = v1[0] + jnp.log(z) * LOG2E
        m2_ref[h:h + 1, :] = v2[0]


def peer_select(st, tl=256):
    nc, _, _, n = st.shape
    stat = pl.BlockSpec((PEER_HEADS, tl), lambda i: (0, i))
    return pl.pallas_call(
        _peer_select_kernel,
        grid=(n // tl,),
        in_specs=[pl.BlockSpec((nc, N_SLABS, SUBLANES, tl), lambda i: (0, 0, 0, i))],
        out_specs=[stat, stat, stat],
        out_shape=[jax.ShapeDtypeStruct((PEER_HEADS, n), F32)] * 3,
        compiler_params=_cparams(("parallel",)),
        name="peer_select",
    )(st)


PEER_LANE_CHUNK = 128
PEER_MXU_CHUNK = 256


def _peer_dense_kernel(hm_ref, u_ref, vt_ref, st_ref, thr_ref, lse_ref, m2_ref, x_ref, mod_ref, g_ref, b_ref,
                       o_ref, acc_ref, ht0_ref, ht1_ref, at0_ref, at1_ref, e2_ref, cut_rep, f1_rep,
                       *, gate_row, te):
    g = pl.program_id(1)
    tm = hm_ref.shape[0]
    n_tiles = 2 * (pl.num_programs(1) - 1)
    hts = (ht0_ref, ht1_ref)
    ats = (at0_ref, at1_ref)

    @pl.when(g == 0)
    def _():
        acc_ref[...] = jnp.zeros_like(acc_ref)
        ht1_ref[...] = jnp.zeros_like(ht1_ref)
        at0_ref[...] = jnp.zeros_like(at0_ref)
        for h in range(PEER_HEADS):
            e2_ref[h] = jnp.exp2(st_ref[2 * h + 1] - m2_ref[h:h + 1, :])

    for sub in range(2):
        cur, prev = sub, 1 - sub
        half = slice(sub * te, (sub + 1) * te)
        n_chunks = tm // PEER_LANE_CHUNK

        def stage_a(c):
            ls = slice(c * PEER_MXU_CHUNK, (c + 1) * PEER_MXU_CHUNK)
            hts[cur][:, ls] = _nt_dot(u_ref[half, :], hm_ref[ls, :])

        def stage_c(c):
            ls = slice(c * PEER_MXU_CHUNK, (c + 1) * PEER_MXU_CHUNK)
            acc_ref[:, ls] += _dot(vt_ref[:, half], ats[cur][:, ls])

        mxu_pieces = [functools.partial(f, c) for c in range(tm // PEER_MXU_CHUNK) for f in (stage_a, stage_c)]
        units_per_piece = SUBLANES * n_chunks // len(mxu_pieces)

        tile = jnp.clip(2 * g + sub - 1, 0, n_tiles - 1)
        for h in range(PEER_HEADS):
            s1 = st_ref[2 * h, tile]
            cut_rows = thr_ref[h:h + 1, :] - s1
            f1_rows = jnp.exp2(s1 - lse_ref[h:h + 1, :])
            for ii in range(SUBLANES):
                cut_rep[h * SUBLANES + ii] = jnp.broadcast_to(cut_rows[ii:ii + 1, :], (SUBLANES, tm))
                f1_rep[h * SUBLANES + ii] = jnp.broadcast_to(f1_rows[ii:ii + 1, :], (SUBLANES, tm))
        for ii in range(SUBLANES):
            rows = slice(ii * N_KEYS, (ii + 1) * N_KEYS)
            for c in range(n_chunks):
                unit = ii * n_chunks + c
                if unit % units_per_piece == 0:
                    mxu_pieces[unit // units_per_piece]()
                ls = slice(c * PEER_LANE_CHUNK, (c + 1) * PEER_LANE_CHUNK)
                w = jnp.zeros((N_SLABS, SUBLANES, PEER_LANE_CHUNK), F32)
                for h in range(PEER_HEADS):
                    sel = st_ref[2 * h + 1, :, :, ls] >= cut_rep[h * SUBLANES + ii, :, ls][None]
                    w = w + jnp.where(sel, e2_ref[h, :, :, ls] * f1_rep[h * SUBLANES + ii, :, ls][None], 0.0)
                act = _gelu_tanh(hts[prev][rows, ls])
                ats[prev][rows, ls] = (act * w.reshape(N_KEYS, PEER_LANE_CHUNK)).astype(BF16)

    @pl.when(g == pl.num_programs(1) - 1)
    def _():
        m = mod_ref[0]
        z = ALPHA * x_ref[...] + m[gate_row:gate_row + 1] * acc_ref[...].T
        o_ref[...] = _layer_norm(z, g_ref[...], b_ref[...])


def peer_dense(hm, u_bf16, vt_bf16, st, thr, lse, m2, x, mod, ln_g, ln_b, tokens_per_group, tm=512, te=1024):
    n, d = x.shape
    ne = u_bf16.shape[0]
    nc = st.shape[0]
    assert te == N_KEYS * SUBLANES
    n_pairs = ne // (2 * te)
    return pl.pallas_call(
        functools.partial(_peer_dense_kernel, gate_row=5, te=te),
        grid=(n // tm, n_pairs + 1),
        in_specs=[pl.BlockSpec((tm, d), lambda i, g: (i, 0)),
                  pl.BlockSpec((2 * te, d), lambda i, g: (jnp.minimum(g, n_pairs - 1), 0)),
                  pl.BlockSpec((d, 2 * te), lambda i, g: (0, jnp.maximum(g - 1, 0))),
                  pl.BlockSpec((nc, N_SLABS, SUBLANES, tm), lambda i, g: (0, 0, 0, i)),
                  pl.BlockSpec((PEER_HEADS, tm), lambda i, g: (0, i)),
                  pl.BlockSpec((PEER_HEADS, tm), lambda i, g: (0, i)),
                  pl.BlockSpec((PEER_HEADS, tm), lambda i, g: (0, i)),
                  pl.BlockSpec((tm, d), lambda i, g: (i, 0)),
                  pl.BlockSpec((1, MOD_ROWS, d), lambda i, g: ((i * tm) // tokens_per_group, 0, 0)),
                  pl.BlockSpec((1, d), lambda i, g: (0, 0)),
                  pl.BlockSpec((1, d), lambda i, g: (0, 0))],
        out_specs=pl.BlockSpec((tm, d), lambda i, g: (i, 0)),
        out_shape=jax.ShapeDtypeStruct((n, d), F32),
        scratch_shapes=[pltpu.VMEM((d, tm), F32),
                        pltpu.VMEM((te, tm), F32), pltpu.VMEM((te, tm), F32),
                        pltpu.VMEM((te, tm), BF16), pltpu.VMEM((te, tm), BF16),
                        pltpu.VMEM((PEER_HEADS, N_SLABS, SUBLANES, tm), F32),
                        pltpu.VMEM((PEER_HEADS * SUBLANES, SUBLANES, tm), F32),
                        pltpu.VMEM((PEER_HEADS * SUBLANES, SUBLANES, tm), F32)],
        compiler_params=_cparams(("parallel", "arbitrary")),
        name="peer_dense",
    )(hm, u_bf16, vt_bf16, st, thr, lse, m2, x, mod, ln_g, ln_b)


def peer_block(x, mod, wq, sk, u, vt, ln_g, ln_b, tokens_per_group):
    hm, st = peer_scores(x, mod, wq, sk, tokens_per_group)
    st = st.reshape(2 * PEER_HEADS, N_SLABS, SUBLANES, x.shape[0])
    cut, lse, m2 = peer_select(st)
    return peer_dense(hm, u, vt, st, cut, lse, m2, x, mod, ln_g, ln_b, tokens_per_group)


def _rope_slab(t, cos_t, sin_fwd, sin_bwd):
    return t * cos_t + pltpu.roll(t, QK_ROPE // 2, axis=1) * sin_fwd \
        + pltpu.roll(t, HEAD_SLAB - QK_ROPE // 2, axis=1) * sin_bwd


def _mla_inproj_kernel(x_ref, mod_ref, win_ref, qg_ref, kvg_ref, wuq_ref, *rest, use_rope):
    if use_rope:
        cos_ref, sf_ref, sb_ref, q_ref, ckv_ref, slab_ref = rest
    else:
        q_ref, ckv_ref, slab_ref = rest
    m = mod_ref[0]
    h = x_ref[...] * (1.0 + m[1:2]) + m[0:1]
    y = _dot(h.astype(BF16), win_ref[...])
    cq = y[:, :Q_LORA]
    ckv = y[:, Q_LORA:Q_LORA + KV_LORA]
    slab = y[:, Q_LORA + KV_LORA:]
    cqn = cq * lax.rsqrt(jnp.mean(cq * cq, axis=-1, keepdims=True) + RMS_EPS) * qg_ref[...]
    ckv_ref[...] = ckv * lax.rsqrt(jnp.mean(ckv * ckv, axis=-1, keepdims=True) + RMS_EPS) * kvg_ref[...]
    q = _dot(cqn.astype(BF16), wuq_ref[...]) * MLA_SCALE
    if use_rope:
        cos_t, sf, sb = cos_ref[...], sf_ref[...], sb_ref[...]
        slab = _rope_slab(slab, cos_t, sf, sb)
        for hh in range(MLA_HEADS):
            sl = slice(hh * HEAD_SLAB, (hh + 1) * HEAD_SLAB)
            q_ref[:, sl] = _rope_slab(q[:, sl], cos_t, sf, sb).astype(BF16)
    else:
        q_ref[...] = q.astype(BF16)
    slab_ref[...] = slab


def mla_inproj(x, mod, win_bf16, q_g, kv_g, wuq_bf16, rope, tokens_per_group, seq, tm=256):
    n, d = x.shape
    use_rope = rope is not None
    in_specs = [pl.BlockSpec((tm, d), lambda i: (i, 0)),
                pl.BlockSpec((1, MOD_ROWS, d), lambda i: ((i * tm) // tokens_per_group, 0, 0)),
                pl.BlockSpec(win_bf16.shape, lambda i: (0, 0)),
                pl.BlockSpec((1, Q_LORA), lambda i: (0, 0)),
                pl.BlockSpec((1, KV_LORA), lambda i: (0, 0)),
                pl.BlockSpec(wuq_bf16.shape, lambda i: (0, 0))]
    args = [x, mod, win_bf16, q_g, kv_g, wuq_bf16]
    if use_rope:
        tps = seq // tm
        in_specs += [pl.BlockSpec((tm, HEAD_SLAB), lambda i: (i % tps, 0))] * 3
        args += list(rope)
    return pl.pallas_call(
        functools.partial(_mla_inproj_kernel, use_rope=use_rope),
        grid=(n // tm,),
        in_specs=in_specs,
        out_specs=[pl.BlockSpec((tm, MLA_HEADS * HEAD_SLAB), lambda i: (i, 0)),
                   pl.BlockSpec((tm, KV_LORA), lambda i: (i, 0)),
                   pl.BlockSpec((tm, HEAD_SLAB), lambda i: (i, 0))],
        out_shape=[jax.ShapeDtypeStruct((n, MLA_HEADS * HEAD_SLAB), BF16),
                   jax.ShapeDtypeStruct((n, KV_LORA), F32),
                   jax.ShapeDtypeStruct((n, HEAD_SLAB), F32)],
        compiler_params=_cparams(("parallel",)),
        name="mla_inproj",
    )(*args)


def _mla_kv_kernel(ckv_ref, slab_ref, wk_ref, wv_ref, k_ref, v_ref):
    c = ckv_ref[...].astype(BF16)
    kn = _dot(c, wk_ref[...])
    slab = slab_ref[...]
    for hh in range(MLA_HEADS):
        sl = slice(hh * HEAD_SLAB, (hh + 1) * HEAD_SLAB)
        k_ref[:, sl] = (kn[:, sl] + slab).astype(BF16)
    v_ref[...] = _dot(c, wv_ref[...]).astype(BF16)


def mla_kv(ckv, slab, wk_bf16, wv_bf16, tm=256):
    n = ckv.shape[0]
    tm = min(tm, n)
    return pl.pallas_call(
        _mla_kv_kernel,
        grid=(n // tm,),
        in_specs=[pl.BlockSpec((tm, KV_LORA), lambda i: (i, 0)),
                  pl.BlockSpec((tm, HEAD_SLAB), lambda i: (i, 0)),
                  pl.BlockSpec(wk_bf16.shape, lambda i: (0, 0)),
                  pl.BlockSpec(wv_bf16.shape, lambda i: (0, 0))],
        out_specs=[pl.BlockSpec((tm, MLA_HEADS * HEAD_SLAB), lambda i: (i, 0)),
                   pl.BlockSpec((tm, MLA_HEADS * V_HEAD), lambda i: (i, 0))],
        out_shape=[jax.ShapeDtypeStruct((n, MLA_HEADS * HEAD_SLAB), BF16),
                   jax.ShapeDtypeStruct((n, MLA_HEADS * V_HEAD), BF16)],
        compiler_params=_cparams(("parallel",)),
        name="mla_kv",
    )(ckv, slab, wk_bf16, wv_bf16)


def _mla_attn_kernel(q_ref, *rest, n_sources):
    k_refs = rest[:n_sources]
    v_refs = rest[n_sources:2 * n_sources]
    o_ref = rest[2 * n_sources]
    tq = q_ref.shape[0]
    lane = lax.broadcasted_iota(jnp.int32, (tq, LANES), 1)
    outs = []
    for j in range(2):
        sl = slice(j * HEAD_SLAB, (j + 1) * HEAD_SLAB)
        q = q_ref[:, sl]
        scores = [_nt_dot(q, k[:, sl]) for k in k_refs]
        mx = functools.reduce(jnp.maximum, [jnp.max(s, axis=-1, keepdims=True) for s in scores])
        es = [jnp.exp(s - mx) for s in scores]
        den = functools.reduce(jnp.add, [jnp.sum(e, axis=-1, keepdims=True) for e in es])
        o = functools.reduce(jnp.add, [_dot(e.astype(BF16), v[...]) for e, v in zip(es, v_refs)])
        outs.append(o / den)
    o_ref[...] = jnp.where(lane < V_HEAD, outs[0], outs[1]).astype(BF16)


def mla_attention(q, k_new, v_new, k_ctx, v_ctx, n_batch, tq=256):
    n = q.shape[0]
    seq = n // n_batch
    tq = min(tq, seq)
    qt = seq // tq
    pairs = MLA_HEADS // 2
    srcs_k, srcs_v = [k_new], [v_new]
    lens = [seq]
    if k_ctx is not None:
        srcs_k.append(k_ctx)
        srcs_v.append(v_ctx)
        lens.append(k_ctx.shape[0] // n_batch)
    in_specs = [pl.BlockSpec((tq, 2 * HEAD_SLAB), lambda b, p, t: (b * qt + t, p))]
    in_specs += [pl.BlockSpec((ln, 2 * HEAD_SLAB), lambda b, p, t: (b, p)) for ln in lens]
    in_specs += [pl.BlockSpec((ln, 2 * V_HEAD), lambda b, p, t: (b, p)) for ln in lens]
    return pl.pallas_call(
        functools.partial(_mla_attn_kernel, n_sources=len(lens)),
        grid=(n_batch, pairs, qt),
        in_specs=in_specs,
        out_specs=pl.BlockSpec((tq, 2 * V_HEAD), lambda b, p, t: (b * qt + t, p)),
        out_shape=jax.ShapeDtypeStruct((n, MLA_HEADS * V_HEAD), BF16),
        compiler_params=_cparams(("parallel", "parallel", "arbitrary")),
        name="mla_attention",
    )(q, *srcs_k, *srcs_v)


def rope_lane_tables(n):
    t = jnp.arange(n)
    row = (t // GRID_W).astype(F32)
    col = (t % GRID_W).astype(F32)
    inv = ROPE_THETA ** (-jnp.arange(ROPE_PAIRS_AXIS, dtype=F32) / ROPE_PAIRS_AXIS)
    ang = jnp.concatenate([row[:, None] * inv, col[:, None] * inv], axis=-1)
    cos, sin = jnp.cos(ang), jnp.sin(ang)
    ones = jnp.ones((n, QK_NOPE), F32)
    zeros = jnp.zeros((n, QK_NOPE), F32)
    tail1 = jnp.ones((n, HEAD_SLAB - QK_NOPE - QK_ROPE), F32)
    tail0 = jnp.zeros((n, HEAD_SLAB - QK_NOPE - QK_ROPE), F32)
    z16 = jnp.zeros_like(sin)
    cos_t = jnp.concatenate([ones, cos, cos, tail1], axis=-1)
    sin_fwd = jnp.concatenate([zeros, z16, sin, tail0], axis=-1)
    sin_bwd = jnp.concatenate([zeros, -sin, z16, tail0], axis=-1)
    return cos_t, sin_fwd, sin_bwd


def _pad_mod(m, groups):
    m = m.reshape(groups, N_MOD, D_MODEL)
    return jnp.pad(m, ((0, 0), (0, MOD_ROWS - N_MOD), (0, 0)))


def _mla_weights(w_in, w_uq, w_ukv):
    pad_tail = HEAD_SLAB - QK_NOPE - QK_ROPE
    slab_cols = jnp.pad(w_in[:, Q_LORA + KV_LORA:], ((0, 0), (QK_NOPE, pad_tail)))
    win = jnp.concatenate([w_in[:, :Q_LORA + KV_LORA], slab_cols], axis=1).astype(BF16)
    wuq = jnp.pad(w_uq.reshape(Q_LORA, MLA_HEADS, QK_NOPE + QK_ROPE), ((0, 0), (0, 0), (0, pad_tail)))
    wuq = wuq.reshape(Q_LORA, MLA_HEADS * HEAD_SLAB).astype(BF16)
    wkv = w_ukv.reshape(KV_LORA, MLA_HEADS, QK_NOPE + V_HEAD)
    wk = jnp.pad(wkv[:, :, :QK_NOPE], ((0, 0), (0, 0), (0, HEAD_SLAB - QK_NOPE)))
    wk = wk.reshape(KV_LORA, MLA_HEADS * HEAD_SLAB).astype(BF16)
    wv = wkv[:, :, QK_NOPE:].reshape(KV_LORA, MLA_HEADS * V_HEAD).astype(BF16)
    return win, wuq, wk, wv


def kernel(x_prompt, x_sample, cache_na_k, cache_na_v, cache_mla_ckv, cache_mla_krope, c, c_ctx, w_ada, b_ada, ln_mix_g, ln_mix_b, ln_ffn_g, ln_ffn_b, w_in_ab, conv_w, na_rpb, w_out_ab, w_in_mla, q_norm_g, w_uq, kv_norm_g, w_ukv, w_out_mla, peer_w_q, peer_subkeys, peer_u, peer_v):
    nb_p, seq_p, d = x_prompt.shape
    nb_s, seq_s, _ = x_sample.shape
    past = cache_na_k.shape[2]
    rows = seq_s // GRID_W
    n_p = nb_p * seq_p
    xp = x_prompt.reshape(n_p, d)
    xs = x_sample.reshape(nb_s * seq_s, d)

    n_cond = 1 + nb_s
    cond_rows = -(-n_cond // 8) * 8
    cvec = jnp.concatenate([c_ctx[None], c, jnp.zeros((cond_rows - n_cond, d), F32)], axis=0)

    outs = {}
    for i in range(DEPTH):
        j = i // 2
        mods = ada_modulation(cvec, w_ada[i], b_ada[i])
        mod_p = _pad_mod(mods[0:1], 1)
        mod_s = _pad_mod(mods[1:n_cond], nb_s)
        lg = ln_mix_g[i].reshape(1, d)
        lb = ln_mix_b[i].reshape(1, d)
        if i % 2 == 0:
            w_in = w_in_ab[j].astype(BF16)
            w_out = w_out_ab[j].astype(BF16)
            cw = jnp.pad(conv_w[j], ((0, 8 - CONV_TAPS), (0, 0)))
            qkv_p, gu_p, kv_p = inproj_ab(xp, mod_p, w_in, n_p, True)
            qkv_s, gu_s = inproj_ab(xs, mod_s, w_in, seq_s, False)
            outs["na_k"] = kv_p[:, :NA_WIDTH].reshape(nb_p, 1, seq_p, NA_HEADS, NA_HEAD_DIM)
            outs["na_v"] = kv_p[:, NA_WIDTH:].reshape(nb_p, 1, seq_p, NA_HEADS, NA_HEAD_DIM)
            att_p = ctx_attention(qkv_p, seq_p)
            kc = cache_na_k[:, j].reshape(nb_s * past, NA_WIDTH).astype(BF16)
            vc = cache_na_v[:, j].reshape(nb_s * past, NA_WIDTH).astype(BF16)
            att_s = na_attention(qkv_s, kc, vc, na_bias_table(na_rpb[j]), nb_s, rows)
            xp = outproj_ab(att_p, gu_p, xp, mod_p, w_out, cw, lg, lb, n_p, seq_p)
            xs = outproj_ab(att_s, gu_s, xs, mod_s, w_out, cw, lg, lb, seq_s, seq_s)
        else:
            win, wuq, wk, wv = _mla_weights(w_in_mla[j], w_uq[j], w_ukv[j])
            w_out = w_out_mla[j].astype(BF16)
            qg = q_norm_g[j].reshape(1, Q_LORA)
            kvg = kv_norm_g[j].reshape(1, KV_LORA)
            q_p, ckv_p, slab_p = mla_inproj(xp, mod_p, win, qg, kvg, wuq, None, n_p, seq_p)
            q_s, ckv_s, slab_s = mla_inproj(xs, mod_s, win, qg, kvg, wuq, rope_lane_tables(seq_s), seq_s, seq_s)
            outs["ckv"] = ckv_p.reshape(nb_p, 1, seq_p, KV_LORA)
            outs["kr"] = slab_p[:, QK_NOPE:QK_NOPE + QK_ROPE].reshape(nb_p, 1, seq_p, QK_ROPE)
            k_p, v_p = mla_kv(ckv_p, slab_p, wk, wv)
            k_s, v_s = mla_kv(ckv_s, slab_s, wk, wv)
            ckv_c = cache_mla_ckv[:, j].reshape(nb_s * past, KV_LORA)
            slab_c = jnp.pad(cache_mla_krope[:, j].reshape(nb_s * past, QK_ROPE),
                             ((0, 0), (QK_NOPE, HEAD_SLAB - QK_NOPE - QK_ROPE)))
            k_c, v_c = mla_kv(ckv_c, slab_c, wk, wv)
            att_p = mla_attention(q_p, k_p, v_p, None, None, nb_p)
            att_s = mla_attention(q_s, k_s, v_s, k_c, v_c, nb_s)
            xp = outproj(att_p, xp, mod_p, w_out, lg, lb, n_p)
            xs = outproj(att_s, xs, mod_s, w_out, lg, lb, seq_s)
        wq = peer_w_q[i].astype(BF16)
        sk = peer_subkeys[i].reshape(2 * PEER_HEADS, N_KEYS, PEER_HALF).astype(BF16)
        u = peer_u[i].astype(BF16)
        vt = peer_v[i].T.astype(BF16)
        fg = ln_ffn_g[i].reshape(1, d)
        fb = ln_ffn_b[i].reshape(1, d)
        xp = peer_block(xp, mod_p, wq, sk, u, vt, fg, fb, n_p)
        xs = peer_block(xs, mod_s, wq, sk, u, vt, fg, fb, seq_s)
    return (xp.reshape(nb_p, seq_p, d), xs.reshape(nb_s, seq_s, d),
            outs["na_k"], outs["na_v"], outs["ckv"], outs["kr"])
```

```python
import functools

import jax
import jax.numpy as jnp
import numpy as np
from jax import lax
from jax.experimental import pallas as pl
from jax.experimental.pallas import tpu as pltpu

F32 = jnp.float32
BF16 = jnp.bfloat16

D_MODEL = 1024
DEPTH = 2
GRID_W = 64
N_MOD = 6
ALPHA = (2.0 * DEPTH) ** 0.25
LN_EPS = 1e-5
RMS_EPS = 1e-6
NA_HEADS = 8
NA_HEAD_DIM = 64
NA_WIDTH = NA_HEADS * NA_HEAD_DIM
NA_MAX_ROWS = 8
NA_WIN_COLS = 16
NA_SCALE = NA_HEAD_DIM ** -0.5
CONV_WIDTH = 512
CONV_TAPS = 3
MLA_HEADS = 16
Q_LORA = 256
KV_LORA = 256
QK_NOPE = 64
QK_ROPE = 32
V_HEAD = 64
MLA_SCALE = (QK_NOPE + QK_ROPE) ** -0.5
ROPE_THETA = 10000.0
ROPE_PAIRS_AXIS = QK_ROPE // 4
PEER_HEADS = 8
PEER_HALF = 128
N_KEYS = 128
N_EXPERTS = N_KEYS * N_KEYS
PEER_TOPK = 16

LANES = 128
MOD_ROWS = 8
HEAD_SLAB = 128
MASK_BIAS = -1e30
VMEM_LIMIT = 52 * 1024 * 1024


def _cparams(sem, flags=None):
    return pltpu.CompilerParams(dimension_semantics=sem, vmem_limit_bytes=VMEM_LIMIT, flags=flags)


def _nt_dot(a, b):
    return lax.dot_general(a, b, (((1,), (1,)), ((), ())), preferred_element_type=F32)


def _dot(a, b):
    return jnp.dot(a, b, preferred_element_type=F32)


def _split_bf16(a):
    hi = a.astype(BF16)
    lo = (a - hi.astype(F32)).astype(BF16)
    return hi, lo


def _layer_norm(z, g, b):
    mu = jnp.mean(z, axis=-1, keepdims=True)
    zc = z - mu
    var = jnp.mean(zc * zc, axis=-1, keepdims=True)
    return zc * lax.rsqrt(var + LN_EPS) * g + b


LOG2E = 1.4426950408889634
LN2 = 0.6931471805599453
_GELU_A = -2.0 * LOG2E * 0.7978845608028654
_GELU_B = _GELU_A * 0.044715


def _gelu_tanh(x):
    return x / (1.0 + jnp.exp2(x * (_GELU_A + _GELU_B * (x * x))))


def _ada_kernel(c_ref, w_ref, b_ref, o_ref):
    c = c_ref[...]
    a = c * (1.0 / (1.0 + jnp.exp(-c)))
    a_hi, a_lo = _split_bf16(a)
    w_hi, w_lo = _split_bf16(w_ref[...])
    o_ref[...] = _dot(a_hi, w_hi) + _dot(a_hi, w_lo) + _dot(a_lo, w_hi) + b_ref[...]


def ada_modulation(cvec, w_ada, b_ada, tn=512):
    r, d = cvec.shape
    n = w_ada.shape[1]
    return pl.pallas_call(
        _ada_kernel,
        grid=(n // tn,),
        in_specs=[pl.BlockSpec((r, d), lambda j: (0, 0)),
                  pl.BlockSpec((d, tn), lambda j: (0, j)),
                  pl.BlockSpec((1, tn), lambda j: (0, j))],
        out_specs=pl.BlockSpec((r, tn), lambda j: (0, j)),
        out_shape=jax.ShapeDtypeStruct((r, n), F32),
        compiler_params=_cparams(("arbitrary",)),
        name="ada_modulation",
    )(cvec, w_ada, b_ada.reshape(1, n))


def _inproj_ab_kernel(x_ref, mod_ref, w_ref, qkv_ref, gu_ref, *kv_ref):
    m = mod_ref[0]
    h = x_ref[...] * (1.0 + m[1:2]) + m[0:1]
    y = _dot(h.astype(BF16), w_ref[...])
    w3 = 3 * NA_WIDTH
    qkv_ref[:, :NA_WIDTH] = (y[:, :NA_WIDTH] * (NA_SCALE * LOG2E)).astype(BF16)
    qkv_ref[:, NA_WIDTH:] = y[:, NA_WIDTH:w3].astype(BF16)
    if kv_ref:
        kv_ref[0][...] = y[:, NA_WIDTH:w3]
    gu_ref[:, :CONV_WIDTH] = y[:, w3:w3 + CONV_WIDTH].astype(BF16)
    gu_ref[:, CONV_WIDTH:] = (y[:, w3 + CONV_WIDTH:w3 + 2 * CONV_WIDTH]
                              * y[:, w3 + 2 * CONV_WIDTH:]).astype(BF16)


def inproj_ab(x, mod, w_bf16, tokens_per_group, emit_kv, tm=512):
    n, d = x.shape
    tm = min(tm, n)
    out_shape = [jax.ShapeDtypeStruct((n, 3 * NA_WIDTH), BF16),
                 jax.ShapeDtypeStruct((n, 2 * CONV_WIDTH), BF16)]
    out_specs = [pl.BlockSpec((tm, 3 * NA_WIDTH), lambda i: (i, 0)),
                 pl.BlockSpec((tm, 2 * CONV_WIDTH), lambda i: (i, 0))]
    if emit_kv:
        out_shape.append(jax.ShapeDtypeStruct((n, 2 * NA_WIDTH), F32))
        out_specs.append(pl.BlockSpec((tm, 2 * NA_WIDTH), lambda i: (i, 0)))
    return pl.pallas_call(
        _inproj_ab_kernel,
        grid=(n // tm,),
        in_specs=[pl.BlockSpec((tm, d), lambda i: (i, 0)),
                  pl.BlockSpec((1, MOD_ROWS, d), lambda i: ((i * tm) // tokens_per_group, 0, 0)),
                  pl.BlockSpec(w_bf16.shape, lambda i: (0, 0))],
        out_specs=out_specs,
        out_shape=out_shape,
        compiler_params=_cparams(("parallel",)),
        name="inproj_ab",
    )(x, mod, w_bf16)


def _pair_attention(instances):
    heads = []
    for n, (q_pair, _, _, _) in enumerate(instances):
        lane = lax.broadcasted_iota(jnp.int32, q_pair.shape, 1)
        qf = q_pair.astype(F32)
        for j in range(2):
            in_head = (lane >= j * NA_HEAD_DIM) & (lane < (j + 1) * NA_HEAD_DIM)
            heads.append((n, j, jnp.where(in_head, qf, 0.0).astype(BF16)))
    scores = []
    for n, j, qm in heads:
        _, keys, _, biases = instances[n]
        ss = [_nt_dot(qm, k) for k in keys]
        scores.append([s if b is None else s + b for s, b in zip(ss, biases[j])])
    mxs = [functools.reduce(jnp.maximum, [jnp.max(s, axis=-1, keepdims=True) for s in ss]) for ss in scores]
    ess = [[jnp.exp2(s - mx) for s in ss] for ss, mx in zip(scores, mxs)]
    dens = [functools.reduce(jnp.add, [jnp.sum(e, axis=-1, keepdims=True) for e in es]) for es in ess]
    outs = []
    for (n, j, _), es in zip(heads, ess):
        values = instances[n][2]
        outs.append(functools.reduce(jnp.add, [_dot(e.astype(BF16), v) for e, v in zip(es, values)]))
    outs = [o / d for o, d in zip(outs, dens)]
    lane = lax.broadcasted_iota(jnp.int32, outs[0].shape, 1)
    return [jnp.where(lane < NA_HEAD_DIM, outs[2 * n], outs[2 * n + 1]) for n in range(len(instances))]


NA_ROWS_PER_STEP = 2


def _na_attn_kernel(q_ref, k_ref, v_ref, kc_ref, vc_ref, *rest, rows):
    bias_refs = rest[:NA_ROWS_PER_STEP]
    o_ref = rest[NA_ROWS_PER_STEP]
    band = NA_MAX_ROWS * GRID_W
    instances, places = [], []
    for rr in range(NA_ROWS_PER_STEP):
        r = pl.program_id(1) * NA_ROWS_PER_STEP + rr
        rs = jnp.clip(r - NA_MAX_ROWS // 2, 0, rows - NA_MAX_ROWS)
        start = pl.multiple_of(rs * GRID_W, GRID_W)
        qrows = slice(rr * GRID_W, (rr + 1) * GRID_W)
        for p in range(NA_HEADS // 2):
            sl = slice(p * LANES, (p + 1) * LANES)
            keys = [k_ref[pl.ds(start, band), sl], kc_ref[:, sl]]
            values = [v_ref[pl.ds(start, band), sl], vc_ref[:, sl]]
            biases = [[bias_refs[rr][0, 2 * p + j], None] for j in range(2)]
            instances.append((q_ref[qrows, sl], keys, values, biases))
            places.append((qrows, sl))
    for (qrows, sl), o in zip(places, _pair_attention(instances)):
        o_ref[qrows, sl] = o.astype(BF16)


def na_attention(qkv, kc, vc, bias, n_batch, rows):
    n = qkv.shape[0]
    seq = rows * GRID_W
    lc = kc.shape[0] // n_batch
    half = NA_MAX_ROWS // 2

    steps = rows // NA_ROWS_PER_STEP
    tq = NA_ROWS_PER_STEP * GRID_W

    def bias_spec(rr):
        def idx(b, g):
            r = g * NA_ROWS_PER_STEP + rr
            return (r - jnp.clip(r - half, 0, rows - NA_MAX_ROWS), 0, 0, 0)
        return pl.BlockSpec((1, NA_HEADS, GRID_W, NA_MAX_ROWS * GRID_W), idx)

    return pl.pallas_call(
        functools.partial(_na_attn_kernel, rows=rows),
        grid=(n_batch, steps),
        in_specs=[pl.BlockSpec((tq, NA_WIDTH), lambda b, g: (b * steps + g, 0)),
                  pl.BlockSpec((seq, NA_WIDTH), lambda b, g: (b, 1)),
                  pl.BlockSpec((seq, NA_WIDTH), lambda b, g: (b, 2)),
                  pl.BlockSpec((lc, NA_WIDTH), lambda b, g: (b, 0)),
                  pl.BlockSpec((lc, NA_WIDTH), lambda b, g: (b, 0))]
        + [bias_spec(rr) for rr in range(NA_ROWS_PER_STEP)],
        out_specs=pl.BlockSpec((tq, NA_WIDTH), lambda b, g: (b * steps + g, 0)),
        out_shape=jax.ShapeDtypeStruct((n, NA_WIDTH), BF16),
        compiler_params=_cparams(("parallel", "arbitrary")),
        name="na_attention",
    )(qkv, qkv, qkv, kc, vc, *([bias] * NA_ROWS_PER_STEP))


def na_bias_table(rpb):
    c = np.arange(GRID_W)[:, None]
    kc = np.arange(GRID_W)[None, :]
    cs = np.clip(c - NA_WIN_COLS // 2, 0, GRID_W - NA_WIN_COLS)
    valid = (kc >= cs) & (kc < cs + NA_WIN_COLS)
    dc = kc - c + (NA_WIN_COLS - 1)
    cc, kk = np.nonzero(valid)
    place = np.zeros((2 * NA_WIN_COLS - 1, GRID_W, GRID_W), np.float32)
    place[dc[cc, kk], cc, kk] = 1.0
    t = jnp.einsum("hrd,dck->hrck", rpb, place, precision=lax.Precision.HIGHEST)
    t = jnp.where(valid[None, None], t * LOG2E, MASK_BIAS)
    bands = [t[:, NA_MAX_ROWS - 1 - off:2 * NA_MAX_ROWS - 1 - off] for off in range(NA_MAX_ROWS)]
    b = jnp.stack(bands, axis=0)
    b = jnp.transpose(b, (0, 1, 3, 2, 4))
    return b.reshape(NA_MAX_ROWS, NA_HEADS, GRID_W, NA_MAX_ROWS * GRID_W).astype(F32)


def _ctx_attn_kernel(q_ref, k_ref, v_ref, o_ref):
    slabs = [slice(p * LANES, (p + 1) * LANES) for p in range(NA_HEADS // 2)]
    instances = [(q_ref[:, sl], [k_ref[:, sl]], [v_ref[:, sl]], [[None], [None]]) for sl in slabs]
    for sl, o in zip(slabs, _pair_attention(instances)):
        o_ref[:, sl] = o.astype(BF16)


def ctx_attention(qkv, seq):
    n = qkv.shape[0]
    return pl.pallas_call(
        _ctx_attn_kernel,
        grid=(n // seq,),
        in_specs=[pl.BlockSpec((seq, NA_WIDTH), lambda b: (b, 0)),
                  pl.BlockSpec((seq, NA_WIDTH), lambda b: (b, 1)),
                  pl.BlockSpec((seq, NA_WIDTH), lambda b: (b, 2))],
        out_specs=pl.BlockSpec((seq, NA_WIDTH), lambda b: (b, 0)),
        out_shape=jax.ShapeDtypeStruct((n, NA_WIDTH), BF16),
        compiler_params=_cparams(("parallel",)),
        name="ctx_attention",
    )(qkv, qkv, qkv)


HALO = 16


def _outproj_ab_kernel(att_ref, gu_ref, prev_ref, next_ref, x_ref, mod_ref, w_ref, cw_ref, g_ref, b_ref,
                       o_ref, *, tiles_per_seq):
    i = pl.program_id(0)
    tm = x_ref.shape[0]
    m = mod_ref[0]
    gu = gu_ref[...].astype(F32)
    bg = gu[:, :CONV_WIDTH]
    u = gu[:, CONV_WIDTH:]
    has_prev = (i % tiles_per_seq != 0).astype(F32)
    has_next = (i % tiles_per_seq != tiles_per_seq - 1).astype(F32)
    prev_row = prev_ref[HALO - 1:HALO, CONV_WIDTH:].astype(F32) * has_prev
    next_row = next_ref[0:1, CONV_WIDTH:].astype(F32) * has_next
    row = lax.broadcasted_iota(jnp.int32, u.shape, 0)
    u_m1 = jnp.where(row == 0, prev_row, pltpu.roll(u, 1, axis=0))
    u_p1 = jnp.where(row == tm - 1, next_row, pltpu.roll(u, tm - 1, axis=0))
    cw = cw_ref[...]
    conv = bg * (u_m1 * cw[0:1] + u * cw[1:2] + u_p1 * cw[2:3])
    y = _dot(att_ref[...], w_ref[:NA_WIDTH, :]) + _dot(conv.astype(BF16), w_ref[NA_WIDTH:, :])
    z = ALPHA * x_ref[...] + m[2:3] * y
    o_ref[...] = _layer_norm(z, g_ref[...], b_ref[...])


def outproj_ab(att, gu, x, mod, w_bf16, conv_w8, ln_g, ln_b, tokens_per_group, seq, tm=256):
    n, d = x.shape
    tm = min(tm, seq)
    hb = tm // HALO
    last_halo = n // HALO - 1
    return pl.pallas_call(
        functools.partial(_outproj_ab_kernel, tiles_per_seq=seq // tm),
        grid=(n // tm,),
        in_specs=[pl.BlockSpec((tm, NA_WIDTH), lambda i: (i, 0)),
                  pl.BlockSpec((tm, 2 * CONV_WIDTH), lambda i: (i, 0)),
                  pl.BlockSpec((HALO, 2 * CONV_WIDTH), lambda i: (jnp.maximum(i * hb - 1, 0), 0)),
                  pl.BlockSpec((HALO, 2 * CONV_WIDTH), lambda i: (jnp.minimum((i + 1) * hb, last_halo), 0)),
                  pl.BlockSpec((tm, d), lambda i: (i, 0)),
                  pl.BlockSpec((1, MOD_ROWS, d), lambda i: ((i * tm) // tokens_per_group, 0, 0)),
                  pl.BlockSpec(w_bf16.shape, lambda i: (0, 0)),
                  pl.BlockSpec(conv_w8.shape, lambda i: (0, 0)),
                  pl.BlockSpec((1, d), lambda i: (0, 0)),
                  pl.BlockSpec((1, d), lambda i: (0, 0))],
        out_specs=pl.BlockSpec((tm, d), lambda i: (i, 0)),
        out_shape=jax.ShapeDtypeStruct((n, d), F32),
        compiler_params=_cparams(("parallel",)),
        name="outproj_ab",
    )(att, gu, gu, gu, x, mod, w_bf16, conv_w8, ln_g, ln_b)


def _outproj_kernel(att_ref, x_ref, mod_ref, w_ref, g_ref, b_ref, o_ref):
    m = mod_ref[0]
    y = _dot(att_ref[...], w_ref[...])
    z = ALPHA * x_ref[...] + m[2:3] * y
    o_ref[...] = _layer_norm(z, g_ref[...], b_ref[...])


def outproj(att, x, mod, w_bf16, ln_g, ln_b, tokens_per_group, tm=512):
    n, d = x.shape
    k = att.shape[1]
    return pl.pallas_call(
        _outproj_kernel,
        grid=(n // tm,),
        in_specs=[pl.BlockSpec((tm, k), lambda i: (i, 0)),
                  pl.BlockSpec((tm, d), lambda i: (i, 0)),
                  pl.BlockSpec((1, MOD_ROWS, d), lambda i: ((i * tm) // tokens_per_group, 0, 0)),
                  pl.BlockSpec(w_bf16.shape, lambda i: (0, 0)),
                  pl.BlockSpec((1, d), lambda i: (0, 0)),
                  pl.BlockSpec((1, d), lambda i: (0, 0))],
        out_specs=pl.BlockSpec((tm, d), lambda i: (i, 0)),
        out_shape=jax.ShapeDtypeStruct((n, d), F32),
        compiler_params=_cparams(("parallel",)),
        name="outproj",
    )(att, x, mod, w_bf16, ln_g, ln_b)


def _peer_scores_kernel(x_ref, mod_ref, wq_ref, sk_ref, hm_ref, st_ref):
    m = mod_ref[0]
    hb = (x_ref[...] * (1.0 + m[4:5]) + m[3:4]).astype(BF16)
    hm_ref[...] = hb
    q = _dot(hb, wq_ref[...])
    for c in range(2 * PEER_HEADS):
        qc = q[:, c * PEER_HALF:(c + 1) * PEER_HALF].astype(BF16)
        st_ref[c] = _nt_dot(sk_ref[c], qc) * LOG2E


def peer_scores(x, mod, wq_bf16, subkeys_bf16, tokens_per_group, tm=512):
    n, d = x.shape
    nc = 2 * PEER_HEADS
    return pl.pallas_call(
        _peer_scores_kernel,
        grid=(n // tm,),
        in_specs=[pl.BlockSpec((tm, d), lambda i: (i, 0)),
                  pl.BlockSpec((1, MOD_ROWS, d), lambda i: ((i * tm) // tokens_per_group, 0, 0)),
                  pl.BlockSpec(wq_bf16.shape, lambda i: (0, 0)),
                  pl.BlockSpec(subkeys_bf16.shape, lambda i: (0, 0, 0))],
        out_specs=[pl.BlockSpec((tm, d), lambda i: (i, 0)),
                   pl.BlockSpec((nc, N_KEYS, tm), lambda i: (0, 0, i))],
        out_shape=[jax.ShapeDtypeStruct((n, d), BF16),
                   jax.ShapeDtypeStruct((nc, N_KEYS, n), F32)],
        compiler_params=_cparams(("parallel",)),
        name="peer_scores",
    )(x, mod, wq_bf16, subkeys_bf16)


SUBLANES = 8
N_SLABS = N_KEYS // SUBLANES


def _sort_network(n):
    pairs = []
    p = 1
    while p < n:
        k = p
        while k >= 1:
            for j in range(k % p, n - k, 2 * k):
                for i in range(min(k, n - j - k)):
                    if (i + j) // (2 * p) == (i + j + k) // (2 * p):
                        pairs.append((i + j, i + j + k))
            k //= 2
        p *= 2
    return pairs


_SLAB_SORT = _sort_network(N_SLABS)


def _top_values(s_ref, c, n_top):
    x = [s_ref[c, r] for r in range(N_SLABS)]
    for a, b in _SLAB_SORT:
        x[a], x[b] = jnp.maximum(x[a], x[b]), jnp.minimum(x[a], x[b])
    sub = lax.broadcasted_iota(jnp.int32, x[0].shape, 0).astype(F32)
    neg = jnp.full(x[0].shape, -jnp.inf, F32)
    x.append(neg)
    vals = []
    for k in range(n_top):
        mx = jnp.max(x[0], axis=0, keepdims=True)
        vals.append(mx)
        if k == n_top - 1:
            break
        first = jnp.min(jnp.where(x[0] == mx, sub, float(SUBLANES)), axis=0, keepdims=True)
        pop = sub == first
        for r in range(n_top - 1 - k):
            x[r] = jnp.where(pop, x[r + 1], x[r])
    return vals


def _rows_to_slab(rows_list):
    t = rows_list[0].shape[1]
    ridx = lax.broadcasted_iota(jnp.int32, (SUBLANES, t), 0)
    out = jnp.full((SUBLANES, t), -jnp.inf, F32)
    for r, row in enumerate(rows_list):
        out = jnp.where(ridx == r, row, out)
    return out


def _peer_select_kernel(st_ref, thr_ref, lse_ref, m2_ref):
    k = PEER_TOPK
    for h in range(PEER_HEADS):
        v1 = _top_values(st_ref, 2 * h, k + 1)
        v2 = _top_values(st_ref, 2 * h + 1, k + 1)
        v1_lo, v1_hi = _rows_to_slab(v1[0:8]), _rows_to_slab(v1[8:16])
        v2_lo, v2_hi = _rows_to_slab(v2[0:8]), _rows_to_slab(v2[8:16])
        ridx = lax.broadcasted_iota(jnp.int32, v1_lo.shape, 0)
        v1_mid = jnp.where(ridx >= 4, v1_lo, -jnp.inf)
        cands = [v1[0] + v2_lo, v1[0] + v2_hi, v1[1] + v2_lo, v1[2] + v2_lo, v1[3] + v2_lo,
                 v1_hi + v2[0], v1_mid + v2[0], v1_mid + v2[1], v1_mid + v2[2],
                 _rows_to_slab([v1[k] + v2[0], v1[0] + v2[k]])]
        top = v1[0] + v2[0]
        work = list(cands)
        thr = jnp.full_like(top, -jnp.inf)
        nxt = jnp.full_like(top, -jnp.inf)
        seen = jnp.zeros_like(top)
        for _ in range(k + 1):
            mx = jnp.max(functools.reduce(jnp.maximum, work), axis=0, keepdims=True)
            eqs = [w == mx for w in work]
            cnt = functools.reduce(jnp.add, [jnp.where(e, 1.0, 0.0) for e in eqs])
            thr = jnp.where(seen < k, mx, thr)
            nxt = jnp.where(seen < k + 1, mx, nxt)
            seen = seen + jnp.sum(cnt, axis=0, keepdims=True)
            work = [jnp.where(e, -jnp.inf, w) for e, w in zip(eqs, work)]
        z = functools.reduce(jnp.add, [jnp.where(cd >= thr, jnp.exp2(cd - top), 0.0) for cd in cands])
        z = jnp.sum(z, axis=0, keepdims=True)
        thr_ref[h:h + 1, :] = 0.5 * (thr + nxt)
        lse_ref[h:h + 1, :] = v1[0] + jnp.log(z) * LOG2E
        m2_ref[h:h + 1, :] = v2[0]


def peer_select(st, tl=256):
    nc, _, _, n = st.shape
    stat = pl.BlockSpec((PEER_HEADS, tl), lambda i: (0, i))
    return pl.pallas_call(
        _peer_select_kernel,
        grid=(n // tl,),
        in_specs=[pl.BlockSpec((nc, N_SLABS, SUBLANES, tl), lambda i: (0, 0, 0, i))],
        out_specs=[stat, stat, stat],
        out_shape=[jax.ShapeDtypeStruct((PEER_HEADS, n), F32)] * 3,
        compiler_params=_cparams(("parallel",)),
        name="peer_select",
    )(st)


PEER_LANE_CHUNK = 128
PEER_MXU_CHUNK = 256


def _peer_dense_kernel(hm_ref, u_ref, vt_ref, st_ref, thr_ref, lse_ref, m2_ref, x_ref, mod_ref, g_ref, b_ref,
                       o_ref, acc_ref, ht0_ref, ht1_ref, at0_ref, at1_ref, e2_ref, cut_rep, f1_rep,
                       *, gate_row, te):
    g = pl.program_id(1)
    tm = hm_ref.shape[0]
    n_tiles = 2 * (pl.num_programs(1) - 1)
    hts = (ht0_ref, ht1_ref)
    ats = (at0_ref, at1_ref)

    @pl.when(g == 0)
    def _():
        acc_ref[...] = jnp.zeros_like(acc_ref)
        ht1_ref[...] = jnp.zeros_like(ht1_ref)
        at0_ref[...] = jnp.zeros_like(at0_ref)
        for h in range(PEER_HEADS):
            e2_ref[h] = jnp.exp2(st_ref[2 * h + 1] - m2_ref[h:h + 1, :])

    for sub in range(2):
        cur, prev = sub, 1 - sub
        half = slice(sub * te, (sub + 1) * te)
        n_chunks = tm // PEER_LANE_CHUNK

        def stage_a(c):
            ls = slice(c * PEER_MXU_CHUNK, (c + 1) * PEER_MXU_CHUNK)
            hts[cur][:, ls] = _nt_dot(u_ref[half, :], hm_ref[ls, :])

        def stage_c(c):
            ls = slice(c * PEER_MXU_CHUNK, (c + 1) * PEER_MXU_CHUNK)
            acc_ref[:, ls] += _dot(vt_ref[:, half], ats[cur][:, ls])

        mxu_pieces = [functools.partial(f, c) for c in range(tm // PEER_MXU_CHUNK) for f in (stage_a, stage_c)]
        units_per_piece = SUBLANES * n_chunks // len(mxu_pieces)

        tile = jnp.clip(2 * g + sub - 1, 0, n_tiles - 1)
        for h in range(PEER_HEADS):
            s1 = st_ref[2 * h, tile]
            cut_rows = thr_ref[h:h + 1, :] - s1
            f1_rows = jnp.exp2(s1 - lse_ref[h:h + 1, :])
            for ii in range(SUBLANES):
                cut_rep[h * SUBLANES + ii] = jnp.broadcast_to(cut_rows[ii:ii + 1, :], (SUBLANES, tm))
                f1_rep[h * SUBLANES + ii] = jnp.broadcast_to(f1_rows[ii:ii + 1, :], (SUBLANES, tm))
        for ii in range(SUBLANES):
            rows = slice(ii * N_KEYS, (ii + 1) * N_KEYS)
            for c in range(n_chunks):
                unit = ii * n_chunks + c
                if unit % units_per_piece == 0:
                    mxu_pieces[unit // units_per_piece]()
                ls = slice(c * PEER_LANE_CHUNK, (c + 1) * PEER_LANE_CHUNK)
                w = jnp.zeros((N_SLABS, SUBLANES, PEER_LANE_CHUNK), F32)
                for h in range(PEER_HEADS):
                    sel = st_ref[2 * h + 1, :, :, ls] >= cut_rep[h * SUBLANES + ii, :, ls][None]
                    w = w + jnp.where(sel, e2_ref[h, :, :, ls] * f1_rep[h * SUBLANES + ii, :, ls][None], 0.0)
                act = _gelu_tanh(hts[prev][rows, ls])
                ats[prev][rows, ls] = (act * w.reshape(N_KEYS, PEER_LANE_CHUNK)).astype(BF16)

    @pl.when(g == pl.num_programs(1) - 1)
    def _():
        m = mod_ref[0]
        z = ALPHA * x_ref[...] + m[gate_row:gate_row + 1] * acc_ref[...].T
        o_ref[...] = _layer_norm(z, g_ref[...], b_ref[...])


def peer_dense(hm, u_bf16, vt_bf16, st, thr, lse, m2, x, mod, ln_g, ln_b, tokens_per_group, tm=512, te=1024):
    n, d = x.shape
    ne = u_bf16.shape[0]
    nc = st.shape[0]
    assert te == N_KEYS * SUBLANES
    n_pairs = ne // (2 * te)
    return pl.pallas_call(
        functools.partial(_peer_dense_kernel, gate_row=5, te=te),
        grid=(n // tm, n_pairs + 1),
        in_specs=[pl.BlockSpec((tm, d), lambda i, g: (i, 0)),
                  pl.BlockSpec((2 * te, d), lambda i, g: (jnp.minimum(g, n_pairs - 1), 0)),
                  pl.BlockSpec((d, 2 * te), lambda i, g: (0, jnp.maximum(g - 1, 0))),
                  pl.BlockSpec((nc, N_SLABS, SUBLANES, tm), lambda i, g: (0, 0, 0, i)),
                  pl.BlockSpec((PEER_HEADS, tm), lambda i, g: (0, i)),
                  pl.BlockSpec((PEER_HEADS, tm), lambda i, g: (0, i)),
                  pl.BlockSpec((PEER_HEADS, tm), lambda i, g: (0, i)),
                  pl.BlockSpec((tm, d), lambda i, g: (i, 0)),
                  pl.BlockSpec((1, MOD_ROWS, d), lambda i, g: ((i * tm) // tokens_per_group, 0, 0)),
                  pl.BlockSpec((1, d), lambda i, g: (0, 0)),
                  pl.BlockSpec((1, d), lambda i, g: (0, 0))],
        out_specs=pl.BlockSpec((tm, d), lambda i, g: (i, 0)),
        out_shape=jax.ShapeDtypeStruct((n, d), F32),
        scratch_shapes=[pltpu.VMEM((d, tm), F32),
                        pltpu.VMEM((te, tm), F32), pltpu.VMEM((te, tm), F32),
                        pltpu.VMEM((te, tm), BF16), pltpu.VMEM((te, tm), BF16),
                        pltpu.VMEM((PEER_HEADS, N_SLABS, SUBLANES, tm), F32),
                        pltpu.VMEM((PEER_HEADS * SUBLANES, SUBLANES, tm), F32),
                        pltpu.VMEM((PEER_HEADS * SUBLANES, SUBLANES, tm), F32)],
        compiler_params=_cparams(("parallel", "arbitrary")),
        name="peer_dense",
    )(hm, u_bf16, vt_bf16, st, thr, lse, m2, x, mod, ln_g, ln_b)


def peer_block(x, mod, wq, sk, u, vt, ln_g, ln_b, tokens_per_group):
    hm, st = peer_scores(x, mod, wq, sk, tokens_per_group)
    st = st.reshape(2 * PEER_HEADS, N_SLABS, SUBLANES, x.shape[0])
    cut, lse, m2 = peer_select(st)
    return peer_dense(hm, u, vt, st, cut, lse, m2, x, mod, ln_g, ln_b, tokens_per_group)


def _rope_slab(t, cos_t, sin_fwd, sin_bwd):
    return t * cos_t + pltpu.roll(t, QK_ROPE // 2, axis=1) * sin_fwd \
        + pltpu.roll(t, HEAD_SLAB - QK_ROPE // 2, axis=1) * sin_bwd


def _mla_inproj_kernel(x_ref, mod_ref, win_ref, qg_ref, kvg_ref, wuq_ref, *rest, use_rope):
    if use_rope:
        cos_ref, sf_ref, sb_ref, q_ref, ckv_ref, slab_ref = rest
    else:
        q_ref, ckv_ref, slab_ref = rest
    m = mod_ref[0]
    h = x_ref[...] * (1.0 + m[1:2]) + m[0:1]
    y = _dot(h.astype(BF16), win_ref[...])
    cq = y[:, :Q_LORA]
    ckv = y[:, Q_LORA:Q_LORA + KV_LORA]
    slab = y[:, Q_LORA + KV_LORA:]
    cqn = cq * lax.rsqrt(jnp.mean(cq * cq, axis=-1, keepdims=True) + RMS_EPS) * qg_ref[...]
    ckv_ref[...] = ckv * lax.rsqrt(jnp.mean(ckv * ckv, axis=-1, keepdims=True) + RMS_EPS) * kvg_ref[...]
    q = _dot(cqn.astype(BF16), wuq_ref[...]) * (MLA_SCALE * LOG2E)
    if use_rope:
        cos_t, sf, sb = cos_ref[...], sf_ref[...], sb_ref[...]
        slab = _rope_slab(slab, cos_t, sf, sb)
        for hh in range(MLA_HEADS):
            sl = slice(hh * HEAD_SLAB, (hh + 1) * HEAD_SLAB)
            q_ref[:, sl] = _rope_slab(q[:, sl], cos_t, sf, sb).astype(BF16)
    else:
        q_ref[...] = q.astype(BF16)
    slab_ref[...] = slab


def mla_inproj(x, mod, win_bf16, q_g, kv_g, wuq_bf16, rope, tokens_per_group, seq, tm=256):
    n, d = x.shape
    use_rope = rope is not None
    in_specs = [pl.BlockSpec((tm, d), lambda i: (i, 0)),
                pl.BlockSpec((1, MOD_ROWS, d), lambda i: ((i * tm) // tokens_per_group, 0, 0)),
                pl.BlockSpec(win_bf16.shape, lambda i: (0, 0)),
                pl.BlockSpec((1, Q_LORA), lambda i: (0, 0)),
                pl.BlockSpec((1, KV_LORA), lambda i: (0, 0)),
                pl.BlockSpec(wuq_bf16.shape, lambda i: (0, 0))]
    args = [x, mod, win_bf16, q_g, kv_g, wuq_bf16]
    if use_rope:
        tps = seq // tm
        in_specs += [pl.BlockSpec((tm, HEAD_SLAB), lambda i: (i % tps, 0))] * 3
        args += list(rope)
    return pl.pallas_call(
        functools.partial(_mla_inproj_kernel, use_rope=use_rope),
        grid=(n // tm,),
        in_specs=in_specs,
        out_specs=[pl.BlockSpec((tm, MLA_HEADS * HEAD_SLAB), lambda i: (i, 0)),
                   pl.BlockSpec((tm, KV_LORA), lambda i: (i, 0)),
                   pl.BlockSpec((tm, HEAD_SLAB), lambda i: (i, 0))],
        out_shape=[jax.ShapeDtypeStruct((n, MLA_HEADS * HEAD_SLAB), BF16),
                   jax.ShapeDtypeStruct((n, KV_LORA), F32),
                   jax.ShapeDtypeStruct((n, HEAD_SLAB), F32)],
        compiler_params=_cparams(("parallel",)),
        name="mla_inproj",
    )(*args)


def _mla_kv_kernel(ckv_ref, slab_ref, wk_ref, wv_ref, k_ref, v_ref):
    c = ckv_ref[...].astype(BF16)
    kn = _dot(c, wk_ref[...])
    slab = slab_ref[...]
    for hh in range(MLA_HEADS):
        sl = slice(hh * HEAD_SLAB, (hh + 1) * HEAD_SLAB)
        k_ref[:, sl] = (kn[:, sl] + slab).astype(BF16)
    v = _dot(c, wv_ref[...])
    lane = lax.broadcasted_iota(jnp.int32, (v.shape[0], LANES), 1)
    for p in range(MLA_HEADS // 2):
        pair = v[:, p * LANES:(p + 1) * LANES]
        v_ref[:, (2 * p) * LANES:(2 * p + 1) * LANES] = jnp.where(lane < V_HEAD, pair, 1.0).astype(BF16)
        v_ref[:, (2 * p + 1) * LANES:(2 * p + 2) * LANES] = jnp.where(lane < V_HEAD, 1.0, pair).astype(BF16)


def mla_kv(ckv, slab, wk_bf16, wv_bf16, tm=256):
    n = ckv.shape[0]
    tm = min(tm, n)
    return pl.pallas_call(
        _mla_kv_kernel,
        grid=(n // tm,),
        in_specs=[pl.BlockSpec((tm, KV_LORA), lambda i: (i, 0)),
                  pl.BlockSpec((tm, HEAD_SLAB), lambda i: (i, 0)),
                  pl.BlockSpec(wk_bf16.shape, lambda i: (0, 0)),
                  pl.BlockSpec(wv_bf16.shape, lambda i: (0, 0))],
        out_specs=[pl.BlockSpec((tm, MLA_HEADS * HEAD_SLAB), lambda i: (i, 0)),
                   pl.BlockSpec((tm, MLA_HEADS * LANES), lambda i: (i, 0))],
        out_shape=[jax.ShapeDtypeStruct((n, MLA_HEADS * HEAD_SLAB), BF16),
                   jax.ShapeDtypeStruct((n, MLA_HEADS * LANES), BF16)],
        compiler_params=_cparams(("parallel",)),
        name="mla_kv",
    )(ckv, slab, wk_bf16, wv_bf16)


MLA_KEY_CHUNK = 256


def _mla_attn_kernel(q_ref, *rest, n_sources):
    k_refs = rest[:n_sources]
    v_refs = rest[n_sources:2 * n_sources]
    o_ref = rest[2 * n_sources]
    tq = q_ref.shape[0]
    lane = lax.broadcasted_iota(jnp.int32, (tq, LANES), 1)
    sls = [slice(j * HEAD_SLAB, (j + 1) * HEAD_SLAB) for j in range(2)]
    qs = [q_ref[:, sl] for sl in sls]
    chunks = [(si, st) for si, k in enumerate(k_refs) for st in range(0, k.shape[0], MLA_KEY_CHUNK)]

    def score(j, ch):
        si, st = ch
        return _nt_dot(qs[j], k_refs[si][st:st + MLA_KEY_CHUNK, sls[j]])

    def row_max(ss):
        return jnp.max(functools.reduce(jnp.maximum, ss), axis=-1, keepdims=True)

    def weighted(j, ch, p):
        si, st = ch
        return _dot(p, v_refs[si][st:st + MLA_KEY_CHUNK, sls[j]])

    s0 = [score(0, ch) for ch in chunks]
    m0 = row_max(s0)
    s1, p0 = [], []
    for c, ch in enumerate(chunks):
        s1.append(score(1, ch))
        p0.append(jnp.exp2(s0[c] - m0).astype(BF16))
    m1 = row_max(s1)
    o0, p1 = None, []
    for c, ch in enumerate(chunks):
        t = weighted(0, ch, p0[c])
        o0 = t if o0 is None else o0 + t
        p1.append(jnp.exp2(s1[c] - m1).astype(BF16))
    o1 = functools.reduce(jnp.add, [weighted(1, ch, p1[c]) for c, ch in enumerate(chunks)])
    outs = [o / pltpu.roll(o, V_HEAD, axis=1) for o in (o0, o1)]
    o_ref[...] = jnp.where(lane < V_HEAD, outs[0], outs[1]).astype(BF16)


def mla_attention(q, k_new, v_new, k_ctx, v_ctx, n_batch, tq=256):
    n = q.shape[0]
    seq = n // n_batch
    tq = min(tq, seq)
    qt = seq // tq
    pairs = MLA_HEADS // 2
    srcs_k, srcs_v = [k_new], [v_new]
    lens = [seq]
    if k_ctx is not None:
        srcs_k.append(k_ctx)
        srcs_v.append(v_ctx)
        lens.append(k_ctx.shape[0] // n_batch)
    in_specs = [pl.BlockSpec((tq, 2 * HEAD_SLAB), lambda b, p, t: (b * qt + t, p))]
    in_specs += [pl.BlockSpec((ln, 2 * HEAD_SLAB), lambda b, p, t: (b, p)) for ln in lens]
    in_specs += [pl.BlockSpec((ln, 2 * LANES), lambda b, p, t: (b, p)) for ln in lens]
    return pl.pallas_call(
        functools.partial(_mla_attn_kernel, n_sources=len(lens)),
        grid=(n_batch, pairs, qt),
        in_specs=in_specs,
        out_specs=pl.BlockSpec((tq, 2 * V_HEAD), lambda b, p, t: (b * qt + t, p)),
        out_shape=jax.ShapeDtypeStruct((n, MLA_HEADS * V_HEAD), BF16),
        compiler_params=_cparams(("parallel", "parallel", "arbitrary")),
        name="mla_attention",
    )(q, *srcs_k, *srcs_v)


def rope_lane_tables(n):
    t = jnp.arange(n)
    row = (t // GRID_W).astype(F32)
    col = (t % GRID_W).astype(F32)
    inv = ROPE_THETA ** (-jnp.arange(ROPE_PAIRS_AXIS, dtype=F32) / ROPE_PAIRS_AXIS)
    ang = jnp.concatenate([row[:, None] * inv, col[:, None] * inv], axis=-1)
    cos, sin = jnp.cos(ang), jnp.sin(ang)
    ones = jnp.ones((n, QK_NOPE), F32)
    zeros = jnp.zeros((n, QK_NOPE), F32)
    tail1 = jnp.ones((n, HEAD_SLAB - QK_NOPE - QK_ROPE), F32)
    tail0 = jnp.zeros((n, HEAD_SLAB - QK_NOPE - QK_ROPE), F32)
    z16 = jnp.zeros_like(sin)
    cos_t = jnp.concatenate([ones, cos, cos, tail1], axis=-1)
    sin_fwd = jnp.concatenate([zeros, z16, sin, tail0], axis=-1)
    sin_bwd = jnp.concatenate([zeros, -sin, z16, tail0], axis=-1)
    return cos_t, sin_fwd, sin_bwd


def _pad_mod(m, groups):
    m = m.reshape(groups, N_MOD, D_MODEL)
    return jnp.pad(m, ((0, 0), (0, MOD_ROWS - N_MOD), (0, 0)))


def _mla_weights(w_in, w_uq, w_ukv):
    pad_tail = HEAD_SLAB - QK_NOPE - QK_ROPE
    slab_cols = jnp.pad(w_in[:, Q_LORA + KV_LORA:], ((0, 0), (QK_NOPE, pad_tail)))
    win = jnp.concatenate([w_in[:, :Q_LORA + KV_LORA], slab_cols], axis=1).astype(BF16)
    wuq = jnp.pad(w_uq.reshape(Q_LORA, MLA_HEADS, QK_NOPE + QK_ROPE), ((0, 0), (0, 0), (0, pad_tail)))
    wuq = wuq.reshape(Q_LORA, MLA_HEADS * HEAD_SLAB).astype(BF16)
    wkv = w_ukv.reshape(KV_LORA, MLA_HEADS, QK_NOPE + V_HEAD)
    wk = jnp.pad(wkv[:, :, :QK_NOPE], ((0, 0), (0, 0), (0, HEAD_SLAB - QK_NOPE)))
    wk = wk.reshape(KV_LORA, MLA_HEADS * HEAD_SLAB).astype(BF16)
    wv = wkv[:, :, QK_NOPE:].reshape(KV_LORA, MLA_HEADS * V_HEAD).astype(BF16)
    return win, wuq, wk, wv


def kernel(x_prompt, x_sample, cache_na_k, cache_na_v, cache_mla_ckv, cache_mla_krope, c, c_ctx, w_ada, b_ada, ln_mix_g, ln_mix_b, ln_ffn_g, ln_ffn_b, w_in_ab, conv_w, na_rpb, w_out_ab, w_in_mla, q_norm_g, w_uq, kv_norm_g, w_ukv, w_out_mla, peer_w_q, peer_subkeys, peer_u, peer_v):
    nb_p, seq_p, d = x_prompt.shape
    nb_s, seq_s, _ = x_sample.shape
    past = cache_na_k.shape[2]
    rows = seq_s // GRID_W
    n_p = nb_p * seq_p
    xp = x_prompt.reshape(n_p, d)
    xs = x_sample.reshape(nb_s * seq_s, d)

    n_cond = 1 + nb_s
    cond_rows = -(-n_cond // 8) * 8
    cvec = jnp.concatenate([c_ctx[None], c, jnp.zeros((cond_rows - n_cond, d), F32)], axis=0)

    outs = {}
    for i in range(DEPTH):
        j = i // 2
        mods = ada_modulation(cvec, w_ada[i], b_ada[i])
        mod_p = _pad_mod(mods[0:1], 1)
        mod_s = _pad_mod(mods[1:n_cond], nb_s)
        lg = ln_mix_g[i].reshape(1, d)
        lb = ln_mix_b[i].reshape(1, d)
        if i % 2 == 0:
            w_in = w_in_ab[j].astype(BF16)
            w_out = w_out_ab[j].astype(BF16)
            cw = jnp.pad(conv_w[j], ((0, 8 - CONV_TAPS), (0, 0)))
            qkv_p, gu_p, kv_p = inproj_ab(xp, mod_p, w_in, n_p, True)
            qkv_s, gu_s = inproj_ab(xs, mod_s, w_in, seq_s, False)
            outs["na_k"] = kv_p[:, :NA_WIDTH].reshape(nb_p, 1, seq_p, NA_HEADS, NA_HEAD_DIM)
            outs["na_v"] = kv_p[:, NA_WIDTH:].reshape(nb_p, 1, seq_p, NA_HEADS, NA_HEAD_DIM)
            att_p = ctx_attention(qkv_p, seq_p)
            kc = cache_na_k[:, j].reshape(nb_s * past, NA_WIDTH).astype(BF16)
            vc = cache_na_v[:, j].reshape(nb_s * past, NA_WIDTH).astype(BF16)
            att_s = na_attention(qkv_s, kc, vc, na_bias_table(na_rpb[j]), nb_s, rows)
            xp = outproj_ab(att_p, gu_p, xp, mod_p, w_out, cw, lg, lb, n_p, seq_p)
            xs = outproj_ab(att_s, gu_s, xs, mod_s, w_out, cw, lg, lb, seq_s, seq_s)
        else:
            win, wuq, wk, wv = _mla_weights(w_in_mla[j], w_uq[j], w_ukv[j])
            w_out = w_out_mla[j].astype(BF16)
            qg = q_norm_g[j].reshape(1, Q_LORA)
            kvg = kv_norm_g[j].reshape(1, KV_LORA)
            q_p, ckv_p, slab_p = mla_inproj(xp, mod_p, win, qg, kvg, wuq, None, n_p, seq_p)
            q_s, ckv_s, slab_s = mla_inproj(xs, mod_s, win, qg, kvg, wuq, rope_lane_tables(seq_s), seq_s, seq_s)
            outs["ckv"] = ckv_p.reshape(nb_p, 1, seq_p, KV_LORA)
            outs["kr"] = slab_p[:, QK_NOPE:QK_NOPE + QK_ROPE].reshape(nb_p, 1, seq_p, QK_ROPE)
            k_p, v_p = mla_kv(ckv_p, slab_p, wk, wv)
            k_s, v_s = mla_kv(ckv_s, slab_s, wk, wv)
            ckv_c = cache_mla_ckv[:, j].reshape(nb_s * past, KV_LORA)
            slab_c = jnp.pad(cache_mla_krope[:, j].reshape(nb_s * past, QK_ROPE),
                             ((0, 0), (QK_NOPE, HEAD_SLAB - QK_NOPE - QK_ROPE)))
            k_c, v_c = mla_kv(ckv_c, slab_c, wk, wv)
            att_p = mla_attention(q_p, k_p, v_p, None, None, nb_p)
            att_s = mla_attention(q_s, k_s, v_s, k_c, v_c, nb_s)
            xp = outproj(att_p, xp, mod_p, w_out, lg, lb, n_p)
            xs = outproj(att_s, xs, mod_s, w_out, lg, lb, seq_s)
        wq = peer_w_q[i].astype(BF16)
        sk = peer_subkeys[i].reshape(2 * PEER_HEADS, N_KEYS, PEER_HALF).astype(BF16)
        u = peer_u[i].astype(BF16)
        vt = peer_v[i].T.astype(BF16)
        fg = ln_ffn_g[i].reshape(1, d)
        fb = ln_ffn_b[i].reshape(1, d)
        xp = peer_block(xp, mod_p, wq, sk, u, vt, fg, fb, n_p)
        xs = peer_block(xs, mod_s, wq, sk, u, vt, fg, fb, seq_s)
    return (xp.reshape(nb_p, seq_p, d), xs.reshape(nb_s, seq_s, d),
            outs["na_k"], outs["na_v"], outs["ckv"], outs["kr"])
```

```python
import functools

import jax
import jax.numpy as jnp
import numpy as np
from jax import lax
from jax.experimental import pallas as pl
from jax.experimental.pallas import tpu as pltpu

F32 = jnp.float32
BF16 = jnp.bfloat16

D_MODEL = 1024
DEPTH = 2
GRID_W = 64
N_MOD = 6
ALPHA = (2.0 * DEPTH) ** 0.25
LN_EPS = 1e-5
RMS_EPS = 1e-6
NA_HEADS = 8
NA_HEAD_DIM = 64
NA_WIDTH = NA_HEADS * NA_HEAD_DIM
NA_MAX_ROWS = 8
NA_WIN_COLS = 16
NA_SCALE = NA_HEAD_DIM ** -0.5
CONV_WIDTH = 512
CONV_TAPS = 3
MLA_HEADS = 16
Q_LORA = 256
KV_LORA = 256
QK_NOPE = 64
QK_ROPE = 32
V_HEAD = 64
MLA_SCALE = (QK_NOPE + QK_ROPE) ** -0.5
ROPE_THETA = 10000.0
ROPE_PAIRS_AXIS = QK_ROPE // 4
PEER_HEADS = 8
PEER_HALF = 128
N_KEYS = 128
N_EXPERTS = N_KEYS * N_KEYS
PEER_TOPK = 16

LANES = 128
MOD_ROWS = 8
HEAD_SLAB = 128
MASK_BIAS = -1e30
VMEM_LIMIT = 52 * 1024 * 1024


def _cparams(sem, flags=None):
    return pltpu.CompilerParams(dimension_semantics=sem, vmem_limit_bytes=VMEM_LIMIT, flags=flags)


def _nt_dot(a, b):
    return lax.dot_general(a, b, (((1,), (1,)), ((), ())), preferred_element_type=F32)


def _dot(a, b):
    return jnp.dot(a, b, preferred_element_type=F32)


def _split_bf16(a):
    hi = a.astype(BF16)
    lo = (a - hi.astype(F32)).astype(BF16)
    return hi, lo


def _layer_norm(z, g, b):
    mu = jnp.mean(z, axis=-1, keepdims=True)
    zc = z - mu
    var = jnp.mean(zc * zc, axis=-1, keepdims=True)
    return zc * lax.rsqrt(var + LN_EPS) * g + b


LOG2E = 1.4426950408889634
LN2 = 0.6931471805599453
_GELU_A = -2.0 * LOG2E * 0.7978845608028654
_GELU_B = _GELU_A * 0.044715


def _gelu_tanh(x):
    return x / (1.0 + jnp.exp2(x * (_GELU_A + _GELU_B * (x * x))))


def _ada_kernel(c_ref, w_ref, b_ref, o_ref):
    c = c_ref[...]
    a = c * (1.0 / (1.0 + jnp.exp(-c)))
    a_hi, a_lo = _split_bf16(a)
    w_hi, w_lo = _split_bf16(w_ref[...])
    o_ref[...] = _dot(a_hi, w_hi) + _dot(a_hi, w_lo) + _dot(a_lo, w_hi) + b_ref[...]


def ada_modulation(cvec, w_ada, b_ada, tn=512):
    r, d = cvec.shape
    n = w_ada.shape[1]
    return pl.pallas_call(
        _ada_kernel,
        grid=(n // tn,),
        in_specs=[pl.BlockSpec((r, d), lambda j: (0, 0)),
                  pl.BlockSpec((d, tn), lambda j: (0, j)),
                  pl.BlockSpec((1, tn), lambda j: (0, j))],
        out_specs=pl.BlockSpec((r, tn), lambda j: (0, j)),
        out_shape=jax.ShapeDtypeStruct((r, n), F32),
        compiler_params=_cparams(("arbitrary",)),
        name="ada_modulation",
    )(cvec, w_ada, b_ada.reshape(1, n))


def _inproj_ab_kernel(x_ref, mod_ref, w_ref, qkv_ref, gu_ref, *kv_ref):
    m = mod_ref[0]
    h = x_ref[...] * (1.0 + m[1:2]) + m[0:1]
    y = _dot(h.astype(BF16), w_ref[...])
    w3 = 3 * NA_WIDTH
    qkv_ref[:, :NA_WIDTH] = (y[:, :NA_WIDTH] * (NA_SCALE * LOG2E)).astype(BF16)
    qkv_ref[:, NA_WIDTH:] = y[:, NA_WIDTH:w3].astype(BF16)
    if kv_ref:
        kv_ref[0][...] = y[:, NA_WIDTH:w3]
    gu_ref[:, :CONV_WIDTH] = y[:, w3:w3 + CONV_WIDTH].astype(BF16)
    gu_ref[:, CONV_WIDTH:] = (y[:, w3 + CONV_WIDTH:w3 + 2 * CONV_WIDTH]
                              * y[:, w3 + 2 * CONV_WIDTH:]).astype(BF16)


def inproj_ab(x, mod, w_bf16, tokens_per_group, emit_kv, tm=512):
    n, d = x.shape
    tm = min(tm, n)
    out_shape = [jax.ShapeDtypeStruct((n, 3 * NA_WIDTH), BF16),
                 jax.ShapeDtypeStruct((n, 2 * CONV_WIDTH), BF16)]
    out_specs = [pl.BlockSpec((tm, 3 * NA_WIDTH), lambda i: (i, 0)),
                 pl.BlockSpec((tm, 2 * CONV_WIDTH), lambda i: (i, 0))]
    if emit_kv:
        out_shape.append(jax.ShapeDtypeStruct((n, 2 * NA_WIDTH), F32))
        out_specs.append(pl.BlockSpec((tm, 2 * NA_WIDTH), lambda i: (i, 0)))
    return pl.pallas_call(
        _inproj_ab_kernel,
        grid=(n // tm,),
        in_specs=[pl.BlockSpec((tm, d), lambda i: (i, 0)),
                  pl.BlockSpec((1, MOD_ROWS, d), lambda i: ((i * tm) // tokens_per_group, 0, 0)),
                  pl.BlockSpec(w_bf16.shape, lambda i: (0, 0))],
        out_specs=out_specs,
        out_shape=out_shape,
        compiler_params=_cparams(("parallel",)),
        name="inproj_ab",
    )(x, mod, w_bf16)


def _pair_attention(instances):
    heads = []
    for n, (q_pair, _, _, _) in enumerate(instances):
        lane = lax.broadcasted_iota(jnp.int32, q_pair.shape, 1)
        qf = q_pair.astype(F32)
        for j in range(2):
            in_head = (lane >= j * NA_HEAD_DIM) & (lane < (j + 1) * NA_HEAD_DIM)
            heads.append((n, j, jnp.where(in_head, qf, 0.0).astype(BF16)))
    scores = []
    for n, j, qm in heads:
        _, keys, _, biases = instances[n]
        ss = [_nt_dot(qm, k) for k in keys]
        scores.append([s if b is None else s + b for s, b in zip(ss, biases[j])])
    mxs = [functools.reduce(jnp.maximum, [jnp.max(s, axis=-1, keepdims=True) for s in ss]) for ss in scores]
    ess = [[jnp.exp2(s - mx) for s in ss] for ss, mx in zip(scores, mxs)]
    dens = [functools.reduce(jnp.add, [jnp.sum(e, axis=-1, keepdims=True) for e in es]) for es in ess]
    outs = []
    for (n, j, _), es in zip(heads, ess):
        values = instances[n][2]
        outs.append(functools.reduce(jnp.add, [_dot(e.astype(BF16), v) for e, v in zip(es, values)]))
    outs = [o / d for o, d in zip(outs, dens)]
    lane = lax.broadcasted_iota(jnp.int32, outs[0].shape, 1)
    return [jnp.where(lane < NA_HEAD_DIM, outs[2 * n], outs[2 * n + 1]) for n in range(len(instances))]


NA_ROWS_PER_STEP = 2


def _na_attn_kernel(q_ref, k_ref, v_ref, kc_ref, vc_ref, *rest, rows):
    bias_refs = rest[:NA_ROWS_PER_STEP]
    o_ref = rest[NA_ROWS_PER_STEP]
    band = NA_MAX_ROWS * GRID_W
    instances, places = [], []
    for rr in range(NA_ROWS_PER_STEP):
        r = pl.program_id(1) * NA_ROWS_PER_STEP + rr
        rs = jnp.clip(r - NA_MAX_ROWS // 2, 0, rows - NA_MAX_ROWS)
        start = pl.multiple_of(rs * GRID_W, GRID_W)
        qrows = slice(rr * GRID_W, (rr + 1) * GRID_W)
        for p in range(NA_HEADS // 2):
            sl = slice(p * LANES, (p + 1) * LANES)
            keys = [k_ref[pl.ds(start, band), sl], kc_ref[:, sl]]
            values = [v_ref[pl.ds(start, band), sl], vc_ref[:, sl]]
            biases = [[bias_refs[rr][0, 2 * p + j], None] for j in range(2)]
            instances.append((q_ref[qrows, sl], keys, values, biases))
            places.append((qrows, sl))
    for (qrows, sl), o in zip(places, _pair_attention(instances)):
        o_ref[qrows, sl] = o.astype(BF16)


def na_attention(qkv, kc, vc, bias, n_batch, rows):
    n = qkv.shape[0]
    seq = rows * GRID_W
    lc = kc.shape[0] // n_batch
    half = NA_MAX_ROWS // 2

    steps = rows // NA_ROWS_PER_STEP
    tq = NA_ROWS_PER_STEP * GRID_W

    def bias_spec(rr):
        def idx(b, g):
            r = g * NA_ROWS_PER_STEP + rr
            return (r - jnp.clip(r - half, 0, rows - NA_MAX_ROWS), 0, 0, 0)
        return pl.BlockSpec((1, NA_HEADS, GRID_W, NA_MAX_ROWS * GRID_W), idx)

    return pl.pallas_call(
        functools.partial(_na_attn_kernel, rows=rows),
        grid=(n_batch, steps),
        in_specs=[pl.BlockSpec((tq, NA_WIDTH), lambda b, g: (b * steps + g, 0)),
                  pl.BlockSpec((seq, NA_WIDTH), lambda b, g: (b, 1)),
                  pl.BlockSpec((seq, NA_WIDTH), lambda b, g: (b, 2)),
                  pl.BlockSpec((lc, NA_WIDTH), lambda b, g: (b, 0)),
                  pl.BlockSpec((lc, NA_WIDTH), lambda b, g: (b, 0))]
        + [bias_spec(rr) for rr in range(NA_ROWS_PER_STEP)],
        out_specs=pl.BlockSpec((tq, NA_WIDTH), lambda b, g: (b * steps + g, 0)),
        out_shape=jax.ShapeDtypeStruct((n, NA_WIDTH), BF16),
        compiler_params=_cparams(("parallel", "arbitrary")),
        name="na_attention",
    )(qkv, qkv, qkv, kc, vc, *([bias] * NA_ROWS_PER_STEP))


def na_bias_table(rpb):
    c = np.arange(GRID_W)[:, None]
    kc = np.arange(GRID_W)[None, :]
    cs = np.clip(c - NA_WIN_COLS // 2, 0, GRID_W - NA_WIN_COLS)
    valid = (kc >= cs) & (kc < cs + NA_WIN_COLS)
    dc = kc - c + (NA_WIN_COLS - 1)
    cc, kk = np.nonzero(valid)
    place = np.zeros((2 * NA_WIN_COLS - 1, GRID_W, GRID_W), np.float32)
    place[dc[cc, kk], cc, kk] = 1.0
    t = jnp.einsum("hrd,dck->hrck", rpb, place, precision=lax.Precision.HIGHEST)
    t = jnp.where(valid[None, None], t * LOG2E, MASK_BIAS)
    bands = [t[:, NA_MAX_ROWS - 1 - off:2 * NA_MAX_ROWS - 1 - off] for off in range(NA_MAX_ROWS)]
    b = jnp.stack(bands, axis=0)
    b = jnp.transpose(b, (0, 1, 3, 2, 4))
    return b.reshape(NA_MAX_ROWS, NA_HEADS, GRID_W, NA_MAX_ROWS * GRID_W).astype(F32)


def _ctx_attn_kernel(q_ref, k_ref, v_ref, o_ref):
    slabs = [slice(p * LANES, (p + 1) * LANES) for p in range(NA_HEADS // 2)]
    instances = [(q_ref[:, sl], [k_ref[:, sl]], [v_ref[:, sl]], [[None], [None]]) for sl in slabs]
    for sl, o in zip(slabs, _pair_attention(instances)):
        o_ref[:, sl] = o.astype(BF16)


def ctx_attention(qkv, seq):
    n = qkv.shape[0]
    return pl.pallas_call(
        _ctx_attn_kernel,
        grid=(n // seq,),
        in_specs=[pl.BlockSpec((seq, NA_WIDTH), lambda b: (b, 0)),
                  pl.BlockSpec((seq, NA_WIDTH), lambda b: (b, 1)),
                  pl.BlockSpec((seq, NA_WIDTH), lambda b: (b, 2))],
        out_specs=pl.BlockSpec((seq, NA_WIDTH), lambda b: (b, 0)),
        out_shape=jax.ShapeDtypeStruct((n, NA_WIDTH), BF16),
        compiler_params=_cparams(("parallel",)),
        name="ctx_attention",
    )(qkv, qkv, qkv)


HALO = 16


def _outproj_ab_kernel(att_ref, gu_ref, prev_ref, next_ref, x_ref, mod_ref, w_ref, cw_ref, g_ref, b_ref,
                       o_ref, *, tiles_per_seq):
    i = pl.program_id(0)
    tm = x_ref.shape[0]
    m = mod_ref[0]
    gu = gu_ref[...].astype(F32)
    bg = gu[:, :CONV_WIDTH]
    u = gu[:, CONV_WIDTH:]
    has_prev = (i % tiles_per_seq != 0).astype(F32)
    has_next = (i % tiles_per_seq != tiles_per_seq - 1).astype(F32)
    prev_row = prev_ref[HALO - 1:HALO, CONV_WIDTH:].astype(F32) * has_prev
    next_row = next_ref[0:1, CONV_WIDTH:].astype(F32) * has_next
    row = lax.broadcasted_iota(jnp.int32, u.shape, 0)
    u_m1 = jnp.where(row == 0, prev_row, pltpu.roll(u, 1, axis=0))
    u_p1 = jnp.where(row == tm - 1, next_row, pltpu.roll(u, tm - 1, axis=0))
    cw = cw_ref[...]
    conv = bg * (u_m1 * cw[0:1] + u * cw[1:2] + u_p1 * cw[2:3])
    y = _dot(att_ref[...], w_ref[:NA_WIDTH, :]) + _dot(conv.astype(BF16), w_ref[NA_WIDTH:, :])
    z = ALPHA * x_ref[...] + m[2:3] * y
    o_ref[...] = _layer_norm(z, g_ref[...], b_ref[...])


def outproj_ab(att, gu, x, mod, w_bf16, conv_w8, ln_g, ln_b, tokens_per_group, seq, tm=256):
    n, d = x.shape
    tm = min(tm, seq)
    hb = tm // HALO
    last_halo = n // HALO - 1
    return pl.pallas_call(
        functools.partial(_outproj_ab_kernel, tiles_per_seq=seq // tm),
        grid=(n // tm,),
        in_specs=[pl.BlockSpec((tm, NA_WIDTH), lambda i: (i, 0)),
                  pl.BlockSpec((tm, 2 * CONV_WIDTH), lambda i: (i, 0)),
                  pl.BlockSpec((HALO, 2 * CONV_WIDTH), lambda i: (jnp.maximum(i * hb - 1, 0), 0)),
                  pl.BlockSpec((HALO, 2 * CONV_WIDTH), lambda i: (jnp.minimum((i + 1) * hb, last_halo), 0)),
                  pl.BlockSpec((tm, d), lambda i: (i, 0)),
                  pl.BlockSpec((1, MOD_ROWS, d), lambda i: ((i * tm) // tokens_per_group, 0, 0)),
                  pl.BlockSpec(w_bf16.shape, lambda i: (0, 0)),
                  pl.BlockSpec(conv_w8.shape, lambda i: (0, 0)),
                  pl.BlockSpec((1, d), lambda i: (0, 0)),
                  pl.BlockSpec((1, d), lambda i: (0, 0))],
        out_specs=pl.BlockSpec((tm, d), lambda i: (i, 0)),
        out_shape=jax.ShapeDtypeStruct((n, d), F32),
        compiler_params=_cparams(("parallel",)),
        name="outproj_ab",
    )(att, gu, gu, gu, x, mod, w_bf16, conv_w8, ln_g, ln_b)


def _outproj_kernel(att_ref, x_ref, mod_ref, w_ref, g_ref, b_ref, o_ref):
    m = mod_ref[0]
    y = _dot(att_ref[...], w_ref[...])
    z = ALPHA * x_ref[...] + m[2:3] * y
    o_ref[...] = _layer_norm(z, g_ref[...], b_ref[...])


def outproj(att, x, mod, w_bf16, ln_g, ln_b, tokens_per_group, tm=512):
    n, d = x.shape
    k = att.shape[1]
    return pl.pallas_call(
        _outproj_kernel,
        grid=(n // tm,),
        in_specs=[pl.BlockSpec((tm, k), lambda i: (i, 0)),
                  pl.BlockSpec((tm, d), lambda i: (i, 0)),
                  pl.BlockSpec((1, MOD_ROWS, d), lambda i: ((i * tm) // tokens_per_group, 0, 0)),
                  pl.BlockSpec(w_bf16.shape, lambda i: (0, 0)),
                  pl.BlockSpec((1, d), lambda i: (0, 0)),
                  pl.BlockSpec((1, d), lambda i: (0, 0))],
        out_specs=pl.BlockSpec((tm, d), lambda i: (i, 0)),
        out_shape=jax.ShapeDtypeStruct((n, d), F32),
        compiler_params=_cparams(("parallel",)),
        name="outproj",
    )(att, x, mod, w_bf16, ln_g, ln_b)


def _peer_scores_kernel(x_ref, mod_ref, wq_ref, sk_ref, hm_ref, st_ref):
    m = mod_ref[0]
    hb = (x_ref[...] * (1.0 + m[4:5]) + m[3:4]).astype(BF16)
    hm_ref[...] = hb
    q = _dot(hb, wq_ref[...])
    for c in range(2 * PEER_HEADS):
        qc = q[:, c * PEER_HALF:(c + 1) * PEER_HALF].astype(BF16)
        st_ref[c] = _nt_dot(sk_ref[c], qc) * LOG2E


def peer_scores(x, mod, wq_bf16, subkeys_bf16, tokens_per_group, tm=512):
    n, d = x.shape
    nc = 2 * PEER_HEADS
    return pl.pallas_call(
        _peer_scores_kernel,
        grid=(n // tm,),
        in_specs=[pl.BlockSpec((tm, d), lambda i: (i, 0)),
                  pl.BlockSpec((1, MOD_ROWS, d), lambda i: ((i * tm) // tokens_per_group, 0, 0)),
                  pl.BlockSpec(wq_bf16.shape, lambda i: (0, 0)),
                  pl.BlockSpec(subkeys_bf16.shape, lambda i: (0, 0, 0))],
        out_specs=[pl.BlockSpec((tm, d), lambda i: (i, 0)),
                   pl.BlockSpec((nc, N_KEYS, tm), lambda i: (0, 0, i))],
        out_shape=[jax.ShapeDtypeStruct((n, d), BF16),
                   jax.ShapeDtypeStruct((nc, N_KEYS, n), F32)],
        compiler_params=_cparams(("parallel",)),
        name="peer_scores",
    )(x, mod, wq_bf16, subkeys_bf16)


SUBLANES = 8
N_SLABS = N_KEYS // SUBLANES


def _sort_network(n):
    pairs = []
    p = 1
    while p < n:
        k = p
        while k >= 1:
            for j in range(k % p, n - k, 2 * k):
                for i in range(min(k, n - j - k)):
                    if (i + j) // (2 * p) == (i + j + k) // (2 * p):
                        pairs.append((i + j, i + j + k))
            k //= 2
        p *= 2
    return pairs


_SLAB_SORT = _sort_network(N_SLABS)


def _top_values(s_ref, c, n_top):
    x = [s_ref[c, r] for r in range(N_SLABS)]
    for a, b in _SLAB_SORT:
        x[a], x[b] = jnp.maximum(x[a], x[b]), jnp.minimum(x[a], x[b])
    sub = lax.broadcasted_iota(jnp.int32, x[0].shape, 0).astype(F32)
    neg = jnp.full(x[0].shape, -jnp.inf, F32)
    x.append(neg)
    vals = []
    for k in range(n_top):
        mx = jnp.max(x[0], axis=0, keepdims=True)
        vals.append(mx)
        if k == n_top - 1:
            break
        first = jnp.min(jnp.where(x[0] == mx, sub, float(SUBLANES)), axis=0, keepdims=True)
        pop = sub == first
        for r in range(n_top - 1 - k):
            x[r] = jnp.where(pop, x[r + 1], x[r])
    return vals


def _rows_to_slab(rows_list):
    t = rows_list[0].shape[1]
    ridx = lax.broadcasted_iota(jnp.int32, (SUBLANES, t), 0)
    out = jnp.full((SUBLANES, t), -jnp.inf, F32)
    for r, row in enumerate(rows_list):
        out = jnp.where(ridx == r, row, out)
    return out


def _peer_select_kernel(st_ref, thr_ref, lse_ref, m2_ref):
    k = PEER_TOPK
    for h in range(PEER_HEADS):
        v1 = _top_values(st_ref, 2 * h, k + 1)
        v2 = _top_values(st_ref, 2 * h + 1, k + 1)
        v1_lo, v1_hi = _rows_to_slab(v1[0:8]), _rows_to_slab(v1[8:16])
        v2_lo, v2_hi = _rows_to_slab(v2[0:8]), _rows_to_slab(v2[8:16])
        ridx = lax.broadcasted_iota(jnp.int32, v1_lo.shape, 0)
        v1_mid = jnp.where(ridx >= 4, v1_lo, -jnp.inf)
        cands = [v1[0] + v2_lo, v1[0] + v2_hi, v1[1] + v2_lo, v1[2] + v2_lo, v1[3] + v2_lo,
                 v1_hi + v2[0], v1_mid + v2[0], v1_mid + v2[1], v1_mid + v2[2],
                 _rows_to_slab([v1[k] + v2[0], v1[0] + v2[k]])]
        top = v1[0] + v2[0]
        work = list(cands)
        thr = jnp.full_like(top, -jnp.inf)
        nxt = jnp.full_like(top, -jnp.inf)
        seen = jnp.zeros_like(top)
        for _ in range(k + 1):
            mx = jnp.max(functools.reduce(jnp.maximum, work), axis=0, keepdims=True)
            eqs = [w == mx for w in work]
            cnt = functools.reduce(jnp.add, [jnp.where(e, 1.0, 0.0) for e in eqs])
            thr = jnp.where(seen < k, mx, thr)
            nxt = jnp.where(seen < k + 1, mx, nxt)
            seen = seen + jnp.sum(cnt, axis=0, keepdims=True)
            work = [jnp.where(e, -jnp.inf, w) for e, w in zip(eqs, work)]
        z = functools.reduce(jnp.add, [jnp.where(cd >= thr, jnp.exp2(cd - top), 0.0) for cd in cands])
        z = jnp.sum(z, axis=0, keepdims=True)
        thr_ref[h:h + 1, :] = 0.5 * (thr + nxt)
        lse_ref[h:h + 1, :] = v1[0] + jnp.log(z) * LOG2E
        m2_ref[h:h + 1, :] = v2[0]


def peer_select(st, tl=256):
    nc, _, _, n = st.shape
    stat = pl.BlockSpec((PEER_HEADS, tl), lambda i: (0, i))
    return pl.pallas_call(
        _peer_select_kernel,
        grid=(n // tl,),
        in_specs=[pl.BlockSpec((nc, N_SLABS, SUBLANES, tl), lambda i: (0, 0, 0, i))],
        out_specs=[stat, stat, stat],
        out_shape=[jax.ShapeDtypeStruct((PEER_HEADS, n), F32)] * 3,
        compiler_params=_cparams(("parallel",)),
        name="peer_select",
    )(st)


PEER_LANE_CHUNK = 128
PEER_MXU_CHUNK = 256


def _peer_dense_kernel(hm_ref, u_ref, vt_ref, st_ref, thr_ref, lse_ref, m2_ref, x_ref, mod_ref, g_ref, b_ref,
                       o_ref, acc_ref, ht_ref, at_ref, e2_ref, cut_rep, f1_rep, *, gate_row):
    e = pl.program_id(1)
    tm = hm_ref.shape[0]

    @pl.when(e == 0)
    def _():
        acc_ref[...] = jnp.zeros_like(acc_ref)
        for h in range(PEER_HEADS):
            e2_ref[h] = jnp.exp2(st_ref[2 * h + 1] - m2_ref[h:h + 1, :])

    for c in range(tm // PEER_MXU_CHUNK):
        ls = slice(c * PEER_MXU_CHUNK, (c + 1) * PEER_MXU_CHUNK)
        ht_ref[:, ls] = _nt_dot(u_ref[...], hm_ref[ls, :])

    for h in range(PEER_HEADS):
        s1 = st_ref[2 * h, e]
        cut_rows = thr_ref[h:h + 1, :] - s1
        f1_rows = jnp.exp2(s1 - lse_ref[h:h + 1, :])
        for ii in range(SUBLANES):
            cut_rep[h * SUBLANES + ii] = jnp.broadcast_to(cut_rows[ii:ii + 1, :], (SUBLANES, tm))
            f1_rep[h * SUBLANES + ii] = jnp.broadcast_to(f1_rows[ii:ii + 1, :], (SUBLANES, tm))
    lanes_per_mxu = PEER_MXU_CHUNK // PEER_LANE_CHUNK
    for cm in range(tm // PEER_MXU_CHUNK):
        for ii in range(SUBLANES):
            rows = slice(ii * N_KEYS, (ii + 1) * N_KEYS)
            for c in range(cm * lanes_per_mxu, (cm + 1) * lanes_per_mxu):
                ls = slice(c * PEER_LANE_CHUNK, (c + 1) * PEER_LANE_CHUNK)
                w = jnp.zeros((N_SLABS, SUBLANES, PEER_LANE_CHUNK), F32)
                for h in range(PEER_HEADS):
                    sel = st_ref[2 * h + 1, :, :, ls] >= cut_rep[h * SUBLANES + ii, :, ls][None]
                    w = w + jnp.where(sel, e2_ref[h, :, :, ls] * f1_rep[h * SUBLANES + ii, :, ls][None], 0.0)
                act = _gelu_tanh(ht_ref[rows, ls])
                at_ref[rows, ls] = (act * w.reshape(N_KEYS, PEER_LANE_CHUNK)).astype(BF16)
        ms = slice(cm * PEER_MXU_CHUNK, (cm + 1) * PEER_MXU_CHUNK)
        acc_ref[:, ms] += _dot(vt_ref[...], at_ref[:, ms])

    @pl.when(e == pl.num_programs(1) - 1)
    def _():
        m = mod_ref[0]
        z = ALPHA * x_ref[...] + m[gate_row:gate_row + 1] * acc_ref[...].T
        o_ref[...] = _layer_norm(z, g_ref[...], b_ref[...])


def peer_dense(hm, u_bf16, vt_bf16, st, thr, lse, m2, x, mod, ln_g, ln_b, tokens_per_group, tm=512, te=1024):
    n, d = x.shape
    ne = u_bf16.shape[0]
    nc = st.shape[0]
    assert te == N_KEYS * SUBLANES
    return pl.pallas_call(
        functools.partial(_peer_dense_kernel, gate_row=5),
        grid=(n // tm, ne // te),
        in_specs=[pl.BlockSpec((tm, d), lambda i, g: (i, 0)),
                  pl.BlockSpec((te, d), lambda i, g: (g, 0)),
                  pl.BlockSpec((d, te), lambda i, g: (0, g)),
                  pl.BlockSpec((nc, N_SLABS, SUBLANES, tm), lambda i, g: (0, 0, 0, i)),
                  pl.BlockSpec((PEER_HEADS, tm), lambda i, g: (0, i)),
                  pl.BlockSpec((PEER_HEADS, tm), lambda i, g: (0, i)),
                  pl.BlockSpec((PEER_HEADS, tm), lambda i, g: (0, i)),
                  pl.BlockSpec((tm, d), lambda i, g: (i, 0)),
                  pl.BlockSpec((1, MOD_ROWS, d), lambda i, g: ((i * tm) // tokens_per_group, 0, 0)),
                  pl.BlockSpec((1, d), lambda i, g: (0, 0)),
                  pl.BlockSpec((1, d), lambda i, g: (0, 0))],
        out_specs=pl.BlockSpec((tm, d), lambda i, g: (i, 0)),
        out_shape=jax.ShapeDtypeStruct((n, d), F32),
        scratch_shapes=[pltpu.VMEM((d, tm), F32),
                        pltpu.VMEM((te, tm), F32),
                        pltpu.VMEM((te, tm), BF16),
                        pltpu.VMEM((PEER_HEADS, N_SLABS, SUBLANES, tm), F32),
                        pltpu.VMEM((PEER_HEADS * SUBLANES, SUBLANES, tm), F32),
                        pltpu.VMEM((PEER_HEADS * SUBLANES, SUBLANES, tm), F32)],
        compiler_params=_cparams(("parallel", "arbitrary")),
        name="peer_dense",
    )(hm, u_bf16, vt_bf16, st, thr, lse, m2, x, mod, ln_g, ln_b)


def peer_block(x, mod, wq, sk, u, vt, ln_g, ln_b, tokens_per_group):
    hm, st = peer_scores(x, mod, wq, sk, tokens_per_group)
    st = st.reshape(2 * PEER_HEADS, N_SLABS, SUBLANES, x.shape[0])
    cut, lse, m2 = peer_select(st)
    return peer_dense(hm, u, vt, st, cut, lse, m2, x, mod, ln_g, ln_b, tokens_per_group)


def _rope_slab(t, cos_t, sin_fwd, sin_bwd):
    return t * cos_t + pltpu.roll(t, QK_ROPE // 2, axis=1) * sin_fwd \
        + pltpu.roll(t, HEAD_SLAB - QK_ROPE // 2, axis=1) * sin_bwd


def _mla_inproj_kernel(x_ref, mod_ref, win_ref, qg_ref, kvg_ref, wuq_ref, *rest, use_rope):
    if use_rope:
        cos_ref, sf_ref, sb_ref, q_ref, ckv_ref, slab_ref = rest
    else:
        q_ref, ckv_ref, slab_ref = rest
    m = mod_ref[0]
    h = x_ref[...] * (1.0 + m[1:2]) + m[0:1]
    y = _dot(h.astype(BF16), win_ref[...])
    cq = y[:, :Q_LORA]
    ckv = y[:, Q_LORA:Q_LORA + KV_LORA]
    slab = y[:, Q_LORA + KV_LORA:]
    cqn = cq * lax.rsqrt(jnp.mean(cq * cq, axis=-1, keepdims=True) + RMS_EPS) * qg_ref[...]
    ckv_ref[...] = ckv * lax.rsqrt(jnp.mean(ckv * ckv, axis=-1, keepdims=True) + RMS_EPS) * kvg_ref[...]
    q = _dot(cqn.astype(BF16), wuq_ref[...]) * (MLA_SCALE * LOG2E)
    if use_rope:
        cos_t, sf, sb = cos_ref[...], sf_ref[...], sb_ref[...]
        slab = _rope_slab(slab, cos_t, sf, sb)
        for hh in range(MLA_HEADS):
            sl = slice(hh * HEAD_SLAB, (hh + 1) * HEAD_SLAB)
            q_ref[:, sl] = _rope_slab(q[:, sl], cos_t, sf, sb).astype(BF16)
    else:
        q_ref[...] = q.astype(BF16)
    slab_ref[...] = slab


def mla_inproj(x, mod, win_bf16, q_g, kv_g, wuq_bf16, rope, tokens_per_group, seq, tm=256):
    n, d = x.shape
    use_rope = rope is not None
    in_specs = [pl.BlockSpec((tm, d), lambda i: (i, 0)),
                pl.BlockSpec((1, MOD_ROWS, d), lambda i: ((i * tm) // tokens_per_group, 0, 0)),
                pl.BlockSpec(win_bf16.shape, lambda i: (0, 0)),
                pl.BlockSpec((1, Q_LORA), lambda i: (0, 0)),
                pl.BlockSpec((1, KV_LORA), lambda i: (0, 0)),
                pl.BlockSpec(wuq_bf16.shape, lambda i: (0, 0))]
    args = [x, mod, win_bf16, q_g, kv_g, wuq_bf16]
    if use_rope:
        tps = seq // tm
        in_specs += [pl.BlockSpec((tm, HEAD_SLAB), lambda i: (i % tps, 0))] * 3
        args += list(rope)
    return pl.pallas_call(
        functools.partial(_mla_inproj_kernel, use_rope=use_rope),
        grid=(n // tm,),
        in_specs=in_specs,
        out_specs=[pl.BlockSpec((tm, MLA_HEADS * HEAD_SLAB), lambda i: (i, 0)),
                   pl.BlockSpec((tm, KV_LORA), lambda i: (i, 0)),
                   pl.BlockSpec((tm, HEAD_SLAB), lambda i: (i, 0))],
        out_shape=[jax.ShapeDtypeStruct((n, MLA_HEADS * HEAD_SLAB), BF16),
                   jax.ShapeDtypeStruct((n, KV_LORA), F32),
                   jax.ShapeDtypeStruct((n, HEAD_SLAB), F32)],
        compiler_params=_cparams(("parallel",)),
        name="mla_inproj",
    )(*args)


def _mla_kv_kernel(ckv_ref, slab_ref, wk_ref, wv_ref, k_ref, v_ref):
    c = ckv_ref[...].astype(BF16)
    kn = _dot(c, wk_ref[...])
    slab = slab_ref[...]
    for hh in range(MLA_HEADS):
        sl = slice(hh * HEAD_SLAB, (hh + 1) * HEAD_SLAB)
        k_ref[:, sl] = (kn[:, sl] + slab).astype(BF16)
    v = _dot(c, wv_ref[...])
    lane = lax.broadcasted_iota(jnp.int32, (v.shape[0], LANES), 1)
    for p in range(MLA_HEADS // 2):
        pair = v[:, p * LANES:(p + 1) * LANES]
        v_ref[:, (2 * p) * LANES:(2 * p + 1) * LANES] = jnp.where(lane < V_HEAD, pair, 1.0).astype(BF16)
        v_ref[:, (2 * p + 1) * LANES:(2 * p + 2) * LANES] = jnp.where(lane < V_HEAD, 1.0, pair).astype(BF16)


def mla_kv(ckv, slab, wk_bf16, wv_bf16, tm=256):
    n = ckv.shape[0]
    tm = min(tm, n)
    return pl.pallas_call(
        _mla_kv_kernel,
        grid=(n // tm,),
        in_specs=[pl.BlockSpec((tm, KV_LORA), lambda i: (i, 0)),
                  pl.BlockSpec((tm, HEAD_SLAB), lambda i: (i, 0)),
                  pl.BlockSpec(wk_bf16.shape, lambda i: (0, 0)),
                  pl.BlockSpec(wv_bf16.shape, lambda i: (0, 0))],
        out_specs=[pl.BlockSpec((tm, MLA_HEADS * HEAD_SLAB), lambda i: (i, 0)),
                   pl.BlockSpec((tm, MLA_HEADS * LANES), lambda i: (i, 0))],
        out_shape=[jax.ShapeDtypeStruct((n, MLA_HEADS * HEAD_SLAB), BF16),
                   jax.ShapeDtypeStruct((n, MLA_HEADS * LANES), BF16)],
        compiler_params=_cparams(("parallel",)),
        name="mla_kv",
    )(ckv, slab, wk_bf16, wv_bf16)


MLA_KEY_CHUNK = 256


def _mla_attn_kernel(q_ref, *rest, n_sources):
    k_refs = rest[:n_sources]
    v_refs = rest[n_sources:2 * n_sources]
    o_ref = rest[2 * n_sources]
    tq = q_ref.shape[0]
    lane = lax.broadcasted_iota(jnp.int32, (tq, LANES), 1)
    sls = [slice(j * HEAD_SLAB, (j + 1) * HEAD_SLAB) for j in range(2)]
    qs = [q_ref[:, sl] for sl in sls]
    chunks = [(si, st) for si, k in enumerate(k_refs) for st in range(0, k.shape[0], MLA_KEY_CHUNK)]

    def score(j, ch):
        si, st = ch
        return _nt_dot(qs[j], k_refs[si][st:st + MLA_KEY_CHUNK, sls[j]])

    def row_max(ss):
        return jnp.max(functools.reduce(jnp.maximum, ss), axis=-1, keepdims=True)

    def weighted(j, ch, p):
        si, st = ch
        return _dot(p, v_refs[si][st:st + MLA_KEY_CHUNK, sls[j]])

    s0 = [score(0, ch) for ch in chunks]
    m0 = row_max(s0)
    s1, p0 = [], []
    for c, ch in enumerate(chunks):
        s1.append(score(1, ch))
        p0.append(jnp.exp2(s0[c] - m0).astype(BF16))
    m1 = row_max(s1)
    o0, p1 = None, []
    for c, ch in enumerate(chunks):
        t = weighted(0, ch, p0[c])
        o0 = t if o0 is None else o0 + t
        p1.append(jnp.exp2(s1[c] - m1).astype(BF16))
    o1 = functools.reduce(jnp.add, [weighted(1, ch, p1[c]) for c, ch in enumerate(chunks)])
    outs = [o / pltpu.roll(o, V_HEAD, axis=1) for o in (o0, o1)]
    o_ref[...] = jnp.where(lane < V_HEAD, outs[0], outs[1]).astype(BF16)


def mla_attention(q, k_new, v_new, k_ctx, v_ctx, n_batch, tq=256):
    n = q.shape[0]
    seq = n // n_batch
    tq = min(tq, seq)
    qt = seq // tq
    pairs = MLA_HEADS // 2
    srcs_k, srcs_v = [k_new], [v_new]
    lens = [seq]
    if k_ctx is not None:
        srcs_k.append(k_ctx)
        srcs_v.append(v_ctx)
        lens.append(k_ctx.shape[0] // n_batch)
    in_specs = [pl.BlockSpec((tq, 2 * HEAD_SLAB), lambda b, p, t: (b * qt + t, p))]
    in_specs += [pl.BlockSpec((ln, 2 * HEAD_SLAB), lambda b, p, t: (b, p)) for ln in lens]
    in_specs += [pl.BlockSpec((ln, 2 * LANES), lambda b, p, t: (b, p)) for ln in lens]
    return pl.pallas_call(
        functools.partial(_mla_attn_kernel, n_sources=len(lens)),
        grid=(n_batch, pairs, qt),
        in_specs=in_specs,
        out_specs=pl.BlockSpec((tq, 2 * V_HEAD), lambda b, p, t: (b * qt + t, p)),
        out_shape=jax.ShapeDtypeStruct((n, MLA_HEADS * V_HEAD), BF16),
        compiler_params=_cparams(("parallel", "parallel", "arbitrary")),
        name="mla_attention",
    )(q, *srcs_k, *srcs_v)


def rope_lane_tables(n):
    t = jnp.arange(n)
    row = (t // GRID_W).astype(F32)
    col = (t % GRID_W).astype(F32)
    inv = ROPE_THETA ** (-jnp.arange(ROPE_PAIRS_AXIS, dtype=F32) / ROPE_PAIRS_AXIS)
    ang = jnp.concatenate([row[:, None] * inv, col[:, None] * inv], axis=-1)
    cos, sin = jnp.cos(ang), jnp.sin(ang)
    ones = jnp.ones((n, QK_NOPE), F32)
    zeros = jnp.zeros((n, QK_NOPE), F32)
    tail1 = jnp.ones((n, HEAD_SLAB - QK_NOPE - QK_ROPE), F32)
    tail0 = jnp.zeros((n, HEAD_SLAB - QK_NOPE - QK_ROPE), F32)
    z16 = jnp.zeros_like(sin)
    cos_t = jnp.concatenate([ones, cos, cos, tail1], axis=-1)
    sin_fwd = jnp.concatenate([zeros, z16, sin, tail0], axis=-1)
    sin_bwd = jnp.concatenate([zeros, -sin, z16, tail0], axis=-1)
    return cos_t, sin_fwd, sin_bwd


def _pad_mod(m, groups):
    m = m.reshape(groups, N_MOD, D_MODEL)
    return jnp.pad(m, ((0, 0), (0, MOD_ROWS - N_MOD), (0, 0)))


def _mla_weights(w_in, w_uq, w_ukv):
    pad_tail = HEAD_SLAB - QK_NOPE - QK_ROPE
    slab_cols = jnp.pad(w_in[:, Q_LORA + KV_LORA:], ((0, 0), (QK_NOPE, pad_tail)))
    win = jnp.concatenate([w_in[:, :Q_LORA + KV_LORA], slab_cols], axis=1).astype(BF16)
    wuq = jnp.pad(w_uq.reshape(Q_LORA, MLA_HEADS, QK_NOPE + QK_ROPE), ((0, 0), (0, 0), (0, pad_tail)))
    wuq = wuq.reshape(Q_LORA, MLA_HEADS * HEAD_SLAB).astype(BF16)
    wkv = w_ukv.reshape(KV_LORA, MLA_HEADS, QK_NOPE + V_HEAD)
    wk = jnp.pad(wkv[:, :, :QK_NOPE], ((0, 0), (0, 0), (0, HEAD_SLAB - QK_NOPE)))
    wk = wk.reshape(KV_LORA, MLA_HEADS * HEAD_SLAB).astype(BF16)
    wv = wkv[:, :, QK_NOPE:].reshape(KV_LORA, MLA_HEADS * V_HEAD).astype(BF16)
    return win, wuq, wk, wv


def kernel(x_prompt, x_sample, cache_na_k, cache_na_v, cache_mla_ckv, cache_mla_krope, c, c_ctx, w_ada, b_ada, ln_mix_g, ln_mix_b, ln_ffn_g, ln_ffn_b, w_in_ab, conv_w, na_rpb, w_out_ab, w_in_mla, q_norm_g, w_uq, kv_norm_g, w_ukv, w_out_mla, peer_w_q, peer_subkeys, peer_u, peer_v):
    nb_p, seq_p, d = x_prompt.shape
    nb_s, seq_s, _ = x_sample.shape
    past = cache_na_k.shape[2]
    rows = seq_s // GRID_W
    n_p = nb_p * seq_p
    xp = x_prompt.reshape(n_p, d)
    xs = x_sample.reshape(nb_s * seq_s, d)

    n_cond = 1 + nb_s
    cond_rows = -(-n_cond // 8) * 8
    cvec = jnp.concatenate([c_ctx[None], c, jnp.zeros((cond_rows - n_cond, d), F32)], axis=0)

    outs = {}
    for i in range(DEPTH):
        j = i // 2
        mods = ada_modulation(cvec, w_ada[i], b_ada[i])
        mod_p = _pad_mod(mods[0:1], 1)
        mod_s = _pad_mod(mods[1:n_cond], nb_s)
        lg = ln_mix_g[i].reshape(1, d)
        lb = ln_mix_b[i].reshape(1, d)
        if i % 2 == 0:
            w_in = w_in_ab[j].astype(BF16)
            w_out = w_out_ab[j].astype(BF16)
            cw = jnp.pad(conv_w[j], ((0, 8 - CONV_TAPS), (0, 0)))
            qkv_p, gu_p, kv_p = inproj_ab(xp, mod_p, w_in, n_p, True)
            qkv_s, gu_s = inproj_ab(xs, mod_s, w_in, seq_s, False)
            outs["na_k"] = kv_p[:, :NA_WIDTH].reshape(nb_p, 1, seq_p, NA_HEADS, NA_HEAD_DIM)
            outs["na_v"] = kv_p[:, NA_WIDTH:].reshape(nb_p, 1, seq_p, NA_HEADS, NA_HEAD_DIM)
            att_p = ctx_attention(qkv_p, seq_p)
            kc = cache_na_k[:, j].reshape(nb_s * past, NA_WIDTH).astype(BF16)
            vc = cache_na_v[:, j].reshape(nb_s * past, NA_WIDTH).astype(BF16)
            att_s = na_attention(qkv_s, kc, vc, na_bias_table(na_rpb[j]), nb_s, rows)
            xp = outproj_ab(att_p, gu_p, xp, mod_p, w_out, cw, lg, lb, n_p, seq_p)
            xs = outproj_ab(att_s, gu_s, xs, mod_s, w_out, cw, lg, lb, seq_s, seq_s)
        else:
            win, wuq, wk, wv = _mla_weights(w_in_mla[j], w_uq[j], w_ukv[j])
            w_out = w_out_mla[j].astype(BF16)
            qg = q_norm_g[j].reshape(1, Q_LORA)
            kvg = kv_norm_g[j].reshape(1, KV_LORA)
            q_p, ckv_p, slab_p = mla_inproj(xp, mod_p, win, qg, kvg, wuq, None, n_p, seq_p)
            q_s, ckv_s, slab_s = mla_inproj(xs, mod_s, win, qg, kvg, wuq, rope_lane_tables(seq_s), seq_s, seq_s)
            outs["ckv"] = ckv_p.reshape(nb_p, 1, seq_p, KV_LORA)
            outs["kr"] = slab_p[:, QK_NOPE:QK_NOPE + QK_ROPE].reshape(nb_p, 1, seq_p, QK_ROPE)
            k_p, v_p = mla_kv(ckv_p, slab_p, wk, wv)
            k_s, v_s = mla_kv(ckv_s, slab_s, wk, wv)
            ckv_c = cache_mla_ckv[:, j].reshape(nb_s * past, KV_LORA)
            slab_c = jnp.pad(cache_mla_krope[:, j].reshape(nb_s * past, QK_ROPE),
                             ((0, 0), (QK_NOPE, HEAD_SLAB - QK_NOPE - QK_ROPE)))
            k_c, v_c = mla_kv(ckv_c, slab_c, wk, wv)
            att_p = mla_attention(q_p, k_p, v_p, None, None, nb_p)
            att_s = mla_attention(q_s, k_s, v_s, k_c, v_c, nb_s)
            xp = outproj(att_p, xp, mod_p, w_out, lg, lb, n_p)
            xs = outproj(att_s, xs, mod_s, w_out, lg, lb, seq_s)
        wq = peer_w_q[i].astype(BF16)
        sk = peer_subkeys[i].reshape(2 * PEER_HEADS, N_KEYS, PEER_HALF).astype(BF16)
        u = peer_u[i].astype(BF16)
        vt = peer_v[i].T.astype(BF16)
        fg = ln_ffn_g[i].reshape(1, d)
        fb = ln_ffn_b[i].reshape(1, d)
        xp = peer_block(xp, mod_p, wq, sk, u, vt, fg, fb, n_p)
        xs = peer_block(xs, mod_s, wq, sk, u, vt, fg, fb, seq_s)
    return (xp.reshape(nb_p, seq_p, d), xs.reshape(nb_s, seq_s, d),
            outs["na_k"], outs["na_v"], outs["ckv"], outs["kr"])
```

```python
import functools

import jax
import jax.numpy as jnp
import numpy as np
from jax import lax
from jax.experimental import pallas as pl
from jax.experimental.pallas import tpu as pltpu

F32 = jnp.float32
BF16 = jnp.bfloat16

D_MODEL = 1024
DEPTH = 2
GRID_W = 64
N_MOD = 6
ALPHA = (2.0 * DEPTH) ** 0.25
LN_EPS = 1e-5
RMS_EPS = 1e-6
NA_HEADS = 8
NA_HEAD_DIM = 64
NA_WIDTH = NA_HEADS * NA_HEAD_DIM
NA_MAX_ROWS = 8
NA_WIN_COLS = 16
NA_SCALE = NA_HEAD_DIM ** -0.5
CONV_WIDTH = 512
CONV_TAPS = 3
MLA_HEADS = 16
Q_LORA = 256
KV_LORA = 256
QK_NOPE = 64
QK_ROPE = 32
V_HEAD = 64
MLA_SCALE = (QK_NOPE + QK_ROPE) ** -0.5
ROPE_THETA = 10000.0
ROPE_PAIRS_AXIS = QK_ROPE // 4
PEER_HEADS = 8
PEER_HALF = 128
N_KEYS = 128
N_EXPERTS = N_KEYS * N_KEYS
PEER_TOPK = 16

LANES = 128
MOD_ROWS = 8
HEAD_SLAB = 128
MASK_BIAS = -1e30
VMEM_LIMIT = 52 * 1024 * 1024


def _cparams(sem, flags=None):
    return pltpu.CompilerParams(dimension_semantics=sem, vmem_limit_bytes=VMEM_LIMIT, flags=flags)


def _nt_dot(a, b):
    return lax.dot_general(a, b, (((1,), (1,)), ((), ())), preferred_element_type=F32)


def _dot(a, b):
    return jnp.dot(a, b, preferred_element_type=F32)


def _split_bf16(a):
    hi = a.astype(BF16)
    lo = (a - hi.astype(F32)).astype(BF16)
    return hi, lo


def _layer_norm(z, g, b):
    mu = jnp.mean(z, axis=-1, keepdims=True)
    zc = z - mu
    var = jnp.mean(zc * zc, axis=-1, keepdims=True)
    return zc * lax.rsqrt(var + LN_EPS) * g + b


LOG2E = 1.4426950408889634
LN2 = 0.6931471805599453
_GELU_A = -2.0 * LOG2E * 0.7978845608028654
_GELU_B = _GELU_A * 0.044715


def _gelu_tanh(x):
    return x / (1.0 + jnp.exp2(x * (_GELU_A + _GELU_B * (x * x))))


def _ada_kernel(c_ref, w_ref, b_ref, o_ref):
    c = c_ref[...]
    a = c * (1.0 / (1.0 + jnp.exp(-c)))
    a_hi, a_lo = _split_bf16(a)
    w_hi, w_lo = _split_bf16(w_ref[...])
    o_ref[...] = _dot(a_hi, w_hi) + _dot(a_hi, w_lo) + _dot(a_lo, w_hi) + b_ref[...]


def ada_modulation(cvec, w_ada, b_ada, tn=512):
    r, d = cvec.shape
    n = w_ada.shape[1]
    return pl.pallas_call(
        _ada_kernel,
        grid=(n // tn,),
        in_specs=[pl.BlockSpec((r, d), lambda j: (0, 0)),
                  pl.BlockSpec((d, tn), lambda j: (0, j)),
                  pl.BlockSpec((1, tn), lambda j: (0, j))],
        out_specs=pl.BlockSpec((r, tn), lambda j: (0, j)),
        out_shape=jax.ShapeDtypeStruct((r, n), F32),
        compiler_params=_cparams(("arbitrary",)),
        name="ada_modulation",
    )(cvec, w_ada, b_ada.reshape(1, n))


def _inproj_ab_kernel(x_ref, mod_ref, w_ref, qkv_ref, gu_ref, *kv_ref):
    m = mod_ref[0]
    h = x_ref[...] * (1.0 + m[1:2]) + m[0:1]
    y = _dot(h.astype(BF16), w_ref[...])
    w3 = 3 * NA_WIDTH
    qkv_ref[:, :NA_WIDTH] = (y[:, :NA_WIDTH] * (NA_SCALE * LOG2E)).astype(BF16)
    qkv_ref[:, NA_WIDTH:] = y[:, NA_WIDTH:w3].astype(BF16)
    if kv_ref:
        kv_ref[0][...] = y[:, NA_WIDTH:w3]
    gu_ref[:, :CONV_WIDTH] = y[:, w3:w3 + CONV_WIDTH].astype(BF16)
    gu_ref[:, CONV_WIDTH:] = (y[:, w3 + CONV_WIDTH:w3 + 2 * CONV_WIDTH]
                              * y[:, w3 + 2 * CONV_WIDTH:]).astype(BF16)


def inproj_ab(x, mod, w_bf16, tokens_per_group, emit_kv, tm=512):
    n, d = x.shape
    tm = min(tm, n)
    out_shape = [jax.ShapeDtypeStruct((n, 3 * NA_WIDTH), BF16),
                 jax.ShapeDtypeStruct((n, 2 * CONV_WIDTH), BF16)]
    out_specs = [pl.BlockSpec((tm, 3 * NA_WIDTH), lambda i: (i, 0)),
                 pl.BlockSpec((tm, 2 * CONV_WIDTH), lambda i: (i, 0))]
    if emit_kv:
        out_shape.append(jax.ShapeDtypeStruct((n, 2 * NA_WIDTH), F32))
        out_specs.append(pl.BlockSpec((tm, 2 * NA_WIDTH), lambda i: (i, 0)))
    return pl.pallas_call(
        _inproj_ab_kernel,
        grid=(n // tm,),
        in_specs=[pl.BlockSpec((tm, d), lambda i: (i, 0)),
                  pl.BlockSpec((1, MOD_ROWS, d), lambda i: ((i * tm) // tokens_per_group, 0, 0)),
                  pl.BlockSpec(w_bf16.shape, lambda i: (0, 0))],
        out_specs=out_specs,
        out_shape=out_shape,
        compiler_params=_cparams(("parallel",)),
        name="inproj_ab",
    )(x, mod, w_bf16)


def _pair_attention(instances):
    heads = []
    for n, (q_pair, _, _, _) in enumerate(instances):
        lane = lax.broadcasted_iota(jnp.int32, q_pair.shape, 1)
        qf = q_pair.astype(F32)
        for j in range(2):
            in_head = (lane >= j * NA_HEAD_DIM) & (lane < (j + 1) * NA_HEAD_DIM)
            heads.append((n, j, jnp.where(in_head, qf, 0.0).astype(BF16)))
    scores = []
    for n, j, qm in heads:
        _, keys, _, biases = instances[n]
        ss = [_nt_dot(qm, k) for k in keys]
        scores.append([s if b is None else s + b for s, b in zip(ss, biases[j])])
    mxs = [functools.reduce(jnp.maximum, [jnp.max(s, axis=-1, keepdims=True) for s in ss]) for ss in scores]
    ess = [[jnp.exp2(s - mx) for s in ss] for ss, mx in zip(scores, mxs)]
    dens = [functools.reduce(jnp.add, [jnp.sum(e, axis=-1, keepdims=True) for e in es]) for es in ess]
    outs = []
    for (n, j, _), es in zip(heads, ess):
        values = instances[n][2]
        outs.append(functools.reduce(jnp.add, [_dot(e.astype(BF16), v) for e, v in zip(es, values)]))
    outs = [o / d for o, d in zip(outs, dens)]
    lane = lax.broadcasted_iota(jnp.int32, outs[0].shape, 1)
    return [jnp.where(lane < NA_HEAD_DIM, outs[2 * n], outs[2 * n + 1]) for n in range(len(instances))]


NA_ROWS_PER_STEP = 2


def _na_attn_kernel(q_ref, k_ref, v_ref, kc_ref, vc_ref, *rest, rows):
    bias_refs = rest[:NA_ROWS_PER_STEP]
    o_ref = rest[NA_ROWS_PER_STEP]
    band = NA_MAX_ROWS * GRID_W
    instances, places = [], []
    for rr in range(NA_ROWS_PER_STEP):
        r = pl.program_id(1) * NA_ROWS_PER_STEP + rr
        rs = jnp.clip(r - NA_MAX_ROWS // 2, 0, rows - NA_MAX_ROWS)
        start = pl.multiple_of(rs * GRID_W, GRID_W)
        qrows = slice(rr * GRID_W, (rr + 1) * GRID_W)
        for p in range(NA_HEADS // 2):
            sl = slice(p * LANES, (p + 1) * LANES)
            keys = [k_ref[pl.ds(start, band), sl], kc_ref[:, sl]]
            values = [v_ref[pl.ds(start, band), sl], vc_ref[:, sl]]
            biases = [[bias_refs[rr][0, 2 * p + j], None] for j in range(2)]
            instances.append((q_ref[qrows, sl], keys, values, biases))
            places.append((qrows, sl))
    for (qrows, sl), o in zip(places, _pair_attention(instances)):
        o_ref[qrows, sl] = o.astype(BF16)


def na_attention(qkv, kc, vc, bias, n_batch, rows):
    n = qkv.shape[0]
    seq = rows * GRID_W
    lc = kc.shape[0] // n_batch
    half = NA_MAX_ROWS // 2

    steps = rows // NA_ROWS_PER_STEP
    tq = NA_ROWS_PER_STEP * GRID_W

    def bias_spec(rr):
        def idx(b, g):
            r = g * NA_ROWS_PER_STEP + rr
            return (r - jnp.clip(r - half, 0, rows - NA_MAX_ROWS), 0, 0, 0)
        return pl.BlockSpec((1, NA_HEADS, GRID_W, NA_MAX_ROWS * GRID_W), idx)

    return pl.pallas_call(
        functools.partial(_na_attn_kernel, rows=rows),
        grid=(n_batch, steps),
        in_specs=[pl.BlockSpec((tq, NA_WIDTH), lambda b, g: (b * steps + g, 0)),
                  pl.BlockSpec((seq, NA_WIDTH), lambda b, g: (b, 1)),
                  pl.BlockSpec((seq, NA_WIDTH), lambda b, g: (b, 2)),
                  pl.BlockSpec((lc, NA_WIDTH), lambda b, g: (b, 0)),
                  pl.BlockSpec((lc, NA_WIDTH), lambda b, g: (b, 0))]
        + [bias_spec(rr) for rr in range(NA_ROWS_PER_STEP)],
        out_specs=pl.BlockSpec((tq, NA_WIDTH), lambda b, g: (b * steps + g, 0)),
        out_shape=jax.ShapeDtypeStruct((n, NA_WIDTH), BF16),
        compiler_params=_cparams(("parallel", "arbitrary")),
        name="na_attention",
    )(qkv, qkv, qkv, kc, vc, *([bias] * NA_ROWS_PER_STEP))


def na_bias_table(rpb):
    c = np.arange(GRID_W)[:, None]
    kc = np.arange(GRID_W)[None, :]
    cs = np.clip(c - NA_WIN_COLS // 2, 0, GRID_W - NA_WIN_COLS)
    valid = (kc >= cs) & (kc < cs + NA_WIN_COLS)
    dc = kc - c + (NA_WIN_COLS - 1)
    cc, kk = np.nonzero(valid)
    place = np.zeros((2 * NA_WIN_COLS - 1, GRID_W, GRID_W), np.float32)
    place[dc[cc, kk], cc, kk] = 1.0
    t = jnp.einsum("hrd,dck->hrck", rpb, place, precision=lax.Precision.HIGHEST)
    t = jnp.where(valid[None, None], t * LOG2E, MASK_BIAS)
    bands = [t[:, NA_MAX_ROWS - 1 - off:2 * NA_MAX_ROWS - 1 - off] for off in range(NA_MAX_ROWS)]
    b = jnp.stack(bands, axis=0)
    b = jnp.transpose(b, (0, 1, 3, 2, 4))
    return b.reshape(NA_MAX_ROWS, NA_HEADS, GRID_W, NA_MAX_ROWS * GRID_W).astype(F32)


def _ctx_attn_kernel(q_ref, k_ref, v_ref, o_ref):
    slabs = [slice(p * LANES, (p + 1) * LANES) for p in range(NA_HEADS // 2)]
    instances = [(q_ref[:, sl], [k_ref[:, sl]], [v_ref[:, sl]], [[None], [None]]) for sl in slabs]
    for sl, o in zip(slabs, _pair_attention(instances)):
        o_ref[:, sl] = o.astype(BF16)


def ctx_attention(qkv, seq):
    n = qkv.shape[0]
    return pl.pallas_call(
        _ctx_attn_kernel,
        grid=(n // seq,),
        in_specs=[pl.BlockSpec((seq, NA_WIDTH), lambda b: (b, 0)),
                  pl.BlockSpec((seq, NA_WIDTH), lambda b: (b, 1)),
                  pl.BlockSpec((seq, NA_WIDTH), lambda b: (b, 2))],
        out_specs=pl.BlockSpec((seq, NA_WIDTH), lambda b: (b, 0)),
        out_shape=jax.ShapeDtypeStruct((n, NA_WIDTH), BF16),
        compiler_params=_cparams(("parallel",)),
        name="ctx_attention",
    )(qkv, qkv, qkv)


HALO = 16


def _outproj_ab_kernel(att_ref, gu_ref, prev_ref, next_ref, x_ref, mod_ref, w_ref, cw_ref, g_ref, b_ref,
                       o_ref, *, tiles_per_seq):
    i = pl.program_id(0)
    tm = x_ref.shape[0]
    m = mod_ref[0]
    gu = gu_ref[...].astype(F32)
    bg = gu[:, :CONV_WIDTH]
    u = gu[:, CONV_WIDTH:]
    has_prev = (i % tiles_per_seq != 0).astype(F32)
    has_next = (i % tiles_per_seq != tiles_per_seq - 1).astype(F32)
    prev_row = prev_ref[HALO - 1:HALO, CONV_WIDTH:].astype(F32) * has_prev
    next_row = next_ref[0:1, CONV_WIDTH:].astype(F32) * has_next
    row = lax.broadcasted_iota(jnp.int32, u.shape, 0)
    u_m1 = jnp.where(row == 0, prev_row, pltpu.roll(u, 1, axis=0))
    u_p1 = jnp.where(row == tm - 1, next_row, pltpu.roll(u, tm - 1, axis=0))
    cw = cw_ref[...]
    conv = bg * (u_m1 * cw[0:1] + u * cw[1:2] + u_p1 * cw[2:3])
    y = _dot(att_ref[...], w_ref[:NA_WIDTH, :]) + _dot(conv.astype(BF16), w_ref[NA_WIDTH:, :])
    z = ALPHA * x_ref[...] + m[2:3] * y
    o_ref[...] = _layer_norm(z, g_ref[...], b_ref[...])


def outproj_ab(att, gu, x, mod, w_bf16, conv_w8, ln_g, ln_b, tokens_per_group, seq, tm=256):
    n, d = x.shape
    tm = min(tm, seq)
    hb = tm // HALO
    last_halo = n // HALO - 1
    return pl.pallas_call(
        functools.partial(_outproj_ab_kernel, tiles_per_seq=seq // tm),
        grid=(n // tm,),
        in_specs=[pl.BlockSpec((tm, NA_WIDTH), lambda i: (i, 0)),
                  pl.BlockSpec((tm, 2 * CONV_WIDTH), lambda i: (i, 0)),
                  pl.BlockSpec((HALO, 2 * CONV_WIDTH), lambda i: (jnp.maximum(i * hb - 1, 0), 0)),
                  pl.BlockSpec((HALO, 2 * CONV_WIDTH), lambda i: (jnp.minimum((i + 1) * hb, last_halo), 0)),
                  pl.BlockSpec((tm, d), lambda i: (i, 0)),
                  pl.BlockSpec((1, MOD_ROWS, d), lambda i: ((i * tm) // tokens_per_group, 0, 0)),
                  pl.BlockSpec(w_bf16.shape, lambda i: (0, 0)),
                  pl.BlockSpec(conv_w8.shape, lambda i: (0, 0)),
                  pl.BlockSpec((1, d), lambda i: (0, 0)),
                  pl.BlockSpec((1, d), lambda i: (0, 0))],
        out_specs=pl.BlockSpec((tm, d), lambda i: (i, 0)),
        out_shape=jax.ShapeDtypeStruct((n, d), F32),
        compiler_params=_cparams(("parallel",)),
        name="outproj_ab",
    )(att, gu, gu, gu, x, mod, w_bf16, conv_w8, ln_g, ln_b)


def _outproj_kernel(att_ref, x_ref, mod_ref, w_ref, g_ref, b_ref, o_ref):
    m = mod_ref[0]
    y = _dot(att_ref[...], w_ref[...])
    z = ALPHA * x_ref[...] + m[2:3] * y
    o_ref[...] = _layer_norm(z, g_ref[...], b_ref[...])


def outproj(att, x, mod, w_bf16, ln_g, ln_b, tokens_per_group, tm=512):
    n, d = x.shape
    k = att.shape[1]
    return pl.pallas_call(
        _outproj_kernel,
        grid=(n // tm,),
        in_specs=[pl.BlockSpec((tm, k), lambda i: (i, 0)),
                  pl.BlockSpec((tm, d), lambda i: (i, 0)),
                  pl.BlockSpec((1, MOD_ROWS, d), lambda i: ((i * tm) // tokens_per_group, 0, 0)),
                  pl.BlockSpec(w_bf16.shape, lambda i: (0, 0)),
                  pl.BlockSpec((1, d), lambda i: (0, 0)),
                  pl.BlockSpec((1, d), lambda i: (0, 0))],
        out_specs=pl.BlockSpec((tm, d), lambda i: (i, 0)),
        out_shape=jax.ShapeDtypeStruct((n, d), F32),
        compiler_params=_cparams(("parallel",)),
        name="outproj",
    )(att, x, mod, w_bf16, ln_g, ln_b)


def _peer_scores_kernel(x_ref, mod_ref, wq_ref, sk_ref, hm_ref, st_ref):
    m = mod_ref[0]
    hb = (x_ref[...] * (1.0 + m[4:5]) + m[3:4]).astype(BF16)
    hm_ref[...] = hb
    q = _dot(hb, wq_ref[...])
    for c in range(2 * PEER_HEADS):
        qc = q[:, c * PEER_HALF:(c + 1) * PEER_HALF].astype(BF16)
        st_ref[c] = _nt_dot(sk_ref[c], qc) * LOG2E


def peer_scores(x, mod, wq_bf16, subkeys_bf16, tokens_per_group, tm=512):
    n, d = x.shape
    nc = 2 * PEER_HEADS
    return pl.pallas_call(
        _peer_scores_kernel,
        grid=(n // tm,),
        in_specs=[pl.BlockSpec((tm, d), lambda i: (i, 0)),
                  pl.BlockSpec((1, MOD_ROWS, d), lambda i: ((i * tm) // tokens_per_group, 0, 0)),
                  pl.BlockSpec(wq_bf16.shape, lambda i: (0, 0)),
                  pl.BlockSpec(subkeys_bf16.shape, lambda i: (0, 0, 0))],
        out_specs=[pl.BlockSpec((tm, d), lambda i: (i, 0)),
                   pl.BlockSpec((nc, N_KEYS, tm), lambda i: (0, 0, i))],
        out_shape=[jax.ShapeDtypeStruct((n, d), BF16),
                   jax.ShapeDtypeStruct((nc, N_KEYS, n), F32)],
        compiler_params=_cparams(("parallel",)),
        name="peer_scores",
    )(x, mod, wq_bf16, subkeys_bf16)


SUBLANES = 8
N_SLABS = N_KEYS // SUBLANES


def _sort_network(n):
    pairs = []
    p = 1
    while p < n:
        k = p
        while k >= 1:
            for j in range(k % p, n - k, 2 * k):
                for i in range(min(k, n - j - k)):
                    if (i + j) // (2 * p) == (i + j + k) // (2 * p):
                        pairs.append((i + j, i + j + k))
            k //= 2
        p *= 2
    return pairs


_SLAB_SORT = _sort_network(N_SLABS)


def _top_values(s_ref, c, n_top):
    x = [s_ref[c, r] for r in range(N_SLABS)]
    for a, b in _SLAB_SORT:
        x[a], x[b] = jnp.maximum(x[a], x[b]), jnp.minimum(x[a], x[b])
    sub = lax.broadcasted_iota(jnp.int32, x[0].shape, 0).astype(F32)
    neg = jnp.full(x[0].shape, -jnp.inf, F32)
    x.append(neg)
    vals = []
    for k in range(n_top):
        mx = jnp.max(x[0], axis=0, keepdims=True)
        vals.append(mx)
        if k == n_top - 1:
            break
        first = jnp.min(jnp.where(x[0] == mx, sub, float(SUBLANES)), axis=0, keepdims=True)
        pop = sub == first
        for r in range(n_top - 1 - k):
            x[r] = jnp.where(pop, x[r + 1], x[r])
    return vals


def _rows_to_slab(rows_list):
    t = rows_list[0].shape[1]
    ridx = lax.broadcasted_iota(jnp.int32, (SUBLANES, t), 0)
    out = jnp.full((SUBLANES, t), -jnp.inf, F32)
    for r, row in enumerate(rows_list):
        out = jnp.where(ridx == r, row, out)
    return out


def _peer_select_kernel(st_ref, thr_ref, lse_ref, m2_ref):
    k = PEER_TOPK
    for h in range(PEER_HEADS):
        v1 = _top_values(st_ref, 2 * h, k + 1)
        v2 = _top_values(st_ref, 2 * h + 1, k + 1)
        v1_lo, v1_hi = _rows_to_slab(v1[0:8]), _rows_to_slab(v1[8:16])
        v2_lo, v2_hi = _rows_to_slab(v2[0:8]), _rows_to_slab(v2[8:16])
        ridx = lax.broadcasted_iota(jnp.int32, v1_lo.shape, 0)
        v1_mid = jnp.where(ridx >= 4, v1_lo, -jnp.inf)
        cands = [v1[0] + v2_lo, v1[0] + v2_hi, v1[1] + v2_lo, v1[2] + v2_lo, v1[3] + v2_lo,
                 v1_hi + v2[0], v1_mid + v2[0], v1_mid + v2[1], v1_mid + v2[2],
                 _rows_to_slab([v1[k] + v2[0], v1[0] + v2[k]])]
        top = v1[0] + v2[0]
        work = list(cands)
        thr = jnp.full_like(top, -jnp.inf)
        nxt = jnp.full_like(top, -jnp.inf)
        seen = jnp.zeros_like(top)
        for _ in range(k + 1):
            mx = jnp.max(functools.reduce(jnp.maximum, work), axis=0, keepdims=True)
            eqs = [w == mx for w in work]
            cnt = functools.reduce(jnp.add, [jnp.where(e, 1.0, 0.0) for e in eqs])
            thr = jnp.where(seen < k, mx, thr)
            nxt = jnp.where(seen < k + 1, mx, nxt)
            seen = seen + jnp.sum(cnt, axis=0, keepdims=True)
            work = [jnp.where(e, -jnp.inf, w) for e, w in zip(eqs, work)]
        z = functools.reduce(jnp.add, [jnp.where(cd >= thr, jnp.exp2(cd - top), 0.0) for cd in cands])
        z = jnp.sum(z, axis=0, keepdims=True)
        thr_ref[h:h + 1, :] = 0.5 * (thr + nxt)
        lse_ref[h:h + 1, :] = v1[0] + jnp.log(z) * LOG2E
        m2_ref[h:h + 1, :] = v2[0]


def peer_select(st, tl=256):
    nc, _, _, n = st.shape
    stat = pl.BlockSpec((PEER_HEADS, tl), lambda i: (0, i))
    return pl.pallas_call(
        _peer_select_kernel,
        grid=(n // tl,),
        in_specs=[pl.BlockSpec((nc, N_SLABS, SUBLANES, tl), lambda i: (0, 0, 0, i))],
        out_specs=[stat, stat, stat],
        out_shape=[jax.ShapeDtypeStruct((PEER_HEADS, n), F32)] * 3,
        compiler_params=_cparams(("parallel",)),
        name="peer_select",
    )(st)


PEER_LANE_CHUNK = 128
PEER_MXU_CHUNK = 256


def _peer_dense_kernel(hm_ref, u_ref, vt_ref, st_ref, thr_ref, lse_ref, m2_ref, x_ref, mod_ref, g_ref, b_ref,
                       o_ref, acc_ref, ht_ref, at_ref, e2_ref, cut_rep, f1_rep, *, gate_row):
    e = pl.program_id(1)
    tm = hm_ref.shape[0]

    @pl.when(e == 0)
    def _():
        acc_ref[...] = jnp.zeros_like(acc_ref)
        for h in range(PEER_HEADS):
            e2_ref[h] = jnp.exp2(st_ref[2 * h + 1] - m2_ref[h:h + 1, :])

    for c in range(tm // PEER_MXU_CHUNK):
        ls = slice(c * PEER_MXU_CHUNK, (c + 1) * PEER_MXU_CHUNK)
        ht_ref[:, ls] = _nt_dot(u_ref[...], hm_ref[ls, :])

    for h in range(PEER_HEADS):
        s1 = st_ref[2 * h, e]
        cut_rows = thr_ref[h:h + 1, :] - s1
        f1_rows = jnp.exp2(s1 - lse_ref[h:h + 1, :])
        for ii in range(SUBLANES):
            cut_rep[h * SUBLANES + ii] = jnp.broadcast_to(cut_rows[ii:ii + 1, :], (SUBLANES, tm))
            f1_rep[h * SUBLANES + ii] = jnp.broadcast_to(f1_rows[ii:ii + 1, :], (SUBLANES, tm))
    lanes_per_mxu = PEER_MXU_CHUNK // PEER_LANE_CHUNK
    for cm in range(tm // PEER_MXU_CHUNK):
        for ii in range(SUBLANES):
            rows = slice(ii * N_KEYS, (ii + 1) * N_KEYS)
            for c in range(cm * lanes_per_mxu, (cm + 1) * lanes_per_mxu):
                ls = slice(c * PEER_LANE_CHUNK, (c + 1) * PEER_LANE_CHUNK)
                w = jnp.zeros((N_SLABS, SUBLANES, PEER_LANE_CHUNK), F32)
                for h in range(PEER_HEADS):
                    sel = st_ref[2 * h + 1, :, :, ls] >= cut_rep[h * SUBLANES + ii, :, ls][None]
                    w = w + jnp.where(sel, e2_ref[h, :, :, ls] * f1_rep[h * SUBLANES + ii, :, ls][None], 0.0)
                act = _gelu_tanh(ht_ref[rows, ls].astype(BF16))
                at_ref[rows, ls] = act * w.reshape(N_KEYS, PEER_LANE_CHUNK).astype(BF16)
        ms = slice(cm * PEER_MXU_CHUNK, (cm + 1) * PEER_MXU_CHUNK)
        acc_ref[:, ms] += _dot(vt_ref[...], at_ref[:, ms])

    @pl.when(e == pl.num_programs(1) - 1)
    def _():
        m = mod_ref[0]
        z = ALPHA * x_ref[...] + m[gate_row:gate_row + 1] * acc_ref[...].T
        o_ref[...] = _layer_norm(z, g_ref[...], b_ref[...])


def peer_dense(hm, u_bf16, vt_bf16, st, thr, lse, m2, x, mod, ln_g, ln_b, tokens_per_group, tm=512, te=1024):
    n, d = x.shape
    ne = u_bf16.shape[0]
    nc = st.shape[0]
    assert te == N_KEYS * SUBLANES
    return pl.pallas_call(
        functools.partial(_peer_dense_kernel, gate_row=5),
        grid=(n // tm, ne // te),
        in_specs=[pl.BlockSpec((tm, d), lambda i, g: (i, 0)),
                  pl.BlockSpec((te, d), lambda i, g: (g, 0)),
                  pl.BlockSpec((d, te), lambda i, g: (0, g)),
                  pl.BlockSpec((nc, N_SLABS, SUBLANES, tm), lambda i, g: (0, 0, 0, i)),
                  pl.BlockSpec((PEER_HEADS, tm), lambda i, g: (0, i)),
                  pl.BlockSpec((PEER_HEADS, tm), lambda i, g: (0, i)),
                  pl.BlockSpec((PEER_HEADS, tm), lambda i, g: (0, i)),
                  pl.BlockSpec((tm, d), lambda i, g: (i, 0)),
                  pl.BlockSpec((1, MOD_ROWS, d), lambda i, g: ((i * tm) // tokens_per_group, 0, 0)),
                  pl.BlockSpec((1, d), lambda i, g: (0, 0)),
                  pl.BlockSpec((1, d), lambda i, g: (0, 0))],
        out_specs=pl.BlockSpec((tm, d), lambda i, g: (i, 0)),
        out_shape=jax.ShapeDtypeStruct((n, d), F32),
        scratch_shapes=[pltpu.VMEM((d, tm), F32),
                        pltpu.VMEM((te, tm), F32),
                        pltpu.VMEM((te, tm), BF16),
                        pltpu.VMEM((PEER_HEADS, N_SLABS, SUBLANES, tm), F32),
                        pltpu.VMEM((PEER_HEADS * SUBLANES, SUBLANES, tm), F32),
                        pltpu.VMEM((PEER_HEADS * SUBLANES, SUBLANES, tm), F32)],
        compiler_params=_cparams(("parallel", "arbitrary")),
        name="peer_dense",
    )(hm, u_bf16, vt_bf16, st, thr, lse, m2, x, mod, ln_g, ln_b)


def peer_block(x, mod, wq, sk, u, vt, ln_g, ln_b, tokens_per_group):
    hm, st = peer_scores(x, mod, wq, sk, tokens_per_group)
    st = st.reshape(2 * PEER_HEADS, N_SLABS, SUBLANES, x.shape[0])
    cut, lse, m2 = peer_select(st)
    return peer_dense(hm, u, vt, st, cut, lse, m2, x, mod, ln_g, ln_b, tokens_per_group)


def _rope_slab(t, cos_t, sin_fwd, sin_bwd):
    return t * cos_t + pltpu.roll(t, QK_ROPE // 2, axis=1) * sin_fwd \
        + pltpu.roll(t, HEAD_SLAB - QK_ROPE // 2, axis=1) * sin_bwd


def _mla_inproj_kernel(x_ref, mod_ref, win_ref, qg_ref, kvg_ref, wuq_ref, *rest, use_rope):
    if use_rope:
        cos_ref, sf_ref, sb_ref, q_ref, ckv_ref, slab_ref = rest
    else:
        q_ref, ckv_ref, slab_ref = rest
    m = mod_ref[0]
    h = x_ref[...] * (1.0 + m[1:2]) + m[0:1]
    y = _dot(h.astype(BF16), win_ref[...])
    cq = y[:, :Q_LORA]
    ckv = y[:, Q_LORA:Q_LORA + KV_LORA]
    slab = y[:, Q_LORA + KV_LORA:]
    cqn = cq * lax.rsqrt(jnp.mean(cq * cq, axis=-1, keepdims=True) + RMS_EPS) * qg_ref[...]
    ckv_ref[...] = ckv * lax.rsqrt(jnp.mean(ckv * ckv, axis=-1, keepdims=True) + RMS_EPS) * kvg_ref[...]
    q = _dot(cqn.astype(BF16), wuq_ref[...]) * (MLA_SCALE * LOG2E)
    if use_rope:
        cos_t, sf, sb = cos_ref[...], sf_ref[...], sb_ref[...]
        slab = _rope_slab(slab, cos_t, sf, sb)
        for hh in range(MLA_HEADS):
            sl = slice(hh * HEAD_SLAB, (hh + 1) * HEAD_SLAB)
            q_ref[:, sl] = _rope_slab(q[:, sl], cos_t, sf, sb).astype(BF16)
    else:
        q_ref[...] = q.astype(BF16)
    slab_ref[...] = slab


def mla_inproj(x, mod, win_bf16, q_g, kv_g, wuq_bf16, rope, tokens_per_group, seq, tm=256):
    n, d = x.shape
    use_rope = rope is not None
    in_specs = [pl.BlockSpec((tm, d), lambda i: (i, 0)),
                pl.BlockSpec((1, MOD_ROWS, d), lambda i: ((i * tm) // tokens_per_group, 0, 0)),
                pl.BlockSpec(win_bf16.shape, lambda i: (0, 0)),
                pl.BlockSpec((1, Q_LORA), lambda i: (0, 0)),
                pl.BlockSpec((1, KV_LORA), lambda i: (0, 0)),
                pl.BlockSpec(wuq_bf16.shape, lambda i: (0, 0))]
    args = [x, mod, win_bf16, q_g, kv_g, wuq_bf16]
    if use_rope:
        tps = seq // tm
        in_specs += [pl.BlockSpec((tm, HEAD_SLAB), lambda i: (i % tps, 0))] * 3
        args += list(rope)
    return pl.pallas_call(
        functools.partial(_mla_inproj_kernel, use_rope=use_rope),
        grid=(n // tm,),
        in_specs=in_specs,
        out_specs=[pl.BlockSpec((tm, MLA_HEADS * HEAD_SLAB), lambda i: (i, 0)),
                   pl.BlockSpec((tm, KV_LORA), lambda i: (i, 0)),
                   pl.BlockSpec((tm, HEAD_SLAB), lambda i: (i, 0))],
        out_shape=[jax.ShapeDtypeStruct((n, MLA_HEADS * HEAD_SLAB), BF16),
                   jax.ShapeDtypeStruct((n, KV_LORA), F32),
                   jax.ShapeDtypeStruct((n, HEAD_SLAB), F32)],
        compiler_params=_cparams(("parallel",)),
        name="mla_inproj",
    )(*args)


def _mla_kv_kernel(ckv_ref, slab_ref, wk_ref, wv_ref, k_ref, v_ref):
    c = ckv_ref[...].astype(BF16)
    kn = _dot(c, wk_ref[...])
    slab = slab_ref[...]
    for hh in range(MLA_HEADS):
        sl = slice(hh * HEAD_SLAB, (hh + 1) * HEAD_SLAB)
        k_ref[:, sl] = (kn[:, sl] + slab).astype(BF16)
    v = _dot(c, wv_ref[...])
    lane = lax.broadcasted_iota(jnp.int32, (v.shape[0], LANES), 1)
    for p in range(MLA_HEADS // 2):
        pair = v[:, p * LANES:(p + 1) * LANES]
        v_ref[:, (2 * p) * LANES:(2 * p + 1) * LANES] = jnp.where(lane < V_HEAD, pair, 1.0).astype(BF16)
        v_ref[:, (2 * p + 1) * LANES:(2 * p + 2) * LANES] = jnp.where(lane < V_HEAD, 1.0, pair).astype(BF16)


def mla_kv(ckv, slab, wk_bf16, wv_bf16, tm=256):
    n = ckv.shape[0]
    tm = min(tm, n)
    return pl.pallas_call(
        _mla_kv_kernel,
        grid=(n // tm,),
        in_specs=[pl.BlockSpec((tm, KV_LORA), lambda i: (i, 0)),
                  pl.BlockSpec((tm, HEAD_SLAB), lambda i: (i, 0)),
                  pl.BlockSpec(wk_bf16.shape, lambda i: (0, 0)),
                  pl.BlockSpec(wv_bf16.shape, lambda i: (0, 0))],
        out_specs=[pl.BlockSpec((tm, MLA_HEADS * HEAD_SLAB), lambda i: (i, 0)),
                   pl.BlockSpec((tm, MLA_HEADS * LANES), lambda i: (i, 0))],
        out_shape=[jax.ShapeDtypeStruct((n, MLA_HEADS * HEAD_SLAB), BF16),
                   jax.ShapeDtypeStruct((n, MLA_HEADS * LANES), BF16)],
        compiler_params=_cparams(("parallel",)),
        name="mla_kv",
    )(ckv, slab, wk_bf16, wv_bf16)


MLA_KEY_CHUNK = 256


def _mla_attn_kernel(q_ref, *rest, n_sources):
    k_refs = rest[:n_sources]
    v_refs = rest[n_sources:2 * n_sources]
    o_ref = rest[2 * n_sources]
    tq = q_ref.shape[0]
    lane = lax.broadcasted_iota(jnp.int32, (tq, LANES), 1)
    sls = [slice(j * HEAD_SLAB, (j + 1) * HEAD_SLAB) for j in range(2)]
    qs = [q_ref[:, sl] for sl in sls]
    chunks = [(si, st) for si, k in enumerate(k_refs) for st in range(0, k.shape[0], MLA_KEY_CHUNK)]

    def score(j, ch):
        si, st = ch
        return _nt_dot(qs[j], k_refs[si][st:st + MLA_KEY_CHUNK, sls[j]])

    def row_max(ss):
        return jnp.max(functools.reduce(jnp.maximum, ss), axis=-1, keepdims=True)

    def weighted(j, ch, p):
        si, st = ch
        return _dot(p, v_refs[si][st:st + MLA_KEY_CHUNK, sls[j]])

    s0 = [score(0, ch) for ch in chunks]
    m0 = row_max(s0)
    s1, p0 = [], []
    for c, ch in enumerate(chunks):
        s1.append(score(1, ch))
        p0.append(jnp.exp2(s0[c] - m0).astype(BF16))
    m1 = row_max(s1)
    o0, p1 = None, []
    for c, ch in enumerate(chunks):
        t = weighted(0, ch, p0[c])
        o0 = t if o0 is None else o0 + t
        p1.append(jnp.exp2(s1[c] - m1).astype(BF16))
    o1 = functools.reduce(jnp.add, [weighted(1, ch, p1[c]) for c, ch in enumerate(chunks)])
    outs = [o / pltpu.roll(o, V_HEAD, axis=1) for o in (o0, o1)]
    o_ref[...] = jnp.where(lane < V_HEAD, outs[0], outs[1]).astype(BF16)


def mla_attention(q, k_new, v_new, k_ctx, v_ctx, n_batch, tq=512):
    n = q.shape[0]
    seq = n // n_batch
    tq = min(tq, seq)
    qt = seq // tq
    pairs = MLA_HEADS // 2
    srcs_k, srcs_v = [k_new], [v_new]
    lens = [seq]
    if k_ctx is not None:
        srcs_k.append(k_ctx)
        srcs_v.append(v_ctx)
        lens.append(k_ctx.shape[0] // n_batch)
    in_specs = [pl.BlockSpec((tq, 2 * HEAD_SLAB), lambda b, p, t: (b * qt + t, p))]
    in_specs += [pl.BlockSpec((ln, 2 * HEAD_SLAB), lambda b, p, t: (b, p)) for ln in lens]
    in_specs += [pl.BlockSpec((ln, 2 * LANES), lambda b, p, t: (b, p)) for ln in lens]
    return pl.pallas_call(
        functools.partial(_mla_attn_kernel, n_sources=len(lens)),
        grid=(n_batch, pairs, qt),
        in_specs=in_specs,
        out_specs=pl.BlockSpec((tq, 2 * V_HEAD), lambda b, p, t: (b * qt + t, p)),
        out_shape=jax.ShapeDtypeStruct((n, MLA_HEADS * V_HEAD), BF16),
        compiler_params=_cparams(("parallel", "parallel", "arbitrary")),
        name="mla_attention",
    )(q, *srcs_k, *srcs_v)


def rope_lane_tables(n):
    t = jnp.arange(n)
    row = (t // GRID_W).astype(F32)
    col = (t % GRID_W).astype(F32)
    inv = ROPE_THETA ** (-jnp.arange(ROPE_PAIRS_AXIS, dtype=F32) / ROPE_PAIRS_AXIS)
    ang = jnp.concatenate([row[:, None] * inv, col[:, None] * inv], axis=-1)
    cos, sin = jnp.cos(ang), jnp.sin(ang)
    ones = jnp.ones((n, QK_NOPE), F32)
    zeros = jnp.zeros((n, QK_NOPE), F32)
    tail1 = jnp.ones((n, HEAD_SLAB - QK_NOPE - QK_ROPE), F32)
    tail0 = jnp.zeros((n, HEAD_SLAB - QK_NOPE - QK_ROPE), F32)
    z16 = jnp.zeros_like(sin)
    cos_t = jnp.concatenate([ones, cos, cos, tail1], axis=-1)
    sin_fwd = jnp.concatenate([zeros, z16, sin, tail0], axis=-1)
    sin_bwd = jnp.concatenate([zeros, -sin, z16, tail0], axis=-1)
    return cos_t, sin_fwd, sin_bwd


def _pad_mod(m, groups):
    m = m.reshape(groups, N_MOD, D_MODEL)
    return jnp.pad(m, ((0, 0), (0, MOD_ROWS - N_MOD), (0, 0)))


def _mla_weights(w_in, w_uq, w_ukv):
    pad_tail = HEAD_SLAB - QK_NOPE - QK_ROPE
    slab_cols = jnp.pad(w_in[:, Q_LORA + KV_LORA:], ((0, 0), (QK_NOPE, pad_tail)))
    win = jnp.concatenate([w_in[:, :Q_LORA + KV_LORA], slab_cols], axis=1).astype(BF16)
    wuq = jnp.pad(w_uq.reshape(Q_LORA, MLA_HEADS, QK_NOPE + QK_ROPE), ((0, 0), (0, 0), (0, pad_tail)))
    wuq = wuq.reshape(Q_LORA, MLA_HEADS * HEAD_SLAB).astype(BF16)
    wkv = w_ukv.reshape(KV_LORA, MLA_HEADS, QK_NOPE + V_HEAD)
    wk = jnp.pad(wkv[:, :, :QK_NOPE], ((0, 0), (0, 0), (0, HEAD_SLAB - QK_NOPE)))
    wk = wk.reshape(KV_LORA, MLA_HEADS * HEAD_SLAB).astype(BF16)
    wv = wkv[:, :, QK_NOPE:].reshape(KV_LORA, MLA_HEADS * V_HEAD).astype(BF16)
    return win, wuq, wk, wv


def kernel(x_prompt, x_sample, cache_na_k, cache_na_v, cache_mla_ckv, cache_mla_krope, c, c_ctx, w_ada, b_ada, ln_mix_g, ln_mix_b, ln_ffn_g, ln_ffn_b, w_in_ab, conv_w, na_rpb, w_out_ab, w_in_mla, q_norm_g, w_uq, kv_norm_g, w_ukv, w_out_mla, peer_w_q, peer_subkeys, peer_u, peer_v):
    nb_p, seq_p, d = x_prompt.shape
    nb_s, seq_s, _ = x_sample.shape
    past = cache_na_k.shape[2]
    rows = seq_s // GRID_W
    n_p = nb_p * seq_p
    xp = x_prompt.reshape(n_p, d)
    xs = x_sample.reshape(nb_s * seq_s, d)

    n_cond = 1 + nb_s
    cond_rows = -(-n_cond // 8) * 8
    cvec = jnp.concatenate([c_ctx[None], c, jnp.zeros((cond_rows - n_cond, d), F32)], axis=0)

    outs = {}
    for i in range(DEPTH):
        j = i // 2
        mods = ada_modulation(cvec, w_ada[i], b_ada[i])
        mod_p = _pad_mod(mods[0:1], 1)
        mod_s = _pad_mod(mods[1:n_cond], nb_s)
        lg = ln_mix_g[i].reshape(1, d)
        lb = ln_mix_b[i].reshape(1, d)
        if i % 2 == 0:
            w_in = w_in_ab[j].astype(BF16)
            w_out = w_out_ab[j].astype(BF16)
            cw = jnp.pad(conv_w[j], ((0, 8 - CONV_TAPS), (0, 0)))
            qkv_p, gu_p, kv_p = inproj_ab(xp, mod_p, w_in, n_p, True)
            qkv_s, gu_s = inproj_ab(xs, mod_s, w_in, seq_s, False)
            outs["na_k"] = kv_p[:, :NA_WIDTH].reshape(nb_p, 1, seq_p, NA_HEADS, NA_HEAD_DIM)
            outs["na_v"] = kv_p[:, NA_WIDTH:].reshape(nb_p, 1, seq_p, NA_HEADS, NA_HEAD_DIM)
            att_p = ctx_attention(qkv_p, seq_p)
            kc = cache_na_k[:, j].reshape(nb_s * past, NA_WIDTH).astype(BF16)
            vc = cache_na_v[:, j].reshape(nb_s * past, NA_WIDTH).astype(BF16)
            att_s = na_attention(qkv_s, kc, vc, na_bias_table(na_rpb[j]), nb_s, rows)
            xp = outproj_ab(att_p, gu_p, xp, mod_p, w_out, cw, lg, lb, n_p, seq_p)
            xs = outproj_ab(att_s, gu_s, xs, mod_s, w_out, cw, lg, lb, seq_s, seq_s)
        else:
            win, wuq, wk, wv = _mla_weights(w_in_mla[j], w_uq[j], w_ukv[j])
            w_out = w_out_mla[j].astype(BF16)
            qg = q_norm_g[j].reshape(1, Q_LORA)
            kvg = kv_norm_g[j].reshape(1, KV_LORA)
            q_p, ckv_p, slab_p = mla_inproj(xp, mod_p, win, qg, kvg, wuq, None, n_p, seq_p)
            q_s, ckv_s, slab_s = mla_inproj(xs, mod_s, win, qg, kvg, wuq, rope_lane_tables(seq_s), seq_s, seq_s)
            outs["ckv"] = ckv_p.reshape(nb_p, 1, seq_p, KV_LORA)
            outs["kr"] = slab_p[:, QK_NOPE:QK_NOPE + QK_ROPE].reshape(nb_p, 1, seq_p, QK_ROPE)
            k_p, v_p = mla_kv(ckv_p, slab_p, wk, wv)
            k_s, v_s = mla_kv(ckv_s, slab_s, wk, wv)
            ckv_c = cache_mla_ckv[:, j].reshape(nb_s * past, KV_LORA)
            slab_c = jnp.pad(cache_mla_krope[:, j].reshape(nb_s * past, QK_ROPE),
                             ((0, 0), (QK_NOPE, HEAD_SLAB - QK_NOPE - QK_ROPE)))
            k_c, v_c = mla_kv(ckv_c, slab_c, wk, wv)
            att_p = mla_attention(q_p, k_p, v_p, None, None, nb_p)
            att_s = mla_attention(q_s, k_s, v_s, k_c, v_c, nb_s)
            xp = outproj(att_p, xp, mod_p, w_out, lg, lb, n_p)
            xs = outproj(att_s, xs, mod_s, w_out, lg, lb, seq_s)
        wq = peer_w_q[i].astype(BF16)
        sk = peer_subkeys[i].reshape(2 * PEER_HEADS, N_KEYS, PEER_HALF).astype(BF16)
        u = peer_u[i].astype(BF16)
        vt = peer_v[i].T.astype(BF16)
        fg = ln_ffn_g[i].reshape(1, d)
        fb = ln_ffn_b[i].reshape(1, d)
        xp = peer_block(xp, mod_p, wq, sk, u, vt, fg, fb, n_p)
        xs = peer_block(xs, mod_s, wq, sk, u, vt, fg, fb, seq_s)
    return (xp.reshape(nb_p, seq_p, d), xs.reshape(nb_s, seq_s, d),
            outs["na_k"], outs["na_v"], outs["ckv"], outs["kr"])
```

```python
import functools

import jax
import jax.numpy as jnp
import numpy as np
from jax import lax
from jax.experimental import pallas as pl
from jax.experimental.pallas import tpu as pltpu

F32 = jnp.float32
BF16 = jnp.bfloat16

D_MODEL = 1024
DEPTH = 2
GRID_W = 64
N_MOD = 6
ALPHA = (2.0 * DEPTH) ** 0.25
LN_EPS = 1e-5
RMS_EPS = 1e-6
NA_HEADS = 8
NA_HEAD_DIM = 64
NA_WIDTH = NA_HEADS * NA_HEAD_DIM
NA_MAX_ROWS = 8
NA_WIN_COLS = 16
NA_SCALE = NA_HEAD_DIM ** -0.5
CONV_WIDTH = 512
CONV_TAPS = 3
MLA_HEADS = 16
Q_LORA = 256
KV_LORA = 256
QK_NOPE = 64
QK_ROPE = 32
V_HEAD = 64
MLA_SCALE = (QK_NOPE + QK_ROPE) ** -0.5
ROPE_THETA = 10000.0
ROPE_PAIRS_AXIS = QK_ROPE // 4
PEER_HEADS = 8
PEER_HALF = 128
N_KEYS = 128
N_EXPERTS = N_KEYS * N_KEYS
PEER_TOPK = 16

LANES = 128
MOD_ROWS = 8
HEAD_SLAB = 128
MASK_BIAS = -1e30
VMEM_LIMIT = 52 * 1024 * 1024


def _cparams(sem, flags=None):
    return pltpu.CompilerParams(dimension_semantics=sem, vmem_limit_bytes=VMEM_LIMIT, flags=flags)


def _nt_dot(a, b):
    return lax.dot_general(a, b, (((1,), (1,)), ((), ())), preferred_element_type=F32)


def _dot(a, b):
    return jnp.dot(a, b, preferred_element_type=F32)


def _split_bf16(a):
    hi = a.astype(BF16)
    lo = (a - hi.astype(F32)).astype(BF16)
    return hi, lo


def _layer_norm(z, g, b):
    mu = jnp.mean(z, axis=-1, keepdims=True)
    zc = z - mu
    var = jnp.mean(zc * zc, axis=-1, keepdims=True)
    return zc * lax.rsqrt(var + LN_EPS) * g + b


LOG2E = 1.4426950408889634
LN2 = 0.6931471805599453
_GELU_A = -2.0 * LOG2E * 0.7978845608028654
_GELU_B = _GELU_A * 0.044715


def _gelu_tanh(x):
    return x / (1.0 + jnp.exp2(x * (_GELU_A + _GELU_B * (x * x))))


def _ada_kernel(c_ref, w_ref, b_ref, o_ref):
    c = c_ref[...]
    a = c * (1.0 / (1.0 + jnp.exp(-c)))
    a_hi, a_lo = _split_bf16(a)
    w_hi, w_lo = _split_bf16(w_ref[...])
    o_ref[...] = _dot(a_hi, w_hi) + _dot(a_hi, w_lo) + _dot(a_lo, w_hi) + b_ref[...]


def ada_modulation(cvec, w_ada, b_ada, tn=512):
    r, d = cvec.shape
    n = w_ada.shape[1]
    return pl.pallas_call(
        _ada_kernel,
        grid=(n // tn,),
        in_specs=[pl.BlockSpec((r, d), lambda j: (0, 0)),
                  pl.BlockSpec((d, tn), lambda j: (0, j)),
                  pl.BlockSpec((1, tn), lambda j: (0, j))],
        out_specs=pl.BlockSpec((r, tn), lambda j: (0, j)),
        out_shape=jax.ShapeDtypeStruct((r, n), F32),
        compiler_params=_cparams(("arbitrary",)),
        name="ada_modulation",
    )(cvec, w_ada, b_ada.reshape(1, n))


def _inproj_ab_kernel(x_ref, mod_ref, w_ref, qkv_ref, gu_ref, *kv_ref):
    m = mod_ref[0]
    h = x_ref[...] * (1.0 + m[1:2]) + m[0:1]
    y = _dot(h.astype(BF16), w_ref[...])
    w3 = 3 * NA_WIDTH
    qkv_ref[:, :NA_WIDTH] = (y[:, :NA_WIDTH] * (NA_SCALE * LOG2E)).astype(BF16)
    qkv_ref[:, NA_WIDTH:] = y[:, NA_WIDTH:w3].astype(BF16)
    if kv_ref:
        kv_ref[0][...] = y[:, NA_WIDTH:w3]
    gu_ref[:, :CONV_WIDTH] = y[:, w3:w3 + CONV_WIDTH].astype(BF16)
    gu_ref[:, CONV_WIDTH:] = (y[:, w3 + CONV_WIDTH:w3 + 2 * CONV_WIDTH]
                              * y[:, w3 + 2 * CONV_WIDTH:]).astype(BF16)


def inproj_ab(x, mod, w_bf16, tokens_per_group, emit_kv, tm=512):
    n, d = x.shape
    tm = min(tm, n)
    out_shape = [jax.ShapeDtypeStruct((n, 3 * NA_WIDTH), BF16),
                 jax.ShapeDtypeStruct((n, 2 * CONV_WIDTH), BF16)]
    out_specs = [pl.BlockSpec((tm, 3 * NA_WIDTH), lambda i: (i, 0)),
                 pl.BlockSpec((tm, 2 * CONV_WIDTH), lambda i: (i, 0))]
    if emit_kv:
        out_shape.append(jax.ShapeDtypeStruct((n, 2 * NA_WIDTH), F32))
        out_specs.append(pl.BlockSpec((tm, 2 * NA_WIDTH), lambda i: (i, 0)))
    return pl.pallas_call(
        _inproj_ab_kernel,
        grid=(n // tm,),
        in_specs=[pl.BlockSpec((tm, d), lambda i: (i, 0)),
                  pl.BlockSpec((1, MOD_ROWS, d), lambda i: ((i * tm) // tokens_per_group, 0, 0)),
                  pl.BlockSpec(w_bf16.shape, lambda i: (0, 0))],
        out_specs=out_specs,
        out_shape=out_shape,
        compiler_params=_cparams(("parallel",)),
        name="inproj_ab",
    )(x, mod, w_bf16)


def _pair_attention(instances):
    heads = []
    for n, (q_pair, _, _, _) in enumerate(instances):
        lane = lax.broadcasted_iota(jnp.int32, q_pair.shape, 1)
        qf = q_pair.astype(F32)
        for j in range(2):
            in_head = (lane >= j * NA_HEAD_DIM) & (lane < (j + 1) * NA_HEAD_DIM)
            heads.append((n, j, jnp.where(in_head, qf, 0.0).astype(BF16)))
    scores = []
    for n, j, qm in heads:
        _, keys, _, biases = instances[n]
        ss = [_nt_dot(qm, k) for k in keys]
        scores.append([s if b is None else s + b for s, b in zip(ss, biases[j])])
    mxs = [functools.reduce(jnp.maximum, [jnp.max(s, axis=-1, keepdims=True) for s in ss]) for ss in scores]
    ess = [[jnp.exp2(s - mx) for s in ss] for ss, mx in zip(scores, mxs)]
    dens = [functools.reduce(jnp.add, [jnp.sum(e, axis=-1, keepdims=True) for e in es]) for es in ess]
    outs = []
    for (n, j, _), es in zip(heads, ess):
        values = instances[n][2]
        outs.append(functools.reduce(jnp.add, [_dot(e.astype(BF16), v) for e, v in zip(es, values)]))
    outs = [o / d for o, d in zip(outs, dens)]
    lane = lax.broadcasted_iota(jnp.int32, outs[0].shape, 1)
    return [jnp.where(lane < NA_HEAD_DIM, outs[2 * n], outs[2 * n + 1]) for n in range(len(instances))]


NA_ROWS_PER_STEP = 2


def _na_attn_kernel(q_ref, k_ref, v_ref, kc_ref, vc_ref, *rest, rows):
    bias_refs = rest[:NA_ROWS_PER_STEP]
    o_ref = rest[NA_ROWS_PER_STEP]
    band = NA_MAX_ROWS * GRID_W
    instances, places = [], []
    for rr in range(NA_ROWS_PER_STEP):
        r = pl.program_id(1) * NA_ROWS_PER_STEP + rr
        rs = jnp.clip(r - NA_MAX_ROWS // 2, 0, rows - NA_MAX_ROWS)
        start = pl.multiple_of(rs * GRID_W, GRID_W)
        qrows = slice(rr * GRID_W, (rr + 1) * GRID_W)
        for p in range(NA_HEADS // 2):
            sl = slice(p * LANES, (p + 1) * LANES)
            keys = [k_ref[pl.ds(start, band), sl], kc_ref[:, sl]]
            values = [v_ref[pl.ds(start, band), sl], vc_ref[:, sl]]
            biases = [[bias_refs[rr][0, 2 * p + j], None] for j in range(2)]
            instances.append((q_ref[qrows, sl], keys, values, biases))
            places.append((qrows, sl))
    for (qrows, sl), o in zip(places, _pair_attention(instances)):
        o_ref[qrows, sl] = o.astype(BF16)


def na_attention(qkv, kc, vc, bias, n_batch, rows):
    n = qkv.shape[0]
    seq = rows * GRID_W
    lc = kc.shape[0] // n_batch
    half = NA_MAX_ROWS // 2

    steps = rows // NA_ROWS_PER_STEP
    tq = NA_ROWS_PER_STEP * GRID_W

    def bias_spec(rr):
        def idx(b, g):
            r = g * NA_ROWS_PER_STEP + rr
            return (r - jnp.clip(r - half, 0, rows - NA_MAX_ROWS), 0, 0, 0)
        return pl.BlockSpec((1, NA_HEADS, GRID_W, NA_MAX_ROWS * GRID_W), idx)

    return pl.pallas_call(
        functools.partial(_na_attn_kernel, rows=rows),
        grid=(n_batch, steps),
        in_specs=[pl.BlockSpec((tq, NA_WIDTH), lambda b, g: (b * steps + g, 0)),
                  pl.BlockSpec((seq, NA_WIDTH), lambda b, g: (b, 1)),
                  pl.BlockSpec((seq, NA_WIDTH), lambda b, g: (b, 2)),
                  pl.BlockSpec((lc, NA_WIDTH), lambda b, g: (b, 0)),
                  pl.BlockSpec((lc, NA_WIDTH), lambda b, g: (b, 0))]
        + [bias_spec(rr) for rr in range(NA_ROWS_PER_STEP)],
        out_specs=pl.BlockSpec((tq, NA_WIDTH), lambda b, g: (b * steps + g, 0)),
        out_shape=jax.ShapeDtypeStruct((n, NA_WIDTH), BF16),
        compiler_params=_cparams(("parallel", "arbitrary")),
        name="na_attention",
    )(qkv, qkv, qkv, kc, vc, *([bias] * NA_ROWS_PER_STEP))


def na_bias_table(rpb):
    c = np.arange(GRID_W)[:, None]
    kc = np.arange(GRID_W)[None, :]
    cs = np.clip(c - NA_WIN_COLS // 2, 0, GRID_W - NA_WIN_COLS)
    valid = (kc >= cs) & (kc < cs + NA_WIN_COLS)
    dc = kc - c + (NA_WIN_COLS - 1)
    cc, kk = np.nonzero(valid)
    place = np.zeros((2 * NA_WIN_COLS - 1, GRID_W, GRID_W), np.float32)
    place[dc[cc, kk], cc, kk] = 1.0
    t = jnp.einsum("hrd,dck->hrck", rpb, place, precision=lax.Precision.HIGHEST)
    t = jnp.where(valid[None, None], t * LOG2E, MASK_BIAS)
    bands = [t[:, NA_MAX_ROWS - 1 - off:2 * NA_MAX_ROWS - 1 - off] for off in range(NA_MAX_ROWS)]
    b = jnp.stack(bands, axis=0)
    b = jnp.transpose(b, (0, 1, 3, 2, 4))
    return b.reshape(NA_MAX_ROWS, NA_HEADS, GRID_W, NA_MAX_ROWS * GRID_W).astype(F32)


def _ctx_attn_kernel(q_ref, k_ref, v_ref, o_ref):
    slabs = [slice(p * LANES, (p + 1) * LANES) for p in range(NA_HEADS // 2)]
    instances = [(q_ref[:, sl], [k_ref[:, sl]], [v_ref[:, sl]], [[None], [None]]) for sl in slabs]
    for sl, o in zip(slabs, _pair_attention(instances)):
        o_ref[:, sl] = o.astype(BF16)


def ctx_attention(qkv, seq):
    n = qkv.shape[0]
    return pl.pallas_call(
        _ctx_attn_kernel,
        grid=(n // seq,),
        in_specs=[pl.BlockSpec((seq, NA_WIDTH), lambda b: (b, 0)),
                  pl.BlockSpec((seq, NA_WIDTH), lambda b: (b, 1)),
                  pl.BlockSpec((seq, NA_WIDTH), lambda b: (b, 2))],
        out_specs=pl.BlockSpec((seq, NA_WIDTH), lambda b: (b, 0)),
        out_shape=jax.ShapeDtypeStruct((n, NA_WIDTH), BF16),
        compiler_params=_cparams(("parallel",)),
        name="ctx_attention",
    )(qkv, qkv, qkv)


HALO = 16


def _outproj_ab_kernel(att_ref, gu_ref, prev_ref, next_ref, x_ref, mod_ref, w_ref, cw_ref, g_ref, b_ref,
                       o_ref, *, tiles_per_seq):
    i = pl.program_id(0)
    tm = x_ref.shape[0]
    m = mod_ref[0]
    gu = gu_ref[...].astype(F32)
    bg = gu[:, :CONV_WIDTH]
    u = gu[:, CONV_WIDTH:]
    has_prev = (i % tiles_per_seq != 0).astype(F32)
    has_next = (i % tiles_per_seq != tiles_per_seq - 1).astype(F32)
    prev_row = prev_ref[HALO - 1:HALO, CONV_WIDTH:].astype(F32) * has_prev
    next_row = next_ref[0:1, CONV_WIDTH:].astype(F32) * has_next
    row = lax.broadcasted_iota(jnp.int32, u.shape, 0)
    u_m1 = jnp.where(row == 0, prev_row, pltpu.roll(u, 1, axis=0))
    u_p1 = jnp.where(row == tm - 1, next_row, pltpu.roll(u, tm - 1, axis=0))
    cw = cw_ref[...]
    conv = bg * (u_m1 * cw[0:1] + u * cw[1:2] + u_p1 * cw[2:3])
    y = _dot(att_ref[...], w_ref[:NA_WIDTH, :]) + _dot(conv.astype(BF16), w_ref[NA_WIDTH:, :])
    z = ALPHA * x_ref[...] + m[2:3] * y
    o_ref[...] = _layer_norm(z, g_ref[...], b_ref[...])


def outproj_ab(att, gu, x, mod, w_bf16, conv_w8, ln_g, ln_b, tokens_per_group, seq, tm=256):
    n, d = x.shape
    tm = min(tm, seq)
    hb = tm // HALO
    last_halo = n // HALO - 1
    return pl.pallas_call(
        functools.partial(_outproj_ab_kernel, tiles_per_seq=seq // tm),
        grid=(n // tm,),
        in_specs=[pl.BlockSpec((tm, NA_WIDTH), lambda i: (i, 0)),
                  pl.BlockSpec((tm, 2 * CONV_WIDTH), lambda i: (i, 0)),
                  pl.BlockSpec((HALO, 2 * CONV_WIDTH), lambda i: (jnp.maximum(i * hb - 1, 0), 0)),
                  pl.BlockSpec((HALO, 2 * CONV_WIDTH), lambda i: (jnp.minimum((i + 1) * hb, last_halo), 0)),
                  pl.BlockSpec((tm, d), lambda i: (i, 0)),
                  pl.BlockSpec((1, MOD_ROWS, d), lambda i: ((i * tm) // tokens_per_group, 0, 0)),
                  pl.BlockSpec(w_bf16.shape, lambda i: (0, 0)),
                  pl.BlockSpec(conv_w8.shape, lambda i: (0, 0)),
                  pl.BlockSpec((1, d), lambda i: (0, 0)),
                  pl.BlockSpec((1, d), lambda i: (0, 0))],
        out_specs=pl.BlockSpec((tm, d), lambda i: (i, 0)),
        out_shape=jax.ShapeDtypeStruct((n, d), F32),
        compiler_params=_cparams(("parallel",)),
        name="outproj_ab",
    )(att, gu, gu, gu, x, mod, w_bf16, conv_w8, ln_g, ln_b)


def _outproj_kernel(att_ref, x_ref, mod_ref, w_ref, g_ref, b_ref, o_ref):
    m = mod_ref[0]
    y = _dot(att_ref[...], w_ref[...])
    z = ALPHA * x_ref[...] + m[2:3] * y
    o_ref[...] = _layer_norm(z, g_ref[...], b_ref[...])


def outproj(att, x, mod, w_bf16, ln_g, ln_b, tokens_per_group, tm=512):
    n, d = x.shape
    k = att.shape[1]
    return pl.pallas_call(
        _outproj_kernel,
        grid=(n // tm,),
        in_specs=[pl.BlockSpec((tm, k), lambda i: (i, 0)),
                  pl.BlockSpec((tm, d), lambda i: (i, 0)),
                  pl.BlockSpec((1, MOD_ROWS, d), lambda i: ((i * tm) // tokens_per_group, 0, 0)),
                  pl.BlockSpec(w_bf16.shape, lambda i: (0, 0)),
                  pl.BlockSpec((1, d), lambda i: (0, 0)),
                  pl.BlockSpec((1, d), lambda i: (0, 0))],
        out_specs=pl.BlockSpec((tm, d), lambda i: (i, 0)),
        out_shape=jax.ShapeDtypeStruct((n, d), F32),
        compiler_params=_cparams(("parallel",)),
        name="outproj",
    )(att, x, mod, w_bf16, ln_g, ln_b)


def _peer_scores_kernel(x_ref, mod_ref, wq_ref, sk_ref, hm_ref, st_ref):
    m = mod_ref[0]
    hb = (x_ref[...] * (1.0 + m[4:5]) + m[3:4]).astype(BF16)
    hm_ref[...] = hb
    q = _dot(hb, wq_ref[...])
    for c in range(2 * PEER_HEADS):
        qc = q[:, c * PEER_HALF:(c + 1) * PEER_HALF].astype(BF16)
        st_ref[c] = _nt_dot(sk_ref[c], qc) * LOG2E


def peer_scores(x, mod, wq_bf16, subkeys_bf16, tokens_per_group, tm=512):
    n, d = x.shape
    nc = 2 * PEER_HEADS
    return pl.pallas_call(
        _peer_scores_kernel,
        grid=(n // tm,),
        in_specs=[pl.BlockSpec((tm, d), lambda i: (i, 0)),
                  pl.BlockSpec((1, MOD_ROWS, d), lambda i: ((i * tm) // tokens_per_group, 0, 0)),
                  pl.BlockSpec(wq_bf16.shape, lambda i: (0, 0)),
                  pl.BlockSpec(subkeys_bf16.shape, lambda i: (0, 0, 0))],
        out_specs=[pl.BlockSpec((tm, d), lambda i: (i, 0)),
                   pl.BlockSpec((nc, N_KEYS, tm), lambda i: (0, 0, i))],
        out_shape=[jax.ShapeDtypeStruct((n, d), BF16),
                   jax.ShapeDtypeStruct((nc, N_KEYS, n), F32)],
        compiler_params=_cparams(("parallel",)),
        name="peer_scores",
    )(x, mod, wq_bf16, subkeys_bf16)


SUBLANES = 8
N_SLABS = N_KEYS // SUBLANES


def _sort_network(n):
    pairs = []
    p = 1
    while p < n:
        k = p
        while k >= 1:
            for j in range(k % p, n - k, 2 * k):
                for i in range(min(k, n - j - k)):
                    if (i + j) // (2 * p) == (i + j + k) // (2 * p):
                        pairs.append((i + j, i + j + k))
            k //= 2
        p *= 2
    return pairs


_SLAB_SORT = _sort_network(N_SLABS)


def _top_values(s_ref, c, n_top):
    return _top_of_slabs([s_ref[c, r] for r in range(N_SLABS)], n_top)


def _top_of_slabs(slabs, n_top):
    x = list(slabs)
    n = len(x)
    for a, b in _SLAB_SORT:
        if b < n:
            x[a], x[b] = jnp.maximum(x[a], x[b]), jnp.minimum(x[a], x[b])
    sub = lax.broadcasted_iota(jnp.int32, x[0].shape, 0).astype(F32)
    x.append(jnp.full(x[0].shape, -jnp.inf, F32))
    vals = []
    for k in range(n_top):
        mx = jnp.max(x[0], axis=0, keepdims=True)
        vals.append(mx)
        if k == n_top - 1:
            break
        first = jnp.min(jnp.where(x[0] == mx, sub, float(SUBLANES)), axis=0, keepdims=True)
        pop = sub == first
        for r in range(min(n, n_top - 1 - k)):
            x[r] = jnp.where(pop, x[r + 1], x[r])
    return vals


def _rows_to_slab(rows_list):
    t = rows_list[0].shape[1]
    ridx = lax.broadcasted_iota(jnp.int32, (SUBLANES, t), 0)
    out = jnp.full((SUBLANES, t), -jnp.inf, F32)
    for r, row in enumerate(rows_list):
        out = jnp.where(ridx == r, row, out)
    return out


def _peer_select_kernel(st_ref, thr_ref, lse_ref, m2_ref):
    k = PEER_TOPK
    for h in range(PEER_HEADS):
        v1 = _top_values(st_ref, 2 * h, k + 1)
        v2 = _top_values(st_ref, 2 * h + 1, k + 1)
        v1_lo, v1_hi = _rows_to_slab(v1[0:8]), _rows_to_slab(v1[8:16])
        v2_lo, v2_hi = _rows_to_slab(v2[0:8]), _rows_to_slab(v2[8:16])
        ridx = lax.broadcasted_iota(jnp.int32, v1_lo.shape, 0)
        v1_mid = jnp.where(ridx >= 4, v1_lo, -jnp.inf)
        cands = [v1[0] + v2_lo, v1[0] + v2_hi, v1[1] + v2_lo, v1[2] + v2_lo, v1[3] + v2_lo,
                 v1_hi + v2[0], v1_mid + v2[0], v1_mid + v2[1], v1_mid + v2[2],
                 _rows_to_slab([v1[k] + v2[0], v1[0] + v2[k]])]
        top = v1[0] + v2[0]
        best = _top_of_slabs(cands, k + 1)
        thr, nxt = best[k - 1], best[k]
        z = functools.reduce(jnp.add, [jnp.where(cd >= thr, jnp.exp2(cd - top), 0.0) for cd in cands])
        z = jnp.sum(z, axis=0, keepdims=True)
        thr_ref[h:h + 1, :] = 0.5 * (thr + nxt)
        lse_ref[h:h + 1, :] = v1[0] + jnp.log(z) * LOG2E
        m2_ref[h:h + 1, :] = v2[0]


def peer_select(st, tl=256):
    nc, _, _, n = st.shape
    stat = pl.BlockSpec((PEER_HEADS, tl), lambda i: (0, i))
    return pl.pallas_call(
        _peer_select_kernel,
        grid=(n // tl,),
        in_specs=[pl.BlockSpec((nc, N_SLABS, SUBLANES, tl), lambda i: (0, 0, 0, i))],
        out_specs=[stat, stat, stat],
        out_shape=[jax.ShapeDtypeStruct((PEER_HEADS, n), F32)] * 3,
        compiler_params=_cparams(("parallel",)),
        name="peer_select",
    )(st)


PEER_LANE_CHUNK = 128
PEER_MXU_CHUNK = 256


def _peer_dense_kernel(hm_ref, u_ref, vt_ref, st_ref, thr_ref, lse_ref, m2_ref, x_ref, mod_ref, g_ref, b_ref,
                       o_ref, acc_ref, ht_ref, at_ref, e2_ref, cut_rep, f1_rep, *, gate_row):
    e = pl.program_id(1)
    tm = hm_ref.shape[0]

    @pl.when(e == 0)
    def _():
        acc_ref[...] = jnp.zeros_like(acc_ref)
        for h in range(PEER_HEADS):
            e2_ref[h] = jnp.exp2(st_ref[2 * h + 1] - m2_ref[h:h + 1, :])

    for c in range(tm // PEER_MXU_CHUNK):
        ls = slice(c * PEER_MXU_CHUNK, (c + 1) * PEER_MXU_CHUNK)
        ht_ref[:, ls] = _nt_dot(u_ref[...], hm_ref[ls, :])

    for h in range(PEER_HEADS):
        s1 = st_ref[2 * h, e]
        cut_rows = thr_ref[h:h + 1, :] - s1
        f1_rows = jnp.exp2(s1 - lse_ref[h:h + 1, :])
        for ii in range(SUBLANES):
            cut_rep[h * SUBLANES + ii] = jnp.broadcast_to(cut_rows[ii:ii + 1, :], (SUBLANES, tm))
            f1_rep[h * SUBLANES + ii] = jnp.broadcast_to(f1_rows[ii:ii + 1, :], (SUBLANES, tm))
    lanes_per_mxu = PEER_MXU_CHUNK // PEER_LANE_CHUNK
    for cm in range(tm // PEER_MXU_CHUNK):
        for ii in range(SUBLANES):
            rows = slice(ii * N_KEYS, (ii + 1) * N_KEYS)
            for c in range(cm * lanes_per_mxu, (cm + 1) * lanes_per_mxu):
                ls = slice(c * PEER_LANE_CHUNK, (c + 1) * PEER_LANE_CHUNK)
                w = jnp.zeros((N_SLABS, SUBLANES, PEER_LANE_CHUNK), F32)
                for h in range(PEER_HEADS):
                    sel = st_ref[2 * h + 1, :, :, ls] >= cut_rep[h * SUBLANES + ii, :, ls][None]
                    w = w + jnp.where(sel, e2_ref[h, :, :, ls] * f1_rep[h * SUBLANES + ii, :, ls][None], 0.0)
                act = _gelu_tanh(ht_ref[rows, ls])
                at_ref[rows, ls] = (act * w.reshape(N_KEYS, PEER_LANE_CHUNK)).astype(BF16)
        ms = slice(cm * PEER_MXU_CHUNK, (cm + 1) * PEER_MXU_CHUNK)
        acc_ref[:, ms] += _dot(vt_ref[...], at_ref[:, ms])

    @pl.when(e == pl.num_programs(1) - 1)
    def _():
        m = mod_ref[0]
        z = ALPHA * x_ref[...] + m[gate_row:gate_row + 1] * acc_ref[...].T
        o_ref[...] = _layer_norm(z, g_ref[...], b_ref[...])


def peer_dense(hm, u_bf16, vt_bf16, st, thr, lse, m2, x, mod, ln_g, ln_b, tokens_per_group, tm=512, te=1024):
    n, d = x.shape
    ne = u_bf16.shape[0]
    nc = st.shape[0]
    assert te == N_KEYS * SUBLANES
    return pl.pallas_call(
        functools.partial(_peer_dense_kernel, gate_row=5),
        grid=(n // tm, ne // te),
        in_specs=[pl.BlockSpec((tm, d), lambda i, g: (i, 0)),
                  pl.BlockSpec((te, d), lambda i, g: (g, 0)),
                  pl.BlockSpec((d, te), lambda i, g: (0, g)),
                  pl.BlockSpec((nc, N_SLABS, SUBLANES, tm), lambda i, g: (0, 0, 0, i)),
                  pl.BlockSpec((PEER_HEADS, tm), lambda i, g: (0, i)),
                  pl.BlockSpec((PEER_HEADS, tm), lambda i, g: (0, i)),
                  pl.BlockSpec((PEER_HEADS, tm), lambda i, g: (0, i)),
                  pl.BlockSpec((tm, d), lambda i, g: (i, 0)),
                  pl.BlockSpec((1, MOD_ROWS, d), lambda i, g: ((i * tm) // tokens_per_group, 0, 0)),
                  pl.BlockSpec((1, d), lambda i, g: (0, 0)),
                  pl.BlockSpec((1, d), lambda i, g: (0, 0))],
        out_specs=pl.BlockSpec((tm, d), lambda i, g: (i, 0)),
        out_shape=jax.ShapeDtypeStruct((n, d), F32),
        scratch_shapes=[pltpu.VMEM((d, tm), F32),
                        pltpu.VMEM((te, tm), F32),
                        pltpu.VMEM((te, tm), BF16),
                        pltpu.VMEM((PEER_HEADS, N_SLABS, SUBLANES, tm), F32),
                        pltpu.VMEM((PEER_HEADS * SUBLANES, SUBLANES, tm), F32),
                        pltpu.VMEM((PEER_HEADS * SUBLANES, SUBLANES, tm), F32)],
        compiler_params=_cparams(("parallel", "arbitrary")),
        name="peer_dense",
    )(hm, u_bf16, vt_bf16, st, thr, lse, m2, x, mod, ln_g, ln_b)


def peer_block(x, mod, wq, sk, u, vt, ln_g, ln_b, tokens_per_group):
    hm, st = peer_scores(x, mod, wq, sk, tokens_per_group)
    st = st.reshape(2 * PEER_HEADS, N_SLABS, SUBLANES, x.shape[0])
    cut, lse, m2 = peer_select(st)
    return peer_dense(hm, u, vt, st, cut, lse, m2, x, mod, ln_g, ln_b, tokens_per_group)


def _rope_slab(t, cos_t, sin_fwd, sin_bwd):
    return t * cos_t + pltpu.roll(t, QK_ROPE // 2, axis=1) * sin_fwd \
        + pltpu.roll(t, HEAD_SLAB - QK_ROPE // 2, axis=1) * sin_bwd


def _mla_inproj_kernel(x_ref, mod_ref, win_ref, qg_ref, kvg_ref, wuq_ref, *rest, use_rope):
    if use_rope:
        cos_ref, sf_ref, sb_ref, q_ref, ckv_ref, slab_ref = rest
    else:
        q_ref, ckv_ref, slab_ref = rest
    m = mod_ref[0]
    h = x_ref[...] * (1.0 + m[1:2]) + m[0:1]
    y = _dot(h.astype(BF16), win_ref[...])
    cq = y[:, :Q_LORA]
    ckv = y[:, Q_LORA:Q_LORA + KV_LORA]
    slab = y[:, Q_LORA + KV_LORA:]
    cqn = cq * lax.rsqrt(jnp.mean(cq * cq, axis=-1, keepdims=True) + RMS_EPS) * qg_ref[...]
    ckv_ref[...] = ckv * lax.rsqrt(jnp.mean(ckv * ckv, axis=-1, keepdims=True) + RMS_EPS) * kvg_ref[...]
    q = _dot(cqn.astype(BF16), wuq_ref[...]) * (MLA_SCALE * LOG2E)
    if use_rope:
        cos_t, sf, sb = cos_ref[...], sf_ref[...], sb_ref[...]
        slab = _rope_slab(slab, cos_t, sf, sb)
        for hh in range(MLA_HEADS):
            sl = slice(hh * HEAD_SLAB, (hh + 1) * HEAD_SLAB)
            q_ref[:, sl] = _rope_slab(q[:, sl], cos_t, sf, sb).astype(BF16)
    else:
        q_ref[...] = q.astype(BF16)
    slab_ref[...] = slab


def mla_inproj(x, mod, win_bf16, q_g, kv_g, wuq_bf16, rope, tokens_per_group, seq, tm=256):
    n, d = x.shape
    use_rope = rope is not None
    in_specs = [pl.BlockSpec((tm, d), lambda i: (i, 0)),
                pl.BlockSpec((1, MOD_ROWS, d), lambda i: ((i * tm) // tokens_per_group, 0, 0)),
                pl.BlockSpec(win_bf16.shape, lambda i: (0, 0)),
                pl.BlockSpec((1, Q_LORA), lambda i: (0, 0)),
                pl.BlockSpec((1, KV_LORA), lambda i: (0, 0)),
                pl.BlockSpec(wuq_bf16.shape, lambda i: (0, 0))]
    args = [x, mod, win_bf16, q_g, kv_g, wuq_bf16]
    if use_rope:
        tps = seq // tm
        in_specs += [pl.BlockSpec((tm, HEAD_SLAB), lambda i: (i % tps, 0))] * 3
        args += list(rope)
    return pl.pallas_call(
        functools.partial(_mla_inproj_kernel, use_rope=use_rope),
        grid=(n // tm,),
        in_specs=in_specs,
        out_specs=[pl.BlockSpec((tm, MLA_HEADS * HEAD_SLAB), lambda i: (i, 0)),
                   pl.BlockSpec((tm, KV_LORA), lambda i: (i, 0)),
                   pl.BlockSpec((tm, HEAD_SLAB), lambda i: (i, 0))],
        out_shape=[jax.ShapeDtypeStruct((n, MLA_HEADS * HEAD_SLAB), BF16),
                   jax.ShapeDtypeStruct((n, KV_LORA), F32),
                   jax.ShapeDtypeStruct((n, HEAD_SLAB), F32)],
        compiler_params=_cparams(("parallel",)),
        name="mla_inproj",
    )(*args)


def _mla_kv_kernel(ckv_ref, slab_ref, wk_ref, wv_ref, k_ref, v_ref):
    c = ckv_ref[...].astype(BF16)
    kn = _dot(c, wk_ref[...])
    slab = slab_ref[...]
    for hh in range(MLA_HEADS):
        sl = slice(hh * HEAD_SLAB, (hh + 1) * HEAD_SLAB)
        k_ref[:, sl] = (kn[:, sl] + slab).astype(BF16)
    v = _dot(c, wv_ref[...])
    lane = lax.broadcasted_iota(jnp.int32, (v.shape[0], LANES), 1)
    for p in range(MLA_HEADS // 2):
        pair = v[:, p * LANES:(p + 1) * LANES]
        v_ref[:, (2 * p) * LANES:(2 * p + 1) * LANES] = jnp.where(lane < V_HEAD, pair, 1.0).astype(BF16)
        v_ref[:, (2 * p + 1) * LANES:(2 * p + 2) * LANES] = jnp.where(lane < V_HEAD, 1.0, pair).astype(BF16)


def mla_kv(ckv, slab, wk_bf16, wv_bf16, tm=256):
    n = ckv.shape[0]
    tm = min(tm, n)
    return pl.pallas_call(
        _mla_kv_kernel,
        grid=(n // tm,),
        in_specs=[pl.BlockSpec((tm, KV_LORA), lambda i: (i, 0)),
                  pl.BlockSpec((tm, HEAD_SLAB), lambda i: (i, 0)),
                  pl.BlockSpec(wk_bf16.shape, lambda i: (0, 0)),
                  pl.BlockSpec(wv_bf16.shape, lambda i: (0, 0))],
        out_specs=[pl.BlockSpec((tm, MLA_HEADS * HEAD_SLAB), lambda i: (i, 0)),
                   pl.BlockSpec((tm, MLA_HEADS * LANES), lambda i: (i, 0))],
        out_shape=[jax.ShapeDtypeStruct((n, MLA_HEADS * HEAD_SLAB), BF16),
                   jax.ShapeDtypeStruct((n, MLA_HEADS * LANES), BF16)],
        compiler_params=_cparams(("parallel",)),
        name="mla_kv",
    )(ckv, slab, wk_bf16, wv_bf16)


MLA_KEY_CHUNK = 256


def _mla_attn_kernel(q_ref, *rest, n_sources):
    k_refs = rest[:n_sources]
    v_refs = rest[n_sources:2 * n_sources]
    o_ref = rest[2 * n_sources]
    tq = q_ref.shape[0]
    lane = lax.broadcasted_iota(jnp.int32, (tq, LANES), 1)
    sls = [slice(j * HEAD_SLAB, (j + 1) * HEAD_SLAB) for j in range(2)]
    qs = [q_ref[:, sl] for sl in sls]
    chunks = [(si, st) for si, k in enumerate(k_refs) for st in range(0, k.shape[0], MLA_KEY_CHUNK)]

    def score(j, ch):
        si, st = ch
        return _nt_dot(qs[j], k_refs[si][st:st + MLA_KEY_CHUNK, sls[j]])

    def row_max(ss):
        return jnp.max(functools.reduce(jnp.maximum, ss), axis=-1, keepdims=True)

    def weighted(j, ch, p):
        si, st = ch
        return _dot(p, v_refs[si][st:st + MLA_KEY_CHUNK, sls[j]])

    s0 = [score(0, ch) for ch in chunks]
    m0 = row_max(s0)
    s1, p0 = [], []
    for c, ch in enumerate(chunks):
        s1.append(score(1, ch))
        p0.append(jnp.exp2(s0[c] - m0).astype(BF16))
    m1 = row_max(s1)
    o0, p1 = None, []
    for c, ch in enumerate(chunks):
        t = weighted(0, ch, p0[c])
        o0 = t if o0 is None else o0 + t
        p1.append(jnp.exp2(s1[c] - m1).astype(BF16))
    o1 = functools.reduce(jnp.add, [weighted(1, ch, p1[c]) for c, ch in enumerate(chunks)])
    outs = [o / pltpu.roll(o, V_HEAD, axis=1) for o in (o0, o1)]
    o_ref[...] = jnp.where(lane < V_HEAD, outs[0], outs[1]).astype(BF16)


def mla_attention(q, k_new, v_new, k_ctx, v_ctx, n_batch, tq=512):
    n = q.shape[0]
    seq = n // n_batch
    tq = min(tq, seq)
    qt = seq // tq
    pairs = MLA_HEADS // 2
    srcs_k, srcs_v = [k_new], [v_new]
    lens = [seq]
    if k_ctx is not None:
        srcs_k.append(k_ctx)
        srcs_v.append(v_ctx)
        lens.append(k_ctx.shape[0] // n_batch)
    in_specs = [pl.BlockSpec((tq, 2 * HEAD_SLAB), lambda b, p, t: (b * qt + t, p))]
    in_specs += [pl.BlockSpec((ln, 2 * HEAD_SLAB), lambda b, p, t: (b, p)) for ln in lens]
    in_specs += [pl.BlockSpec((ln, 2 * LANES), lambda b, p, t: (b, p)) for ln in lens]
    return pl.pallas_call(
        functools.partial(_mla_attn_kernel, n_sources=len(lens)),
        grid=(n_batch, pairs, qt),
        in_specs=in_specs,
        out_specs=pl.BlockSpec((tq, 2 * V_HEAD), lambda b, p, t: (b * qt + t, p)),
        out_shape=jax.ShapeDtypeStruct((n, MLA_HEADS * V_HEAD), BF16),
        compiler_params=_cparams(("parallel", "parallel", "arbitrary")),
        name="mla_attention",
    )(q, *srcs_k, *srcs_v)


def rope_lane_tables(n):
    t = jnp.arange(n)
    row = (t // GRID_W).astype(F32)
    col = (t % GRID_W).astype(F32)
    inv = ROPE_THETA ** (-jnp.arange(ROPE_PAIRS_AXIS, dtype=F32) / ROPE_PAIRS_AXIS)
    ang = jnp.concatenate([row[:, None] * inv, col[:, None] * inv], axis=-1)
    cos, sin = jnp.cos(ang), jnp.sin(ang)
    ones = jnp.ones((n, QK_NOPE), F32)
    zeros = jnp.zeros((n, QK_NOPE), F32)
    tail1 = jnp.ones((n, HEAD_SLAB - QK_NOPE - QK_ROPE), F32)
    tail0 = jnp.zeros((n, HEAD_SLAB - QK_NOPE - QK_ROPE), F32)
    z16 = jnp.zeros_like(sin)
    cos_t = jnp.concatenate([ones, cos, cos, tail1], axis=-1)
    sin_fwd = jnp.concatenate([zeros, z16, sin, tail0], axis=-1)
    sin_bwd = jnp.concatenate([zeros, -sin, z16, tail0], axis=-1)
    return cos_t, sin_fwd, sin_bwd


def _pad_mod(m, groups):
    m = m.reshape(groups, N_MOD, D_MODEL)
    return jnp.pad(m, ((0, 0), (0, MOD_ROWS - N_MOD), (0, 0)))


def _mla_weights(w_in, w_uq, w_ukv):
    pad_tail = HEAD_SLAB - QK_NOPE - QK_ROPE
    slab_cols = jnp.pad(w_in[:, Q_LORA + KV_LORA:], ((0, 0), (QK_NOPE, pad_tail)))
    win = jnp.concatenate([w_in[:, :Q_LORA + KV_LORA], slab_cols], axis=1).astype(BF16)
    wuq = jnp.pad(w_uq.reshape(Q_LORA, MLA_HEADS, QK_NOPE + QK_ROPE), ((0, 0), (0, 0), (0, pad_tail)))
    wuq = wuq.reshape(Q_LORA, MLA_HEADS * HEAD_SLAB).astype(BF16)
    wkv = w_ukv.reshape(KV_LORA, MLA_HEADS, QK_NOPE + V_HEAD)
    wk = jnp.pad(wkv[:, :, :QK_NOPE], ((0, 0), (0, 0), (0, HEAD_SLAB - QK_NOPE)))
    wk = wk.reshape(KV_LORA, MLA_HEADS * HEAD_SLAB).astype(BF16)
    wv = wkv[:, :, QK_NOPE:].reshape(KV_LORA, MLA_HEADS * V_HEAD).astype(BF16)
    return win, wuq, wk, wv


def kernel(x_prompt, x_sample, cache_na_k, cache_na_v, cache_mla_ckv, cache_mla_krope, c, c_ctx, w_ada, b_ada, ln_mix_g, ln_mix_b, ln_ffn_g, ln_ffn_b, w_in_ab, conv_w, na_rpb, w_out_ab, w_in_mla, q_norm_g, w_uq, kv_norm_g, w_ukv, w_out_mla, peer_w_q, peer_subkeys, peer_u, peer_v):
    nb_p, seq_p, d = x_prompt.shape
    nb_s, seq_s, _ = x_sample.shape
    past = cache_na_k.shape[2]
    rows = seq_s // GRID_W
    n_p = nb_p * seq_p
    xp = x_prompt.reshape(n_p, d)
    xs = x_sample.reshape(nb_s * seq_s, d)

    n_cond = 1 + nb_s
    cond_rows = -(-n_cond // 8) * 8
    cvec = jnp.concatenate([c_ctx[None], c, jnp.zeros((cond_rows - n_cond, d), F32)], axis=0)

    outs = {}
    for i in range(DEPTH):
        j = i // 2
        mods = ada_modulation(cvec, w_ada[i], b_ada[i])
        mod_p = _pad_mod(mods[0:1], 1)
        mod_s = _pad_mod(mods[1:n_cond], nb_s)
        lg = ln_mix_g[i].reshape(1, d)
        lb = ln_mix_b[i].reshape(1, d)
        if i % 2 == 0:
            w_in = w_in_ab[j].astype(BF16)
            w_out = w_out_ab[j].astype(BF16)
            cw = jnp.pad(conv_w[j], ((0, 8 - CONV_TAPS), (0, 0)))
            qkv_p, gu_p, kv_p = inproj_ab(xp, mod_p, w_in, n_p, True)
            qkv_s, gu_s = inproj_ab(xs, mod_s, w_in, seq_s, False)
            outs["na_k"] = kv_p[:, :NA_WIDTH].reshape(nb_p, 1, seq_p, NA_HEADS, NA_HEAD_DIM)
            outs["na_v"] = kv_p[:, NA_WIDTH:].reshape(nb_p, 1, seq_p, NA_HEADS, NA_HEAD_DIM)
            att_p = ctx_attention(qkv_p, seq_p)
            kc = cache_na_k[:, j].reshape(nb_s * past, NA_WIDTH).astype(BF16)
            vc = cache_na_v[:, j].reshape(nb_s * past, NA_WIDTH).astype(BF16)
            att_s = na_attention(qkv_s, kc, vc, na_bias_table(na_rpb[j]), nb_s, rows)
            xp = outproj_ab(att_p, gu_p, xp, mod_p, w_out, cw, lg, lb, n_p, seq_p)
            xs = outproj_ab(att_s, gu_s, xs, mod_s, w_out, cw, lg, lb, seq_s, seq_s)
        else:
            win, wuq, wk, wv = _mla_weights(w_in_mla[j], w_uq[j], w_ukv[j])
            w_out = w_out_mla[j].astype(BF16)
            qg = q_norm_g[j].reshape(1, Q_LORA)
            kvg = kv_norm_g[j].reshape(1, KV_LORA)
            q_p, ckv_p, slab_p = mla_inproj(xp, mod_p, win, qg, kvg, wuq, None, n_p, seq_p)
            q_s, ckv_s, slab_s = mla_inproj(xs, mod_s, win, qg, kvg, wuq, rope_lane_tables(seq_s), seq_s, seq_s)
            outs["ckv"] = ckv_p.reshape(nb_p, 1, seq_p, KV_LORA)
            outs["kr"] = slab_p[:, QK_NOPE:QK_NOPE + QK_ROPE].reshape(nb_p, 1, seq_p, QK_ROPE)
            k_p, v_p = mla_kv(ckv_p, slab_p, wk, wv)
            k_s, v_s = mla_kv(ckv_s, slab_s, wk, wv)
            ckv_c = cache_mla_ckv[:, j].reshape(nb_s * past, KV_LORA)
            slab_c = jnp.pad(cache_mla_krope[:, j].reshape(nb_s * past, QK_ROPE),
                             ((0, 0), (QK_NOPE, HEAD_SLAB - QK_NOPE - QK_ROPE)))
            k_c, v_c = mla_kv(ckv_c, slab_c, wk, wv)
            att_p = mla_attention(q_p, k_p, v_p, None, None, nb_p)
            att_s = mla_attention(q_s, k_s, v_s, k_c, v_c, nb_s)
            xp = outproj(att_p, xp, mod_p, w_out, lg, lb, n_p)
            xs = outproj(att_s, xs, mod_s, w_out, lg, lb, seq_s)
        wq = peer_w_q[i].astype(BF16)
        sk = peer_subkeys[i].reshape(2 * PEER_HEADS, N_KEYS, PEER_HALF).astype(BF16)
        u = peer_u[i].astype(BF16)
        vt = peer_v[i].T.astype(BF16)
        fg = ln_ffn_g[i].reshape(1, d)
        fb = ln_ffn_b[i].reshape(1, d)
        xp = peer_block(xp, mod_p, wq, sk, u, vt, fg, fb, n_p)
        xs = peer_block(xs, mod_s, wq, sk, u, vt, fg, fb, seq_s)
    return (xp.reshape(nb_p, seq_p, d), xs.reshape(nb_s, seq_s, d),
            outs["na_k"], outs["na_v"], outs["ckv"], outs["kr"])
```

```python
import functools

import jax
import jax.numpy as jnp
import numpy as np
from jax import lax
from jax.experimental import pallas as pl
from jax.experimental.pallas import tpu as pltpu

F32 = jnp.float32
BF16 = jnp.bfloat16

D_MODEL = 1024
DEPTH = 2
GRID_W = 64
N_MOD = 6
ALPHA = (2.0 * DEPTH) ** 0.25
LN_EPS = 1e-5
RMS_EPS = 1e-6
NA_HEADS = 8
NA_HEAD_DIM = 64
NA_WIDTH = NA_HEADS * NA_HEAD_DIM
NA_MAX_ROWS = 8
NA_WIN_COLS = 16
NA_SCALE = NA_HEAD_DIM ** -0.5
CONV_WIDTH = 512
CONV_TAPS = 3
MLA_HEADS = 16
Q_LORA = 256
KV_LORA = 256
QK_NOPE = 64
QK_ROPE = 32
V_HEAD = 64
MLA_SCALE = (QK_NOPE + QK_ROPE) ** -0.5
ROPE_THETA = 10000.0
ROPE_PAIRS_AXIS = QK_ROPE // 4
PEER_HEADS = 8
PEER_HALF = 128
N_KEYS = 128
N_EXPERTS = N_KEYS * N_KEYS
PEER_TOPK = 16

LANES = 128
MOD_ROWS = 8
HEAD_SLAB = 128
MASK_BIAS = -1e30
VMEM_LIMIT = 52 * 1024 * 1024


def _cparams(sem, flags=None):
    return pltpu.CompilerParams(dimension_semantics=sem, vmem_limit_bytes=VMEM_LIMIT, flags=flags)


def _nt_dot(a, b):
    return lax.dot_general(a, b, (((1,), (1,)), ((), ())), preferred_element_type=F32)


def _dot(a, b):
    return jnp.dot(a, b, preferred_element_type=F32)


def _split_bf16(a):
    hi = a.astype(BF16)
    lo = (a - hi.astype(F32)).astype(BF16)
    return hi, lo


def _layer_norm(z, g, b):
    mu = jnp.mean(z, axis=-1, keepdims=True)
    zc = z - mu
    var = jnp.mean(zc * zc, axis=-1, keepdims=True)
    return zc * lax.rsqrt(var + LN_EPS) * g + b


LOG2E = 1.4426950408889634
LN2 = 0.6931471805599453
_GELU_A = -2.0 * LOG2E * 0.7978845608028654
_GELU_B = _GELU_A * 0.044715


def _gelu_tanh(x):
    return x / (1.0 + jnp.exp2(x * (_GELU_A + _GELU_B * (x * x))))


def _ada_kernel(c_ref, w_ref, b_ref, o_ref):
    c = c_ref[...]
    a = c * (1.0 / (1.0 + jnp.exp(-c)))
    a_hi, a_lo = _split_bf16(a)
    w_hi, w_lo = _split_bf16(w_ref[...])
    o_ref[...] = _dot(a_hi, w_hi) + _dot(a_hi, w_lo) + _dot(a_lo, w_hi) + b_ref[...]


def ada_modulation(cvec, w_ada, b_ada, tn=512):
    r, d = cvec.shape
    n = w_ada.shape[1]
    return pl.pallas_call(
        _ada_kernel,
        grid=(n // tn,),
        in_specs=[pl.BlockSpec((r, d), lambda j: (0, 0)),
                  pl.BlockSpec((d, tn), lambda j: (0, j)),
                  pl.BlockSpec((1, tn), lambda j: (0, j))],
        out_specs=pl.BlockSpec((r, tn), lambda j: (0, j)),
        out_shape=jax.ShapeDtypeStruct((r, n), F32),
        compiler_params=_cparams(("arbitrary",)),
        name="ada_modulation",
    )(cvec, w_ada, b_ada.reshape(1, n))


def _inproj_ab_kernel(x_ref, mod_ref, w_ref, qkv_ref, gu_ref, *kv_ref):
    m = mod_ref[0]
    h = x_ref[...] * (1.0 + m[1:2]) + m[0:1]
    y = _dot(h.astype(BF16), w_ref[...])
    w3 = 3 * NA_WIDTH
    qkv_ref[:, :NA_WIDTH] = (y[:, :NA_WIDTH] * (NA_SCALE * LOG2E)).astype(BF16)
    qkv_ref[:, NA_WIDTH:] = y[:, NA_WIDTH:w3].astype(BF16)
    if kv_ref:
        kv_ref[0][...] = y[:, NA_WIDTH:w3]
    gu_ref[:, :CONV_WIDTH] = y[:, w3:w3 + CONV_WIDTH].astype(BF16)
    gu_ref[:, CONV_WIDTH:] = (y[:, w3 + CONV_WIDTH:w3 + 2 * CONV_WIDTH]
                              * y[:, w3 + 2 * CONV_WIDTH:]).astype(BF16)


def inproj_ab(x, mod, w_bf16, tokens_per_group, emit_kv, tm=512):
    n, d = x.shape
    tm = min(tm, n)
    out_shape = [jax.ShapeDtypeStruct((n, 3 * NA_WIDTH), BF16),
                 jax.ShapeDtypeStruct((n, 2 * CONV_WIDTH), BF16)]
    out_specs = [pl.BlockSpec((tm, 3 * NA_WIDTH), lambda i: (i, 0)),
                 pl.BlockSpec((tm, 2 * CONV_WIDTH), lambda i: (i, 0))]
    if emit_kv:
        out_shape.append(jax.ShapeDtypeStruct((n, 2 * NA_WIDTH), F32))
        out_specs.append(pl.BlockSpec((tm, 2 * NA_WIDTH), lambda i: (i, 0)))
    return pl.pallas_call(
        _inproj_ab_kernel,
        grid=(n // tm,),
        in_specs=[pl.BlockSpec((tm, d), lambda i: (i, 0)),
                  pl.BlockSpec((1, MOD_ROWS, d), lambda i: ((i * tm) // tokens_per_group, 0, 0)),
                  pl.BlockSpec(w_bf16.shape, lambda i: (0, 0))],
        out_specs=out_specs,
        out_shape=out_shape,
        compiler_params=_cparams(("parallel",)),
        name="inproj_ab",
    )(x, mod, w_bf16)


def _pair_attention(instances):
    heads = []
    for n, (q_pair, _, _, _) in enumerate(instances):
        lane = lax.broadcasted_iota(jnp.int32, q_pair.shape, 1)
        qf = q_pair.astype(F32)
        for j in range(2):
            in_head = (lane >= j * NA_HEAD_DIM) & (lane < (j + 1) * NA_HEAD_DIM)
            heads.append((n, j, jnp.where(in_head, qf, 0.0).astype(BF16)))
    scores = []
    for n, j, qm in heads:
        _, keys, _, biases = instances[n]
        ss = [_nt_dot(qm, k) for k in keys]
        scores.append([s if b is None else s + b for s, b in zip(ss, biases[j])])
    mxs = [functools.reduce(jnp.maximum, [jnp.max(s, axis=-1, keepdims=True) for s in ss]) for ss in scores]
    ess = [[jnp.exp2(s - mx) for s in ss] for ss, mx in zip(scores, mxs)]
    dens = [functools.reduce(jnp.add, [jnp.sum(e, axis=-1, keepdims=True) for e in es]) for es in ess]
    outs = []
    for (n, j, _), es in zip(heads, ess):
        values = instances[n][2]
        outs.append(functools.reduce(jnp.add, [_dot(e.astype(BF16), v) for e, v in zip(es, values)]))
    outs = [o / d for o, d in zip(outs, dens)]
    lane = lax.broadcasted_iota(jnp.int32, outs[0].shape, 1)
    return [jnp.where(lane < NA_HEAD_DIM, outs[2 * n], outs[2 * n + 1]) for n in range(len(instances))]


NA_ROWS_PER_STEP = 2


def _na_attn_kernel(q_ref, k_ref, v_ref, kc_ref, vc_ref, *rest, rows):
    bias_refs = rest[:NA_ROWS_PER_STEP]
    o_ref = rest[NA_ROWS_PER_STEP]
    band = NA_MAX_ROWS * GRID_W
    instances, places = [], []
    for rr in range(NA_ROWS_PER_STEP):
        r = pl.program_id(1) * NA_ROWS_PER_STEP + rr
        rs = jnp.clip(r - NA_MAX_ROWS // 2, 0, rows - NA_MAX_ROWS)
        start = pl.multiple_of(rs * GRID_W, GRID_W)
        qrows = slice(rr * GRID_W, (rr + 1) * GRID_W)
        for p in range(NA_HEADS // 2):
            sl = slice(p * LANES, (p + 1) * LANES)
            keys = [k_ref[pl.ds(start, band), sl], kc_ref[:, sl]]
            values = [v_ref[pl.ds(start, band), sl], vc_ref[:, sl]]
            biases = [[bias_refs[rr][0, 2 * p + j], None] for j in range(2)]
            instances.append((q_ref[qrows, sl], keys, values, biases))
            places.append((qrows, sl))
    for (qrows, sl), o in zip(places, _pair_attention(instances)):
        o_ref[qrows, sl] = o.astype(BF16)


def na_attention(qkv, kc, vc, bias, n_batch, rows):
    n = qkv.shape[0]
    seq = rows * GRID_W
    lc = kc.shape[0] // n_batch
    half = NA_MAX_ROWS // 2

    steps = rows // NA_ROWS_PER_STEP
    tq = NA_ROWS_PER_STEP * GRID_W

    def bias_spec(rr):
        def idx(b, g):
            r = g * NA_ROWS_PER_STEP + rr
            return (r - jnp.clip(r - half, 0, rows - NA_MAX_ROWS), 0, 0, 0)
        return pl.BlockSpec((1, NA_HEADS, GRID_W, NA_MAX_ROWS * GRID_W), idx)

    return pl.pallas_call(
        functools.partial(_na_attn_kernel, rows=rows),
        grid=(n_batch, steps),
        in_specs=[pl.BlockSpec((tq, NA_WIDTH), lambda b, g: (b * steps + g, 0)),
                  pl.BlockSpec((seq, NA_WIDTH), lambda b, g: (b, 1)),
                  pl.BlockSpec((seq, NA_WIDTH), lambda b, g: (b, 2)),
                  pl.BlockSpec((lc, NA_WIDTH), lambda b, g: (b, 0)),
                  pl.BlockSpec((lc, NA_WIDTH), lambda b, g: (b, 0))]
        + [bias_spec(rr) for rr in range(NA_ROWS_PER_STEP)],
        out_specs=pl.BlockSpec((tq, NA_WIDTH), lambda b, g: (b * steps + g, 0)),
        out_shape=jax.ShapeDtypeStruct((n, NA_WIDTH), BF16),
        compiler_params=_cparams(("parallel", "arbitrary")),
        name="na_attention",
    )(qkv, qkv, qkv, kc, vc, *([bias] * NA_ROWS_PER_STEP))


def na_bias_table(rpb):
    c = np.arange(GRID_W)[:, None]
    kc = np.arange(GRID_W)[None, :]
    cs = np.clip(c - NA_WIN_COLS // 2, 0, GRID_W - NA_WIN_COLS)
    valid = (kc >= cs) & (kc < cs + NA_WIN_COLS)
    dc = kc - c + (NA_WIN_COLS - 1)
    cc, kk = np.nonzero(valid)
    place = np.zeros((2 * NA_WIN_COLS - 1, GRID_W, GRID_W), np.float32)
    place[dc[cc, kk], cc, kk] = 1.0
    t = jnp.einsum("hrd,dck->hrck", rpb, place, precision=lax.Precision.HIGHEST)
    t = jnp.where(valid[None, None], t * LOG2E, MASK_BIAS)
    bands = [t[:, NA_MAX_ROWS - 1 - off:2 * NA_MAX_ROWS - 1 - off] for off in range(NA_MAX_ROWS)]
    b = jnp.stack(bands, axis=0)
    b = jnp.transpose(b, (0, 1, 3, 2, 4))
    return b.reshape(NA_MAX_ROWS, NA_HEADS, GRID_W, NA_MAX_ROWS * GRID_W).astype(F32)


def _ctx_attn_kernel(q_ref, k_ref, v_ref, o_ref):
    slabs = [slice(p * LANES, (p + 1) * LANES) for p in range(NA_HEADS // 2)]
    instances = [(q_ref[:, sl], [k_ref[:, sl]], [v_ref[:, sl]], [[None], [None]]) for sl in slabs]
    for sl, o in zip(slabs, _pair_attention(instances)):
        o_ref[:, sl] = o.astype(BF16)


def ctx_attention(qkv, seq):
    n = qkv.shape[0]
    return pl.pallas_call(
        _ctx_attn_kernel,
        grid=(n // seq,),
        in_specs=[pl.BlockSpec((seq, NA_WIDTH), lambda b: (b, 0)),
                  pl.BlockSpec((seq, NA_WIDTH), lambda b: (b, 1)),
                  pl.BlockSpec((seq, NA_WIDTH), lambda b: (b, 2))],
        out_specs=pl.BlockSpec((seq, NA_WIDTH), lambda b: (b, 0)),
        out_shape=jax.ShapeDtypeStruct((n, NA_WIDTH), BF16),
        compiler_params=_cparams(("parallel",)),
        name="ctx_attention",
    )(qkv, qkv, qkv)


HALO = 16


def _outproj_ab_kernel(att_ref, gu_ref, prev_ref, next_ref, x_ref, mod_ref, w_ref, cw_ref, g_ref, b_ref,
                       o_ref, *, tiles_per_seq):
    i = pl.program_id(0)
    tm = x_ref.shape[0]
    m = mod_ref[0]
    gu = gu_ref[...].astype(F32)
    bg = gu[:, :CONV_WIDTH]
    u = gu[:, CONV_WIDTH:]
    has_prev = (i % tiles_per_seq != 0).astype(F32)
    has_next = (i % tiles_per_seq != tiles_per_seq - 1).astype(F32)
    prev_row = prev_ref[HALO - 1:HALO, CONV_WIDTH:].astype(F32) * has_prev
    next_row = next_ref[0:1, CONV_WIDTH:].astype(F32) * has_next
    row = lax.broadcasted_iota(jnp.int32, u.shape, 0)
    u_m1 = jnp.where(row == 0, prev_row, pltpu.roll(u, 1, axis=0))
    u_p1 = jnp.where(row == tm - 1, next_row, pltpu.roll(u, tm - 1, axis=0))
    cw = cw_ref[...]
    conv = bg * (u_m1 * cw[0:1] + u * cw[1:2] + u_p1 * cw[2:3])
    y = _dot(att_ref[...], w_ref[:NA_WIDTH, :]) + _dot(conv.astype(BF16), w_ref[NA_WIDTH:, :])
    z = ALPHA * x_ref[...] + m[2:3] * y
    o_ref[...] = _layer_norm(z, g_ref[...], b_ref[...])


def outproj_ab(att, gu, x, mod, w_bf16, conv_w8, ln_g, ln_b, tokens_per_group, seq, tm=256):
    n, d = x.shape
    tm = min(tm, seq)
    hb = tm // HALO
    last_halo = n // HALO - 1
    return pl.pallas_call(
        functools.partial(_outproj_ab_kernel, tiles_per_seq=seq // tm),
        grid=(n // tm,),
        in_specs=[pl.BlockSpec((tm, NA_WIDTH), lambda i: (i, 0)),
                  pl.BlockSpec((tm, 2 * CONV_WIDTH), lambda i: (i, 0)),
                  pl.BlockSpec((HALO, 2 * CONV_WIDTH), lambda i: (jnp.maximum(i * hb - 1, 0), 0)),
                  pl.BlockSpec((HALO, 2 * CONV_WIDTH), lambda i: (jnp.minimum((i + 1) * hb, last_halo), 0)),
                  pl.BlockSpec((tm, d), lambda i: (i, 0)),
                  pl.BlockSpec((1, MOD_ROWS, d), lambda i: ((i * tm) // tokens_per_group, 0, 0)),
                  pl.BlockSpec(w_bf16.shape, lambda i: (0, 0)),
                  pl.BlockSpec(conv_w8.shape, lambda i: (0, 0)),
                  pl.BlockSpec((1, d), lambda i: (0, 0)),
                  pl.BlockSpec((1, d), lambda i: (0, 0))],
        out_specs=pl.BlockSpec((tm, d), lambda i: (i, 0)),
        out_shape=jax.ShapeDtypeStruct((n, d), F32),
        compiler_params=_cparams(("parallel",)),
        name="outproj_ab",
    )(att, gu, gu, gu, x, mod, w_bf16, conv_w8, ln_g, ln_b)


def _outproj_kernel(att_ref, x_ref, mod_ref, w_ref, g_ref, b_ref, o_ref):
    m = mod_ref[0]
    y = _dot(att_ref[...], w_ref[...])
    z = ALPHA * x_ref[...] + m[2:3] * y
    o_ref[...] = _layer_norm(z, g_ref[...], b_ref[...])


def outproj(att, x, mod, w_bf16, ln_g, ln_b, tokens_per_group, tm=512):
    n, d = x.shape
    k = att.shape[1]
    return pl.pallas_call(
        _outproj_kernel,
        grid=(n // tm,),
        in_specs=[pl.BlockSpec((tm, k), lambda i: (i, 0)),
                  pl.BlockSpec((tm, d), lambda i: (i, 0)),
                  pl.BlockSpec((1, MOD_ROWS, d), lambda i: ((i * tm) // tokens_per_group, 0, 0)),
                  pl.BlockSpec(w_bf16.shape, lambda i: (0, 0)),
                  pl.BlockSpec((1, d), lambda i: (0, 0)),
                  pl.BlockSpec((1, d), lambda i: (0, 0))],
        out_specs=pl.BlockSpec((tm, d), lambda i: (i, 0)),
        out_shape=jax.ShapeDtypeStruct((n, d), F32),
        compiler_params=_cparams(("parallel",)),
        name="outproj",
    )(att, x, mod, w_bf16, ln_g, ln_b)


def _peer_scores_kernel(x_ref, mod_ref, wq_ref, sk_ref, hm_ref, st_ref):
    m = mod_ref[0]
    h = x_ref[...] * (1.0 + m[4:5]) + m[3:4]
    hb = h.astype(BF16)
    hm_ref[...] = h.T.astype(BF16)
    q = _dot(hb, wq_ref[...])
    for c in range(2 * PEER_HEADS):
        qc = q[:, c * PEER_HALF:(c + 1) * PEER_HALF].astype(BF16)
        st_ref[c] = _nt_dot(sk_ref[c], qc) * LOG2E


def peer_scores(x, mod, wq_bf16, subkeys_bf16, tokens_per_group, tm=512):
    n, d = x.shape
    nc = 2 * PEER_HEADS
    return pl.pallas_call(
        _peer_scores_kernel,
        grid=(n // tm,),
        in_specs=[pl.BlockSpec((tm, d), lambda i: (i, 0)),
                  pl.BlockSpec((1, MOD_ROWS, d), lambda i: ((i * tm) // tokens_per_group, 0, 0)),
                  pl.BlockSpec(wq_bf16.shape, lambda i: (0, 0)),
                  pl.BlockSpec(subkeys_bf16.shape, lambda i: (0, 0, 0))],
        out_specs=[pl.BlockSpec((d, tm), lambda i: (0, i)),
                   pl.BlockSpec((nc, N_KEYS, tm), lambda i: (0, 0, i))],
        out_shape=[jax.ShapeDtypeStruct((d, n), BF16),
                   jax.ShapeDtypeStruct((nc, N_KEYS, n), F32)],
        compiler_params=_cparams(("parallel",)),
        name="peer_scores",
    )(x, mod, wq_bf16, subkeys_bf16)


SUBLANES = 8
N_SLABS = N_KEYS // SUBLANES


def _sort_network(n):
    pairs = []
    p = 1
    while p < n:
        k = p
        while k >= 1:
            for j in range(k % p, n - k, 2 * k):
                for i in range(min(k, n - j - k)):
                    if (i + j) // (2 * p) == (i + j + k) // (2 * p):
                        pairs.append((i + j, i + j + k))
            k //= 2
        p *= 2
    return pairs


_SLAB_SORT = _sort_network(N_SLABS)


def _top_values(s_ref, c, n_top):
    return _top_of_slabs([s_ref[c, r] for r in range(N_SLABS)], n_top)


def _top_of_slabs(slabs, n_top):
    x = list(slabs)
    n = len(x)
    for a, b in _SLAB_SORT:
        if b < n:
            x[a], x[b] = jnp.maximum(x[a], x[b]), jnp.minimum(x[a], x[b])
    sub = lax.broadcasted_iota(jnp.int32, x[0].shape, 0).astype(F32)
    x.append(jnp.full(x[0].shape, -jnp.inf, F32))
    vals = []
    for k in range(n_top):
        mx = jnp.max(x[0], axis=0, keepdims=True)
        vals.append(mx)
        if k == n_top - 1:
            break
        first = jnp.min(jnp.where(x[0] == mx, sub, float(SUBLANES)), axis=0, keepdims=True)
        pop = sub == first
        for r in range(min(n, n_top - 1 - k)):
            x[r] = jnp.where(pop, x[r + 1], x[r])
    return vals


def _rows_to_slab(rows_list):
    t = rows_list[0].shape[1]
    ridx = lax.broadcasted_iota(jnp.int32, (SUBLANES, t), 0)
    out = jnp.full((SUBLANES, t), -jnp.inf, F32)
    for r, row in enumerate(rows_list):
        out = jnp.where(ridx == r, row, out)
    return out


def _peer_select_kernel(st_ref, thr_ref, lse_ref, m2_ref):
    k = PEER_TOPK
    for h in range(PEER_HEADS):
        v1 = _top_values(st_ref, 2 * h, k + 1)
        v2 = _top_values(st_ref, 2 * h + 1, k + 1)
        v1_lo, v1_hi = _rows_to_slab(v1[0:8]), _rows_to_slab(v1[8:16])
        v2_lo, v2_hi = _rows_to_slab(v2[0:8]), _rows_to_slab(v2[8:16])
        ridx = lax.broadcasted_iota(jnp.int32, v1_lo.shape, 0)
        v1_mid = jnp.where(ridx >= 4, v1_lo, -jnp.inf)
        cands = [v1[0] + v2_lo, v1[0] + v2_hi, v1[1] + v2_lo, v1[2] + v2_lo, v1[3] + v2_lo,
                 v1_hi + v2[0], v1_mid + v2[0], v1_mid + v2[1], v1_mid + v2[2],
                 _rows_to_slab([v1[k] + v2[0], v1[0] + v2[k]])]
        top = v1[0] + v2[0]
        best = _top_of_slabs(cands, k + 1)
        thr, nxt = best[k - 1], best[k]
        z = functools.reduce(jnp.add, [jnp.where(cd >= thr, jnp.exp2(cd - top), 0.0) for cd in cands])
        z = jnp.sum(z, axis=0, keepdims=True)
        thr_ref[h:h + 1, :] = 0.5 * (thr + nxt)
        lse_ref[h:h + 1, :] = v1[0] + jnp.log(z) * LOG2E
        m2_ref[h:h + 1, :] = v2[0]


def peer_select(st, tl=256):
    nc, _, _, n = st.shape
    stat = pl.BlockSpec((PEER_HEADS, tl), lambda i: (0, i))
    return pl.pallas_call(
        _peer_select_kernel,
        grid=(n // tl,),
        in_specs=[pl.BlockSpec((nc, N_SLABS, SUBLANES, tl), lambda i: (0, 0, 0, i))],
        out_specs=[stat, stat, stat],
        out_shape=[jax.ShapeDtypeStruct((PEER_HEADS, n), F32)] * 3,
        compiler_params=_cparams(("parallel",)),
        name="peer_select",
    )(st)


PEER_LANE_CHUNK = 128
PEER_MXU_CHUNK = 256
PEER_MXU_ROWS = 256


def _peer_dense_kernel(hm_ref, u_ref, vt_ref, st_ref, thr_ref, lse_ref, m2_ref, x_ref, mod_ref, g_ref, b_ref,
                       o_ref, acc_ref, ht_ref, at_ref, e2_ref, cut_rep, f1_rep, *, gate_row):
    e = pl.program_id(1)
    tm = hm_ref.shape[1]
    n_mxu_chunks = tm // PEER_MXU_CHUNK
    row_blocks = u_ref.shape[0] // PEER_MXU_ROWS

    @pl.when(e == 0)
    def _():
        acc_ref[...] = jnp.zeros_like(acc_ref)
        for h in range(PEER_HEADS):
            e2_ref[h] = jnp.exp2(st_ref[2 * h + 1] - m2_ref[h:h + 1, :])

    def first_matmul(cm, rb):
        ms = slice(cm * PEER_MXU_CHUNK, (cm + 1) * PEER_MXU_CHUNK)
        rs = slice(rb * PEER_MXU_ROWS, (rb + 1) * PEER_MXU_ROWS)
        ht_ref[rs, ms] = _dot(u_ref[rs, :], hm_ref[:, ms])


    for h in range(PEER_HEADS):
        s1 = st_ref[2 * h, e]
        cut_rows = thr_ref[h:h + 1, :] - s1
        f1_rows = jnp.exp2(s1 - lse_ref[h:h + 1, :])
        for ii in range(SUBLANES):
            cut_rep[h * SUBLANES + ii] = jnp.broadcast_to(cut_rows[ii:ii + 1, :], (SUBLANES, tm))
            f1_rep[h * SUBLANES + ii] = jnp.broadcast_to(f1_rows[ii:ii + 1, :], (SUBLANES, tm))
    def second_matmul(cm, kb):
        ms = slice(cm * PEER_MXU_CHUNK, (cm + 1) * PEER_MXU_CHUNK)
        ks = slice(kb * (te // 2), (kb + 1) * (te // 2))
        acc_ref[:, ms] += _dot(vt_ref[:, ks], at_ref[ks, ms])

    lanes_per_mxu = PEER_MXU_CHUNK // PEER_LANE_CHUNK
    te = u_ref.shape[0]
    for cm in range(n_mxu_chunks):
        units = [(ii, c) for ii in range(SUBLANES) for c in range(cm * lanes_per_mxu, (cm + 1) * lanes_per_mxu)]
        per_block = len(units) // row_blocks
        pieces = [functools.partial(first_matmul, cm, rb) for rb in range(row_blocks)] if cm == 0 else []
        if cm + 1 < n_mxu_chunks:
            pieces += [functools.partial(first_matmul, cm + 1, rb) for rb in range(row_blocks)]
        lead = per_block // 2 if cm == 0 else per_block
        for idx, (ii, c) in enumerate(units):
            if idx % lead == 0 and idx // lead < len(pieces):
                pieces[idx // lead]()
            if idx == len(units) // 2:
                second_matmul(cm, 0)
            rows = slice(ii * N_KEYS, (ii + 1) * N_KEYS)
            ls = slice(c * PEER_LANE_CHUNK, (c + 1) * PEER_LANE_CHUNK)
            w = jnp.zeros((N_SLABS, SUBLANES, PEER_LANE_CHUNK), F32)
            for h in range(PEER_HEADS):
                sel = st_ref[2 * h + 1, :, :, ls] >= cut_rep[h * SUBLANES + ii, :, ls][None]
                w = w + jnp.where(sel, e2_ref[h, :, :, ls] * f1_rep[h * SUBLANES + ii, :, ls][None], 0.0)
            act = _gelu_tanh(ht_ref[rows, ls])
            at_ref[rows, ls] = (act * w.reshape(N_KEYS, PEER_LANE_CHUNK)).astype(BF16)
        second_matmul(cm, 1)

    @pl.when(e == pl.num_programs(1) - 1)
    def _():
        m = mod_ref[0]
        z = ALPHA * x_ref[...] + m[gate_row:gate_row + 1] * acc_ref[...].T
        o_ref[...] = _layer_norm(z, g_ref[...], b_ref[...])


def peer_dense(hm, u_bf16, vt_bf16, st, thr, lse, m2, x, mod, ln_g, ln_b, tokens_per_group, tm=512, te=1024):
    n, d = x.shape
    ne = u_bf16.shape[0]
    nc = st.shape[0]
    assert te == N_KEYS * SUBLANES
    return pl.pallas_call(
        functools.partial(_peer_dense_kernel, gate_row=5),
        grid=(n // tm, ne // te),
        in_specs=[pl.BlockSpec((d, tm), lambda i, g: (0, i)),
                  pl.BlockSpec((te, d), lambda i, g: (g, 0)),
                  pl.BlockSpec((d, te), lambda i, g: (0, g)),
                  pl.BlockSpec((nc, N_SLABS, SUBLANES, tm), lambda i, g: (0, 0, 0, i)),
                  pl.BlockSpec((PEER_HEADS, tm), lambda i, g: (0, i)),
                  pl.BlockSpec((PEER_HEADS, tm), lambda i, g: (0, i)),
                  pl.BlockSpec((PEER_HEADS, tm), lambda i, g: (0, i)),
                  pl.BlockSpec((tm, d), lambda i, g: (i, 0)),
                  pl.BlockSpec((1, MOD_ROWS, d), lambda i, g: ((i * tm) // tokens_per_group, 0, 0)),
                  pl.BlockSpec((1, d), lambda i, g: (0, 0)),
                  pl.BlockSpec((1, d), lambda i, g: (0, 0))],
        out_specs=pl.BlockSpec((tm, d), lambda i, g: (i, 0)),
        out_shape=jax.ShapeDtypeStruct((n, d), F32),
        scratch_shapes=[pltpu.VMEM((d, tm), F32),
                        pltpu.VMEM((te, tm), F32),
                        pltpu.VMEM((te, tm), BF16),
                        pltpu.VMEM((PEER_HEADS, N_SLABS, SUBLANES, tm), F32),
                        pltpu.VMEM((PEER_HEADS * SUBLANES, SUBLANES, tm), F32),
                        pltpu.VMEM((PEER_HEADS * SUBLANES, SUBLANES, tm), F32)],
        compiler_params=_cparams(("parallel", "arbitrary")),
        name="peer_dense",
    )(hm, u_bf16, vt_bf16, st, thr, lse, m2, x, mod, ln_g, ln_b)


def peer_block(x, mod, wq, sk, u, vt, ln_g, ln_b, tokens_per_group):
    hm, st = peer_scores(x, mod, wq, sk, tokens_per_group)
    st = st.reshape(2 * PEER_HEADS, N_SLABS, SUBLANES, x.shape[0])
    cut, lse, m2 = peer_select(st)
    return peer_dense(hm, u, vt, st, cut, lse, m2, x, mod, ln_g, ln_b, tokens_per_group)


def _rope_slab(t, cos_t, sin_fwd, sin_bwd):
    return t * cos_t + pltpu.roll(t, QK_ROPE // 2, axis=1) * sin_fwd \
        + pltpu.roll(t, HEAD_SLAB - QK_ROPE // 2, axis=1) * sin_bwd


def _mla_inproj_kernel(x_ref, mod_ref, win_ref, qg_ref, kvg_ref, wuq_ref, *rest, use_rope):
    if use_rope:
        cos_ref, sf_ref, sb_ref, q_ref, ckv_ref, slab_ref = rest
    else:
        q_ref, ckv_ref, slab_ref = rest
    m = mod_ref[0]
    h = x_ref[...] * (1.0 + m[1:2]) + m[0:1]
    y = _dot(h.astype(BF16), win_ref[...])
    cq = y[:, :Q_LORA]
    ckv = y[:, Q_LORA:Q_LORA + KV_LORA]
    slab = y[:, Q_LORA + KV_LORA:]
    cqn = cq * lax.rsqrt(jnp.mean(cq * cq, axis=-1, keepdims=True) + RMS_EPS) * qg_ref[...]
    ckv_ref[...] = ckv * lax.rsqrt(jnp.mean(ckv * ckv, axis=-1, keepdims=True) + RMS_EPS) * kvg_ref[...]
    q = _dot(cqn.astype(BF16), wuq_ref[...]) * (MLA_SCALE * LOG2E)
    if use_rope:
        cos_t, sf, sb = cos_ref[...], sf_ref[...], sb_ref[...]
        slab = _rope_slab(slab, cos_t, sf, sb)
        for hh in range(MLA_HEADS):
            sl = slice(hh * HEAD_SLAB, (hh + 1) * HEAD_SLAB)
            q_ref[:, sl] = _rope_slab(q[:, sl], cos_t, sf, sb).astype(BF16)
    else:
        q_ref[...] = q.astype(BF16)
    slab_ref[...] = slab


def mla_inproj(x, mod, win_bf16, q_g, kv_g, wuq_bf16, rope, tokens_per_group, seq, tm=256):
    n, d = x.shape
    use_rope = rope is not None
    in_specs = [pl.BlockSpec((tm, d), lambda i: (i, 0)),
                pl.BlockSpec((1, MOD_ROWS, d), lambda i: ((i * tm) // tokens_per_group, 0, 0)),
                pl.BlockSpec(win_bf16.shape, lambda i: (0, 0)),
                pl.BlockSpec((1, Q_LORA), lambda i: (0, 0)),
                pl.BlockSpec((1, KV_LORA), lambda i: (0, 0)),
                pl.BlockSpec(wuq_bf16.shape, lambda i: (0, 0))]
    args = [x, mod, win_bf16, q_g, kv_g, wuq_bf16]
    if use_rope:
        tps = seq // tm
        in_specs += [pl.BlockSpec((tm, HEAD_SLAB), lambda i: (i % tps, 0))] * 3
        args += list(rope)
    return pl.pallas_call(
        functools.partial(_mla_inproj_kernel, use_rope=use_rope),
        grid=(n // tm,),
        in_specs=in_specs,
        out_specs=[pl.BlockSpec((tm, MLA_HEADS * HEAD_SLAB), lambda i: (i, 0)),
                   pl.BlockSpec((tm, KV_LORA), lambda i: (i, 0)),
                   pl.BlockSpec((tm, HEAD_SLAB), lambda i: (i, 0))],
        out_shape=[jax.ShapeDtypeStruct((n, MLA_HEADS * HEAD_SLAB), BF16),
                   jax.ShapeDtypeStruct((n, KV_LORA), F32),
                   jax.ShapeDtypeStruct((n, HEAD_SLAB), F32)],
        compiler_params=_cparams(("parallel",)),
        name="mla_inproj",
    )(*args)


def _mla_kv_kernel(ckv_ref, slab_ref, wk_ref, wv_ref, k_ref, v_ref):
    c = ckv_ref[...].astype(BF16)
    kn = _dot(c, wk_ref[...])
    slab = slab_ref[...]
    for hh in range(MLA_HEADS):
        sl = slice(hh * HEAD_SLAB, (hh + 1) * HEAD_SLAB)
        k_ref[:, sl] = (kn[:, sl] + slab).astype(BF16)
    v = _dot(c, wv_ref[...])
    lane = lax.broadcasted_iota(jnp.int32, (v.shape[0], LANES), 1)
    for p in range(MLA_HEADS // 2):
        pair = v[:, p * LANES:(p + 1) * LANES]
        v_ref[:, (2 * p) * LANES:(2 * p + 1) * LANES] = jnp.where(lane < V_HEAD, pair, 1.0).astype(BF16)
        v_ref[:, (2 * p + 1) * LANES:(2 * p + 2) * LANES] = jnp.where(lane < V_HEAD, 1.0, pair).astype(BF16)


def mla_kv(ckv, slab, wk_bf16, wv_bf16, tm=256):
    n = ckv.shape[0]
    tm = min(tm, n)
    return pl.pallas_call(
        _mla_kv_kernel,
        grid=(n // tm,),
        in_specs=[pl.BlockSpec((tm, KV_LORA), lambda i: (i, 0)),
                  pl.BlockSpec((tm, HEAD_SLAB), lambda i: (i, 0)),
                  pl.BlockSpec(wk_bf16.shape, lambda i: (0, 0)),
                  pl.BlockSpec(wv_bf16.shape, lambda i: (0, 0))],
        out_specs=[pl.BlockSpec((tm, MLA_HEADS * HEAD_SLAB), lambda i: (i, 0)),
                   pl.BlockSpec((tm, MLA_HEADS * LANES), lambda i: (i, 0))],
        out_shape=[jax.ShapeDtypeStruct((n, MLA_HEADS * HEAD_SLAB), BF16),
                   jax.ShapeDtypeStruct((n, MLA_HEADS * LANES), BF16)],
        compiler_params=_cparams(("parallel",)),
        name="mla_kv",
    )(ckv, slab, wk_bf16, wv_bf16)


MLA_KEY_CHUNK = 256


def _mla_attn_kernel(q_ref, *rest, n_sources):
    k_refs = rest[:n_sources]
    v_refs = rest[n_sources:2 * n_sources]
    o_ref = rest[2 * n_sources]
    tq = q_ref.shape[0]
    lane = lax.broadcasted_iota(jnp.int32, (tq, LANES), 1)
    sls = [slice(j * HEAD_SLAB, (j + 1) * HEAD_SLAB) for j in range(2)]
    qs = [q_ref[:, sl] for sl in sls]
    chunks = [(si, st) for si, k in enumerate(k_refs) for st in range(0, k.shape[0], MLA_KEY_CHUNK)]

    def score(j, ch):
        si, st = ch
        return _nt_dot(qs[j], k_refs[si][st:st + MLA_KEY_CHUNK, sls[j]])

    def row_max(ss):
        return jnp.max(functools.reduce(jnp.maximum, ss), axis=-1, keepdims=True)

    def weighted(j, ch, p):
        si, st = ch
        return _dot(p, v_refs[si][st:st + MLA_KEY_CHUNK, sls[j]])

    s0 = [score(0, ch) for ch in chunks]
    m0 = row_max(s0)
    s1, p0 = [], []
    for c, ch in enumerate(chunks):
        s1.append(score(1, ch))
        p0.append(jnp.exp2(s0[c] - m0).astype(BF16))
    m1 = row_max(s1)
    o0, p1 = None, []
    for c, ch in enumerate(chunks):
        t = weighted(0, ch, p0[c])
        o0 = t if o0 is None else o0 + t
        p1.append(jnp.exp2(s1[c] - m1).astype(BF16))
    o1 = functools.reduce(jnp.add, [weighted(1, ch, p1[c]) for c, ch in enumerate(chunks)])
    outs = [o / pltpu.roll(o, V_HEAD, axis=1) for o in (o0, o1)]
    o_ref[...] = jnp.where(lane < V_HEAD, outs[0], outs[1]).astype(BF16)


def mla_attention(q, k_new, v_new, k_ctx, v_ctx, n_batch, tq=512):
    n = q.shape[0]
    seq = n // n_batch
    tq = min(tq, seq)
    qt = seq // tq
    pairs = MLA_HEADS // 2
    srcs_k, srcs_v = [k_new], [v_new]
    lens = [seq]
    if k_ctx is not None:
        srcs_k.append(k_ctx)
        srcs_v.append(v_ctx)
        lens.append(k_ctx.shape[0] // n_batch)
    in_specs = [pl.BlockSpec((tq, 2 * HEAD_SLAB), lambda b, p, t: (b * qt + t, p))]
    in_specs += [pl.BlockSpec((ln, 2 * HEAD_SLAB), lambda b, p, t: (b, p)) for ln in lens]
    in_specs += [pl.BlockSpec((ln, 2 * LANES), lambda b, p, t: (b, p)) for ln in lens]
    return pl.pallas_call(
        functools.partial(_mla_attn_kernel, n_sources=len(lens)),
        grid=(n_batch, pairs, qt),
        in_specs=in_specs,
        out_specs=pl.BlockSpec((tq, 2 * V_HEAD), lambda b, p, t: (b * qt + t, p)),
        out_shape=jax.ShapeDtypeStruct((n, MLA_HEADS * V_HEAD), BF16),
        compiler_params=_cparams(("parallel", "parallel", "arbitrary")),
        name="mla_attention",
    )(q, *srcs_k, *srcs_v)


def rope_lane_tables(n):
    t = jnp.arange(n)
    row = (t // GRID_W).astype(F32)
    col = (t % GRID_W).astype(F32)
    inv = ROPE_THETA ** (-jnp.arange(ROPE_PAIRS_AXIS, dtype=F32) / ROPE_PAIRS_AXIS)
    ang = jnp.concatenate([row[:, None] * inv, col[:, None] * inv], axis=-1)
    cos, sin = jnp.cos(ang), jnp.sin(ang)
    ones = jnp.ones((n, QK_NOPE), F32)
    zeros = jnp.zeros((n, QK_NOPE), F32)
    tail1 = jnp.ones((n, HEAD_SLAB - QK_NOPE - QK_ROPE), F32)
    tail0 = jnp.zeros((n, HEAD_SLAB - QK_NOPE - QK_ROPE), F32)
    z16 = jnp.zeros_like(sin)
    cos_t = jnp.concatenate([ones, cos, cos, tail1], axis=-1)
    sin_fwd = jnp.concatenate([zeros, z16, sin, tail0], axis=-1)
    sin_bwd = jnp.concatenate([zeros, -sin, z16, tail0], axis=-1)
    return cos_t, sin_fwd, sin_bwd


def _pad_mod(m, groups):
    m = m.reshape(groups, N_MOD, D_MODEL)
    return jnp.pad(m, ((0, 0), (0, MOD_ROWS - N_MOD), (0, 0)))


def _mla_weights(w_in, w_uq, w_ukv):
    pad_tail = HEAD_SLAB - QK_NOPE - QK_ROPE
    slab_cols = jnp.pad(w_in[:, Q_LORA + KV_LORA:], ((0, 0), (QK_NOPE, pad_tail)))
    win = jnp.concatenate([w_in[:, :Q_LORA + KV_LORA], slab_cols], axis=1).astype(BF16)
    wuq = jnp.pad(w_uq.reshape(Q_LORA, MLA_HEADS, QK_NOPE + QK_ROPE), ((0, 0), (0, 0), (0, pad_tail)))
    wuq = wuq.reshape(Q_LORA, MLA_HEADS * HEAD_SLAB).astype(BF16)
    wkv = w_ukv.reshape(KV_LORA, MLA_HEADS, QK_NOPE + V_HEAD)
    wk = jnp.pad(wkv[:, :, :QK_NOPE], ((0, 0), (0, 0), (0, HEAD_SLAB - QK_NOPE)))
    wk = wk.reshape(KV_LORA, MLA_HEADS * HEAD_SLAB).astype(BF16)
    wv = wkv[:, :, QK_NOPE:].reshape(KV_LORA, MLA_HEADS * V_HEAD).astype(BF16)
    return win, wuq, wk, wv


def kernel(x_prompt, x_sample, cache_na_k, cache_na_v, cache_mla_ckv, cache_mla_krope, c, c_ctx, w_ada, b_ada, ln_mix_g, ln_mix_b, ln_ffn_g, ln_ffn_b, w_in_ab, conv_w, na_rpb, w_out_ab, w_in_mla, q_norm_g, w_uq, kv_norm_g, w_ukv, w_out_mla, peer_w_q, peer_subkeys, peer_u, peer_v):
    nb_p, seq_p, d = x_prompt.shape
    nb_s, seq_s, _ = x_sample.shape
    past = cache_na_k.shape[2]
    rows = seq_s // GRID_W
    n_p = nb_p * seq_p
    xp = x_prompt.reshape(n_p, d)
    xs = x_sample.reshape(nb_s * seq_s, d)

    n_cond = 1 + nb_s
    cond_rows = -(-n_cond // 8) * 8
    cvec = jnp.concatenate([c_ctx[None], c, jnp.zeros((cond_rows - n_cond, d), F32)], axis=0)

    outs = {}
    for i in range(DEPTH):
        j = i // 2
        mods = ada_modulation(cvec, w_ada[i], b_ada[i])
        mod_p = _pad_mod(mods[0:1], 1)
        mod_s = _pad_mod(mods[1:n_cond], nb_s)
        lg = ln_mix_g[i].reshape(1, d)
        lb = ln_mix_b[i].reshape(1, d)
        if i % 2 == 0:
            w_in = w_in_ab[j].astype(BF16)
            w_out = w_out_ab[j].astype(BF16)
            cw = jnp.pad(conv_w[j], ((0, 8 - CONV_TAPS), (0, 0)))
            qkv_p, gu_p, kv_p = inproj_ab(xp, mod_p, w_in, n_p, True)
            qkv_s, gu_s = inproj_ab(xs, mod_s, w_in, seq_s, False)
            outs["na_k"] = kv_p[:, :NA_WIDTH].reshape(nb_p, 1, seq_p, NA_HEADS, NA_HEAD_DIM)
            outs["na_v"] = kv_p[:, NA_WIDTH:].reshape(nb_p, 1, seq_p, NA_HEADS, NA_HEAD_DIM)
            att_p = ctx_attention(qkv_p, seq_p)
            kc = cache_na_k[:, j].reshape(nb_s * past, NA_WIDTH).astype(BF16)
            vc = cache_na_v[:, j].reshape(nb_s * past, NA_WIDTH).astype(BF16)
            att_s = na_attention(qkv_s, kc, vc, na_bias_table(na_rpb[j]), nb_s, rows)
            xp = outproj_ab(att_p, gu_p, xp, mod_p, w_out, cw, lg, lb, n_p, seq_p)
            xs = outproj_ab(att_s, gu_s, xs, mod_s, w_out, cw, lg, lb, seq_s, seq_s)
        else:
            win, wuq, wk, wv = _mla_weights(w_in_mla[j], w_uq[j], w_ukv[j])
            w_out = w_out_mla[j].astype(BF16)
            qg = q_norm_g[j].reshape(1, Q_LORA)
            kvg = kv_norm_g[j].reshape(1, KV_LORA)
            q_p, ckv_p, slab_p = mla_inproj(xp, mod_p, win, qg, kvg, wuq, None, n_p, seq_p)
            q_s, ckv_s, slab_s = mla_inproj(xs, mod_s, win, qg, kvg, wuq, rope_lane_tables(seq_s), seq_s, seq_s)
            outs["ckv"] = ckv_p.reshape(nb_p, 1, seq_p, KV_LORA)
            outs["kr"] = slab_p[:, QK_NOPE:QK_NOPE + QK_ROPE].reshape(nb_p, 1, seq_p, QK_ROPE)
            k_p, v_p = mla_kv(ckv_p, slab_p, wk, wv)
            k_s, v_s = mla_kv(ckv_s, slab_s, wk, wv)
            ckv_c = cache_mla_ckv[:, j].reshape(nb_s * past, KV_LORA)
            slab_c = jnp.pad(cache_mla_krope[:, j].reshape(nb_s * past, QK_ROPE),
                             ((0, 0), (QK_NOPE, HEAD_SLAB - QK_NOPE - QK_ROPE)))
            k_c, v_c = mla_kv(ckv_c, slab_c, wk, wv)
            att_p = mla_attention(q_p, k_p, v_p, None, None, nb_p)
            att_s = mla_attention(q_s, k_s, v_s, k_c, v_c, nb_s)
            xp = outproj(att_p, xp, mod_p, w_out, lg, lb, n_p)
            xs = outproj(att_s, xs, mod_s, w_out, lg, lb, seq_s)
        wq = peer_w_q[i].astype(BF16)
        sk = peer_subkeys[i].reshape(2 * PEER_HEADS, N_KEYS, PEER_HALF).astype(BF16)
        u = peer_u[i].astype(BF16)
        vt = peer_v[i].T.astype(BF16)
        fg = ln_ffn_g[i].reshape(1, d)
        fb = ln_ffn_b[i].reshape(1, d)
        xp = peer_block(xp, mod_p, wq, sk, u, vt, fg, fb, n_p)
        xs = peer_block(xs, mod_s, wq, sk, u, vt, fg, fb, seq_s)
    return (xp.reshape(nb_p, seq_p, d), xs.reshape(nb_s, seq_s, d),
            outs["na_k"], outs["na_v"], outs["ckv"], outs["kr"])
```

```python
import functools

import jax
import jax.numpy as jnp
import numpy as np
from jax import lax
from jax.experimental import pallas as pl
from jax.experimental.pallas import tpu as pltpu

F32 = jnp.float32
BF16 = jnp.bfloat16

D_MODEL = 1024
DEPTH = 2
GRID_W = 64
N_MOD = 6
ALPHA = (2.0 * DEPTH) ** 0.25
LN_EPS = 1e-5
RMS_EPS = 1e-6
NA_HEADS = 8
NA_HEAD_DIM = 64
NA_WIDTH = NA_HEADS * NA_HEAD_DIM
NA_MAX_ROWS = 8
NA_WIN_COLS = 16
NA_SCALE = NA_HEAD_DIM ** -0.5
CONV_WIDTH = 512
CONV_TAPS = 3
MLA_HEADS = 16
Q_LORA = 256
KV_LORA = 256
QK_NOPE = 64
QK_ROPE = 32
V_HEAD = 64
MLA_SCALE = (QK_NOPE + QK_ROPE) ** -0.5
ROPE_THETA = 10000.0
ROPE_PAIRS_AXIS = QK_ROPE // 4
PEER_HEADS = 8
PEER_HALF = 128
N_KEYS = 128
N_EXPERTS = N_KEYS * N_KEYS
PEER_TOPK = 16

LANES = 128
MOD_ROWS = 8
HEAD_SLAB = 128
MASK_BIAS = -1e30
VMEM_LIMIT = 52 * 1024 * 1024


def _cparams(sem, flags=None):
    return pltpu.CompilerParams(dimension_semantics=sem, vmem_limit_bytes=VMEM_LIMIT, flags=flags)


def _nt_dot(a, b):
    return lax.dot_general(a, b, (((1,), (1,)), ((), ())), preferred_element_type=F32)


def _dot(a, b):
    return jnp.dot(a, b, preferred_element_type=F32)


def _split_bf16(a):
    hi = a.astype(BF16)
    lo = (a - hi.astype(F32)).astype(BF16)
    return hi, lo


def _layer_norm(z, g, b):
    mu = jnp.mean(z, axis=-1, keepdims=True)
    zc = z - mu
    var = jnp.mean(zc * zc, axis=-1, keepdims=True)
    return zc * lax.rsqrt(var + LN_EPS) * g + b


LOG2E = 1.4426950408889634
LN2 = 0.6931471805599453
_GELU_A = -2.0 * LOG2E * 0.7978845608028654
_GELU_B = _GELU_A * 0.044715


def _gelu_tanh(x):
    return x / (1.0 + jnp.exp2(x * (_GELU_A + _GELU_B * (x * x))))


def _ada_kernel(c_ref, w_ref, b_ref, o_ref):
    c = c_ref[...]
    a = c * (1.0 / (1.0 + jnp.exp(-c)))
    a_hi, a_lo = _split_bf16(a)
    w_hi, w_lo = _split_bf16(w_ref[...])
    o_ref[...] = _dot(a_hi, w_hi) + _dot(a_hi, w_lo) + _dot(a_lo, w_hi) + b_ref[...]


def ada_modulation(cvec, w_ada, b_ada, tn=512):
    r, d = cvec.shape
    n = w_ada.shape[1]
    return pl.pallas_call(
        _ada_kernel,
        grid=(n // tn,),
        in_specs=[pl.BlockSpec((r, d), lambda j: (0, 0)),
                  pl.BlockSpec((d, tn), lambda j: (0, j)),
                  pl.BlockSpec((1, tn), lambda j: (0, j))],
        out_specs=pl.BlockSpec((r, tn), lambda j: (0, j)),
        out_shape=jax.ShapeDtypeStruct((r, n), F32),
        compiler_params=_cparams(("arbitrary",)),
        name="ada_modulation",
    )(cvec, w_ada, b_ada.reshape(1, n))


def _inproj_ab_kernel(x_ref, mod_ref, w_ref, qkv_ref, gu_ref, *kv_ref):
    m = mod_ref[0]
    h = x_ref[...] * (1.0 + m[1:2]) + m[0:1]
    y = _dot(h.astype(BF16), w_ref[...])
    w3 = 3 * NA_WIDTH
    qkv_ref[:, :NA_WIDTH] = (y[:, :NA_WIDTH] * (NA_SCALE * LOG2E)).astype(BF16)
    qkv_ref[:, NA_WIDTH:] = y[:, NA_WIDTH:w3].astype(BF16)
    if kv_ref:
        kv_ref[0][...] = y[:, NA_WIDTH:w3]
    gu_ref[:, :CONV_WIDTH] = y[:, w3:w3 + CONV_WIDTH].astype(BF16)
    gu_ref[:, CONV_WIDTH:] = (y[:, w3 + CONV_WIDTH:w3 + 2 * CONV_WIDTH]
                              * y[:, w3 + 2 * CONV_WIDTH:]).astype(BF16)


def inproj_ab(x, mod, w_bf16, tokens_per_group, emit_kv, tm=512):
    n, d = x.shape
    tm = min(tm, n)
    out_shape = [jax.ShapeDtypeStruct((n, 3 * NA_WIDTH), BF16),
                 jax.ShapeDtypeStruct((n, 2 * CONV_WIDTH), BF16)]
    out_specs = [pl.BlockSpec((tm, 3 * NA_WIDTH), lambda i: (i, 0)),
                 pl.BlockSpec((tm, 2 * CONV_WIDTH), lambda i: (i, 0))]
    if emit_kv:
        out_shape.append(jax.ShapeDtypeStruct((n, 2 * NA_WIDTH), F32))
        out_specs.append(pl.BlockSpec((tm, 2 * NA_WIDTH), lambda i: (i, 0)))
    return pl.pallas_call(
        _inproj_ab_kernel,
        grid=(n // tm,),
        in_specs=[pl.BlockSpec((tm, d), lambda i: (i, 0)),
                  pl.BlockSpec((1, MOD_ROWS, d), lambda i: ((i * tm) // tokens_per_group, 0, 0)),
                  pl.BlockSpec(w_bf16.shape, lambda i: (0, 0))],
        out_specs=out_specs,
        out_shape=out_shape,
        compiler_params=_cparams(("parallel",)),
        name="inproj_ab",
    )(x, mod, w_bf16)


def _pair_attention(instances):
    heads = []
    for n, (q_pair, _, _, _) in enumerate(instances):
        lane = lax.broadcasted_iota(jnp.int32, q_pair.shape, 1)
        qf = q_pair.astype(F32)
        for j in range(2):
            in_head = (lane >= j * NA_HEAD_DIM) & (lane < (j + 1) * NA_HEAD_DIM)
            heads.append((n, j, jnp.where(in_head, qf, 0.0).astype(BF16)))
    scores = []
    for n, j, qm in heads:
        _, keys, _, biases = instances[n]
        ss = [_nt_dot(qm, k) for k in keys]
        scores.append([s if b is None else s + b for s, b in zip(ss, biases[j])])
    mxs = [functools.reduce(jnp.maximum, [jnp.max(s, axis=-1, keepdims=True) for s in ss]) for ss in scores]
    ess = [[jnp.exp2(s - mx) for s in ss] for ss, mx in zip(scores, mxs)]
    dens = [functools.reduce(jnp.add, [jnp.sum(e, axis=-1, keepdims=True) for e in es]) for es in ess]
    outs = []
    for (n, j, _), es in zip(heads, ess):
        values = instances[n][2]
        outs.append(functools.reduce(jnp.add, [_dot(e.astype(BF16), v) for e, v in zip(es, values)]))
    outs = [o / d for o, d in zip(outs, dens)]
    lane = lax.broadcasted_iota(jnp.int32, outs[0].shape, 1)
    return [jnp.where(lane < NA_HEAD_DIM, outs[2 * n], outs[2 * n + 1]) for n in range(len(instances))]


NA_ROWS_PER_STEP = 2


def _na_attn_kernel(q_ref, k_ref, v_ref, kc_ref, vc_ref, *rest, rows):
    bias_refs = rest[:NA_ROWS_PER_STEP]
    o_ref = rest[NA_ROWS_PER_STEP]
    band = NA_MAX_ROWS * GRID_W
    instances, places = [], []
    for rr in range(NA_ROWS_PER_STEP):
        r = pl.program_id(1) * NA_ROWS_PER_STEP + rr
        rs = jnp.clip(r - NA_MAX_ROWS // 2, 0, rows - NA_MAX_ROWS)
        start = pl.multiple_of(rs * GRID_W, GRID_W)
        qrows = slice(rr * GRID_W, (rr + 1) * GRID_W)
        for p in range(NA_HEADS // 2):
            sl = slice(p * LANES, (p + 1) * LANES)
            keys = [k_ref[pl.ds(start, band), sl], kc_ref[:, sl]]
            values = [v_ref[pl.ds(start, band), sl], vc_ref[:, sl]]
            biases = [[bias_refs[rr][0, 2 * p + j], None] for j in range(2)]
            instances.append((q_ref[qrows, sl], keys, values, biases))
            places.append((qrows, sl))
    for (qrows, sl), o in zip(places, _pair_attention(instances)):
        o_ref[qrows, sl] = o.astype(BF16)


def na_attention(qkv, kc, vc, bias, n_batch, rows):
    n = qkv.shape[0]
    seq = rows * GRID_W
    lc = kc.shape[0] // n_batch
    half = NA_MAX_ROWS // 2

    steps = rows // NA_ROWS_PER_STEP
    tq = NA_ROWS_PER_STEP * GRID_W

    def bias_spec(rr):
        def idx(b, g):
            r = g * NA_ROWS_PER_STEP + rr
            return (r - jnp.clip(r - half, 0, rows - NA_MAX_ROWS), 0, 0, 0)
        return pl.BlockSpec((1, NA_HEADS, GRID_W, NA_MAX_ROWS * GRID_W), idx)

    return pl.pallas_call(
        functools.partial(_na_attn_kernel, rows=rows),
        grid=(n_batch, steps),
        in_specs=[pl.BlockSpec((tq, NA_WIDTH), lambda b, g: (b * steps + g, 0)),
                  pl.BlockSpec((seq, NA_WIDTH), lambda b, g: (b, 1)),
                  pl.BlockSpec((seq, NA_WIDTH), lambda b, g: (b, 2)),
                  pl.BlockSpec((lc, NA_WIDTH), lambda b, g: (b, 0)),
                  pl.BlockSpec((lc, NA_WIDTH), lambda b, g: (b, 0))]
        + [bias_spec(rr) for rr in range(NA_ROWS_PER_STEP)],
        out_specs=pl.BlockSpec((tq, NA_WIDTH), lambda b, g: (b * steps + g, 0)),
        out_shape=jax.ShapeDtypeStruct((n, NA_WIDTH), BF16),
        compiler_params=_cparams(("parallel", "arbitrary")),
        name="na_attention",
    )(qkv, qkv, qkv, kc, vc, *([bias] * NA_ROWS_PER_STEP))


def na_bias_table(rpb):
    c = np.arange(GRID_W)[:, None]
    kc = np.arange(GRID_W)[None, :]
    cs = np.clip(c - NA_WIN_COLS // 2, 0, GRID_W - NA_WIN_COLS)
    valid = (kc >= cs) & (kc < cs + NA_WIN_COLS)
    dc = kc - c + (NA_WIN_COLS - 1)
    cc, kk = np.nonzero(valid)
    place = np.zeros((2 * NA_WIN_COLS - 1, GRID_W, GRID_W), np.float32)
    place[dc[cc, kk], cc, kk] = 1.0
    t = jnp.einsum("hrd,dck->hrck", rpb, place, precision=lax.Precision.HIGHEST)
    t = jnp.where(valid[None, None], t * LOG2E, MASK_BIAS)
    bands = [t[:, NA_MAX_ROWS - 1 - off:2 * NA_MAX_ROWS - 1 - off] for off in range(NA_MAX_ROWS)]
    b = jnp.stack(bands, axis=0)
    b = jnp.transpose(b, (0, 1, 3, 2, 4))
    return b.reshape(NA_MAX_ROWS, NA_HEADS, GRID_W, NA_MAX_ROWS * GRID_W).astype(F32)


def _ctx_attn_kernel(q_ref, k_ref, v_ref, o_ref):
    slabs = [slice(p * LANES, (p + 1) * LANES) for p in range(NA_HEADS // 2)]
    instances = [(q_ref[:, sl], [k_ref[:, sl]], [v_ref[:, sl]], [[None], [None]]) for sl in slabs]
    for sl, o in zip(slabs, _pair_attention(instances)):
        o_ref[:, sl] = o.astype(BF16)


def ctx_attention(qkv, seq):
    n = qkv.shape[0]
    return pl.pallas_call(
        _ctx_attn_kernel,
        grid=(n // seq,),
        in_specs=[pl.BlockSpec((seq, NA_WIDTH), lambda b: (b, 0)),
                  pl.BlockSpec((seq, NA_WIDTH), lambda b: (b, 1)),
                  pl.BlockSpec((seq, NA_WIDTH), lambda b: (b, 2))],
        out_specs=pl.BlockSpec((seq, NA_WIDTH), lambda b: (b, 0)),
        out_shape=jax.ShapeDtypeStruct((n, NA_WIDTH), BF16),
        compiler_params=_cparams(("parallel",)),
        name="ctx_attention",
    )(qkv, qkv, qkv)


HALO = 16


def _outproj_ab_kernel(att_ref, gu_ref, prev_ref, next_ref, x_ref, mod_ref, w_ref, cw_ref, g_ref, b_ref,
                       o_ref, *, tiles_per_seq):
    i = pl.program_id(0)
    tm = x_ref.shape[0]
    m = mod_ref[0]
    gu = gu_ref[...].astype(F32)
    bg = gu[:, :CONV_WIDTH]
    u = gu[:, CONV_WIDTH:]
    has_prev = (i % tiles_per_seq != 0).astype(F32)
    has_next = (i % tiles_per_seq != tiles_per_seq - 1).astype(F32)
    prev_row = prev_ref[HALO - 1:HALO, CONV_WIDTH:].astype(F32) * has_prev
    next_row = next_ref[0:1, CONV_WIDTH:].astype(F32) * has_next
    row = lax.broadcasted_iota(jnp.int32, u.shape, 0)
    u_m1 = jnp.where(row == 0, prev_row, pltpu.roll(u, 1, axis=0))
    u_p1 = jnp.where(row == tm - 1, next_row, pltpu.roll(u, tm - 1, axis=0))
    cw = cw_ref[...]
    conv = bg * (u_m1 * cw[0:1] + u * cw[1:2] + u_p1 * cw[2:3])
    y = _dot(att_ref[...], w_ref[:NA_WIDTH, :]) + _dot(conv.astype(BF16), w_ref[NA_WIDTH:, :])
    z = ALPHA * x_ref[...] + m[2:3] * y
    o_ref[...] = _layer_norm(z, g_ref[...], b_ref[...])


def outproj_ab(att, gu, x, mod, w_bf16, conv_w8, ln_g, ln_b, tokens_per_group, seq, tm=256):
    n, d = x.shape
    tm = min(tm, seq)
    hb = tm // HALO
    last_halo = n // HALO - 1
    return pl.pallas_call(
        functools.partial(_outproj_ab_kernel, tiles_per_seq=seq // tm),
        grid=(n // tm,),
        in_specs=[pl.BlockSpec((tm, NA_WIDTH), lambda i: (i, 0)),
                  pl.BlockSpec((tm, 2 * CONV_WIDTH), lambda i: (i, 0)),
                  pl.BlockSpec((HALO, 2 * CONV_WIDTH), lambda i: (jnp.maximum(i * hb - 1, 0), 0)),
                  pl.BlockSpec((HALO, 2 * CONV_WIDTH), lambda i: (jnp.minimum((i + 1) * hb, last_halo), 0)),
                  pl.BlockSpec((tm, d), lambda i: (i, 0)),
                  pl.BlockSpec((1, MOD_ROWS, d), lambda i: ((i * tm) // tokens_per_group, 0, 0)),
                  pl.BlockSpec(w_bf16.shape, lambda i: (0, 0)),
                  pl.BlockSpec(conv_w8.shape, lambda i: (0, 0)),
                  pl.BlockSpec((1, d), lambda i: (0, 0)),
                  pl.BlockSpec((1, d), lambda i: (0, 0))],
        out_specs=pl.BlockSpec((tm, d), lambda i: (i, 0)),
        out_shape=jax.ShapeDtypeStruct((n, d), F32),
        compiler_params=_cparams(("parallel",)),
        name="outproj_ab",
    )(att, gu, gu, gu, x, mod, w_bf16, conv_w8, ln_g, ln_b)


def _outproj_kernel(att_ref, x_ref, mod_ref, w_ref, g_ref, b_ref, o_ref):
    m = mod_ref[0]
    y = _dot(att_ref[...], w_ref[...])
    z = ALPHA * x_ref[...] + m[2:3] * y
    o_ref[...] = _layer_norm(z, g_ref[...], b_ref[...])


def outproj(att, x, mod, w_bf16, ln_g, ln_b, tokens_per_group, tm=512):
    n, d = x.shape
    k = att.shape[1]
    return pl.pallas_call(
        _outproj_kernel,
        grid=(n // tm,),
        in_specs=[pl.BlockSpec((tm, k), lambda i: (i, 0)),
                  pl.BlockSpec((tm, d), lambda i: (i, 0)),
                  pl.BlockSpec((1, MOD_ROWS, d), lambda i: ((i * tm) // tokens_per_group, 0, 0)),
                  pl.BlockSpec(w_bf16.shape, lambda i: (0, 0)),
                  pl.BlockSpec((1, d), lambda i: (0, 0)),
                  pl.BlockSpec((1, d), lambda i: (0, 0))],
        out_specs=pl.BlockSpec((tm, d), lambda i: (i, 0)),
        out_shape=jax.ShapeDtypeStruct((n, d), F32),
        compiler_params=_cparams(("parallel",)),
        name="outproj",
    )(att, x, mod, w_bf16, ln_g, ln_b)


def _peer_scores_kernel(x_ref, mod_ref, wq_ref, sk_ref, hm_ref, st_ref):
    m = mod_ref[0]
    h = x_ref[...] * (1.0 + m[4:5]) + m[3:4]
    hb = h.astype(BF16)
    hm_ref[...] = h.T.astype(BF16)
    q = _dot(hb, wq_ref[...])
    for c in range(2 * PEER_HEADS):
        qc = q[:, c * PEER_HALF:(c + 1) * PEER_HALF].astype(BF16)
        st_ref[c] = _nt_dot(sk_ref[c], qc) * LOG2E


def peer_scores(x, mod, wq_bf16, subkeys_bf16, tokens_per_group, tm=512):
    n, d = x.shape
    nc = 2 * PEER_HEADS
    return pl.pallas_call(
        _peer_scores_kernel,
        grid=(n // tm,),
        in_specs=[pl.BlockSpec((tm, d), lambda i: (i, 0)),
                  pl.BlockSpec((1, MOD_ROWS, d), lambda i: ((i * tm) // tokens_per_group, 0, 0)),
                  pl.BlockSpec(wq_bf16.shape, lambda i: (0, 0)),
                  pl.BlockSpec(subkeys_bf16.shape, lambda i: (0, 0, 0))],
        out_specs=[pl.BlockSpec((d, tm), lambda i: (0, i)),
                   pl.BlockSpec((nc, N_KEYS, tm), lambda i: (0, 0, i))],
        out_shape=[jax.ShapeDtypeStruct((d, n), BF16),
                   jax.ShapeDtypeStruct((nc, N_KEYS, n), F32)],
        compiler_params=_cparams(("parallel",)),
        name="peer_scores",
    )(x, mod, wq_bf16, subkeys_bf16)


SUBLANES = 8
N_SLABS = N_KEYS // SUBLANES


def _sort_network(n):
    pairs = []
    p = 1
    while p < n:
        k = p
        while k >= 1:
            for j in range(k % p, n - k, 2 * k):
                for i in range(min(k, n - j - k)):
                    if (i + j) // (2 * p) == (i + j + k) // (2 * p):
                        pairs.append((i + j, i + j + k))
            k //= 2
        p *= 2
    return pairs


_SLAB_SORT = _sort_network(N_SLABS)


def _top_values(s_ref, c, n_top):
    return _top_of_slabs([s_ref[c, r] for r in range(N_SLABS)], n_top)


def _top_of_slabs(slabs, n_top):
    x = list(slabs)
    n = len(x)
    for a, b in _SLAB_SORT:
        if b < n:
            x[a], x[b] = jnp.maximum(x[a], x[b]), jnp.minimum(x[a], x[b])
    sub = lax.broadcasted_iota(jnp.int32, x[0].shape, 0).astype(F32)
    x.append(jnp.full(x[0].shape, -jnp.inf, F32))
    vals = []
    for k in range(n_top):
        mx = jnp.max(x[0], axis=0, keepdims=True)
        vals.append(mx)
        if k == n_top - 1:
            break
        first = jnp.min(jnp.where(x[0] == mx, sub, float(SUBLANES)), axis=0, keepdims=True)
        pop = sub == first
        for r in range(min(n, n_top - 1 - k)):
            x[r] = jnp.where(pop, x[r + 1], x[r])
    return vals


def _rows_to_slab(rows_list):
    t = rows_list[0].shape[1]
    ridx = lax.broadcasted_iota(jnp.int32, (SUBLANES, t), 0)
    out = jnp.full((SUBLANES, t), -jnp.inf, F32)
    for r, row in enumerate(rows_list):
        out = jnp.where(ridx == r, row, out)
    return out


def _peer_select_kernel(st_ref, thr_ref, lse_ref, m2_ref):
    k = PEER_TOPK
    for h in range(PEER_HEADS):
        v1 = _top_values(st_ref, 2 * h, k + 1)
        v2 = _top_values(st_ref, 2 * h + 1, k + 1)
        v1_lo, v1_hi = _rows_to_slab(v1[0:8]), _rows_to_slab(v1[8:16])
        v2_lo, v2_hi = _rows_to_slab(v2[0:8]), _rows_to_slab(v2[8:16])
        ridx = lax.broadcasted_iota(jnp.int32, v1_lo.shape, 0)
        v1_mid = jnp.where(ridx >= 4, v1_lo, -jnp.inf)
        cands = [v1[0] + v2_lo, v1[0] + v2_hi, v1[1] + v2_lo, v1[2] + v2_lo, v1[3] + v2_lo,
                 v1_hi + v2[0], v1_mid + v2[0], v1_mid + v2[1], v1_mid + v2[2],
                 _rows_to_slab([v1[k] + v2[0], v1[0] + v2[k]])]
        top = v1[0] + v2[0]
        best = _top_of_slabs(cands, k + 1)
        thr, nxt = best[k - 1], best[k]
        z = functools.reduce(jnp.add, [jnp.where(cd >= thr, jnp.exp2(cd - top), 0.0) for cd in cands])
        z = jnp.sum(z, axis=0, keepdims=True)
        thr_ref[h:h + 1, :] = 0.5 * (thr + nxt)
        lse_ref[h:h + 1, :] = v1[0] + jnp.log(z) * LOG2E
        m2_ref[h:h + 1, :] = v2[0]


def peer_select(st, tl=256):
    nc, _, _, n = st.shape
    stat = pl.BlockSpec((PEER_HEADS, tl), lambda i: (0, i))
    return pl.pallas_call(
        _peer_select_kernel,
        grid=(n // tl,),
        in_specs=[pl.BlockSpec((nc, N_SLABS, SUBLANES, tl), lambda i: (0, 0, 0, i))],
        out_specs=[stat, stat, stat],
        out_shape=[jax.ShapeDtypeStruct((PEER_HEADS, n), F32)] * 3,
        compiler_params=_cparams(("parallel",)),
        name="peer_select",
    )(st)


PEER_LANE_CHUNK = 128
PEER_MXU_CHUNK = 256
PEER_MXU_ROWS = 256


def _peer_dense_kernel(hm_ref, u_ref, vt_ref, st_ref, thr_ref, lse_ref, m2_ref, x_ref, mod_ref, g_ref, b_ref,
                       o_ref, acc_ref, ht_ref, at_ref, e2_ref, cut_rep, f1_rep, *, gate_row):
    e = pl.program_id(1)
    tm = hm_ref.shape[1]
    n_mxu_chunks = tm // PEER_MXU_CHUNK
    row_blocks = u_ref.shape[0] // PEER_MXU_ROWS

    @pl.when(e == 0)
    def _():
        acc_ref[...] = jnp.zeros_like(acc_ref)
        for h in range(PEER_HEADS):
            e2_ref[h] = jnp.exp2(st_ref[2 * h + 1] - m2_ref[h:h + 1, :])

    def first_matmul(cm, rb):
        ms = slice(cm * PEER_MXU_CHUNK, (cm + 1) * PEER_MXU_CHUNK)
        rs = slice(rb * PEER_MXU_ROWS, (rb + 1) * PEER_MXU_ROWS)
        ht_ref[rs, ms] = _dot(u_ref[rs, :], hm_ref[:, ms])


    for h in range(PEER_HEADS):
        s1 = st_ref[2 * h, e]
        cut_rows = thr_ref[h:h + 1, :] - s1
        f1_rows = jnp.exp2(s1 - lse_ref[h:h + 1, :])
        for ii in range(SUBLANES):
            cut_rep[h * SUBLANES + ii] = jnp.broadcast_to(cut_rows[ii:ii + 1, :], (SUBLANES, tm))
            f1_rep[h * SUBLANES + ii] = jnp.broadcast_to(f1_rows[ii:ii + 1, :], (SUBLANES, tm))
    def second_matmul(cm, kb):
        ms = slice(cm * PEER_MXU_CHUNK, (cm + 1) * PEER_MXU_CHUNK)
        ks = slice(kb * (te // 2), (kb + 1) * (te // 2))
        r = _dot(vt_ref[:, ks], at_ref[ks, ms])
        acc_ref[:, ms] += r
        bits = lax.bitcast_convert_type(r[0:SUBLANES, 0:PEER_LANE_CHUNK], jnp.int32)
        return lax.shift_right_logical(lax.shift_right_logical(bits, 16), 16).astype(F32)

    lanes_per_mxu = PEER_MXU_CHUNK // PEER_LANE_CHUNK
    te = u_ref.shape[0]
    seeds = {}
    for cm in range(n_mxu_chunks):
        units = [(ii, c) for ii in range(SUBLANES) for c in range(cm * lanes_per_mxu, (cm + 1) * lanes_per_mxu)]
        per_block = len(units) // row_blocks
        pieces = [functools.partial(first_matmul, cm, rb) for rb in range(row_blocks)] if cm == 0 else []
        if cm + 1 < n_mxu_chunks:
            pieces += [functools.partial(first_matmul, cm + 1, rb) for rb in range(row_blocks)]
        lead = per_block // 2 if cm == 0 else per_block
        for idx, (ii, c) in enumerate(units):
            if idx % lead == 0 and idx // lead < len(pieces):
                pieces[idx // lead]()
            if idx == len(units) // 2:
                seeds[(cm, len(units) - 1)] = second_matmul(cm, 0)
            rows = slice(ii * N_KEYS, (ii + 1) * N_KEYS)
            ls = slice(c * PEER_LANE_CHUNK, (c + 1) * PEER_LANE_CHUNK)
            w = jnp.zeros((N_SLABS, SUBLANES, PEER_LANE_CHUNK), F32)
            if (cm, idx) in seeds:
                w = w + seeds.pop((cm, idx))[None]
            for h in range(PEER_HEADS):
                sel = st_ref[2 * h + 1, :, :, ls] >= cut_rep[h * SUBLANES + ii, :, ls][None]
                w = w + jnp.where(sel, e2_ref[h, :, :, ls] * f1_rep[h * SUBLANES + ii, :, ls][None], 0.0)
            act = _gelu_tanh(ht_ref[rows, ls])
            at_ref[rows, ls] = (act * w.reshape(N_KEYS, PEER_LANE_CHUNK)).astype(BF16)
        seeds[(cm + 1, len(units) // 2)] = second_matmul(cm, 1)

    @pl.when(e == pl.num_programs(1) - 1)
    def _():
        m = mod_ref[0]
        z = ALPHA * x_ref[...] + m[gate_row:gate_row + 1] * acc_ref[...].T
        o_ref[...] = _layer_norm(z, g_ref[...], b_ref[...])


def peer_dense(hm, u_bf16, vt_bf16, st, thr, lse, m2, x, mod, ln_g, ln_b, tokens_per_group, tm=512, te=1024):
    n, d = x.shape
    ne = u_bf16.shape[0]
    nc = st.shape[0]
    assert te == N_KEYS * SUBLANES
    return pl.pallas_call(
        functools.partial(_peer_dense_kernel, gate_row=5),
        grid=(n // tm, ne // te),
        in_specs=[pl.BlockSpec((d, tm), lambda i, g: (0, i)),
                  pl.BlockSpec((te, d), lambda i, g: (g, 0)),
                  pl.BlockSpec((d, te), lambda i, g: (0, g)),
                  pl.BlockSpec((nc, N_SLABS, SUBLANES, tm), lambda i, g: (0, 0, 0, i)),
                  pl.BlockSpec((PEER_HEADS, tm), lambda i, g: (0, i)),
                  pl.BlockSpec((PEER_HEADS, tm), lambda i, g: (0, i)),
                  pl.BlockSpec((PEER_HEADS, tm), lambda i, g: (0, i)),
                  pl.BlockSpec((tm, d), lambda i, g: (i, 0)),
                  pl.BlockSpec((1, MOD_ROWS, d), lambda i, g: ((i * tm) // tokens_per_group, 0, 0)),
                  pl.BlockSpec((1, d), lambda i, g: (0, 0)),
                  pl.BlockSpec((1, d), lambda i, g: (0, 0))],
        out_specs=pl.BlockSpec((tm, d), lambda i, g: (i, 0)),
        out_shape=jax.ShapeDtypeStruct((n, d), F32),
        scratch_shapes=[pltpu.VMEM((d, tm), F32),
                        pltpu.VMEM((te, tm), F32),
                        pltpu.VMEM((te, tm), BF16),
                        pltpu.VMEM((PEER_HEADS, N_SLABS, SUBLANES, tm), F32),
                        pltpu.VMEM((PEER_HEADS * SUBLANES, SUBLANES, tm), F32),
                        pltpu.VMEM((PEER_HEADS * SUBLANES, SUBLANES, tm), F32)],
        compiler_params=_cparams(("parallel", "arbitrary")),
        name="peer_dense",
    )(hm, u_bf16, vt_bf16, st, thr, lse, m2, x, mod, ln_g, ln_b)


def peer_block(x, mod, wq, sk, u, vt, ln_g, ln_b, tokens_per_group):
    hm, st = peer_scores(x, mod, wq, sk, tokens_per_group)
    st = st.reshape(2 * PEER_HEADS, N_SLABS, SUBLANES, x.shape[0])
    cut, lse, m2 = peer_select(st)
    return peer_dense(hm, u, vt, st, cut, lse, m2, x, mod, ln_g, ln_b, tokens_per_group)


def _rope_slab(t, cos_t, sin_fwd, sin_bwd):
    return t * cos_t + pltpu.roll(t, QK_ROPE // 2, axis=1) * sin_fwd \
        + pltpu.roll(t, HEAD_SLAB - QK_ROPE // 2, axis=1) * sin_bwd


def _mla_inproj_kernel(x_ref, mod_ref, win_ref, qg_ref, kvg_ref, wuq_ref, *rest, use_rope):
    if use_rope:
        cos_ref, sf_ref, sb_ref, q_ref, ckv_ref, slab_ref = rest
    else:
        q_ref, ckv_ref, slab_ref = rest
    m = mod_ref[0]
    h = x_ref[...] * (1.0 + m[1:2]) + m[0:1]
    y = _dot(h.astype(BF16), win_ref[...])
    cq = y[:, :Q_LORA]
    ckv = y[:, Q_LORA:Q_LORA + KV_LORA]
    slab = y[:, Q_LORA + KV_LORA:]
    cqn = cq * lax.rsqrt(jnp.mean(cq * cq, axis=-1, keepdims=True) + RMS_EPS) * qg_ref[...]
    ckv_ref[...] = ckv * lax.rsqrt(jnp.mean(ckv * ckv, axis=-1, keepdims=True) + RMS_EPS) * kvg_ref[...]
    q = _dot(cqn.astype(BF16), wuq_ref[...]) * (MLA_SCALE * LOG2E)
    if use_rope:
        cos_t, sf, sb = cos_ref[...], sf_ref[...], sb_ref[...]
        slab = _rope_slab(slab, cos_t, sf, sb)
        for hh in range(MLA_HEADS):
            sl = slice(hh * HEAD_SLAB, (hh + 1) * HEAD_SLAB)
            q_ref[:, sl] = _rope_slab(q[:, sl], cos_t, sf, sb).astype(BF16)
    else:
        q_ref[...] = q.astype(BF16)
    slab_ref[...] = slab


def mla_inproj(x, mod, win_bf16, q_g, kv_g, wuq_bf16, rope, tokens_per_group, seq, tm=256):
    n, d = x.shape
    use_rope = rope is not None
    in_specs = [pl.BlockSpec((tm, d), lambda i: (i, 0)),
                pl.BlockSpec((1, MOD_ROWS, d), lambda i: ((i * tm) // tokens_per_group, 0, 0)),
                pl.BlockSpec(win_bf16.shape, lambda i: (0, 0)),
                pl.BlockSpec((1, Q_LORA), lambda i: (0, 0)),
                pl.BlockSpec((1, KV_LORA), lambda i: (0, 0)),
                pl.BlockSpec(wuq_bf16.shape, lambda i: (0, 0))]
    args = [x, mod, win_bf16, q_g, kv_g, wuq_bf16]
    if use_rope:
        tps = seq // tm
        in_specs += [pl.BlockSpec((tm, HEAD_SLAB), lambda i: (i % tps, 0))] * 3
        args += list(rope)
    return pl.pallas_call(
        functools.partial(_mla_inproj_kernel, use_rope=use_rope),
        grid=(n // tm,),
        in_specs=in_specs,
        out_specs=[pl.BlockSpec((tm, MLA_HEADS * HEAD_SLAB), lambda i: (i, 0)),
                   pl.BlockSpec((tm, KV_LORA), lambda i: (i, 0)),
                   pl.BlockSpec((tm, HEAD_SLAB), lambda i: (i, 0))],
        out_shape=[jax.ShapeDtypeStruct((n, MLA_HEADS * HEAD_SLAB), BF16),
                   jax.ShapeDtypeStruct((n, KV_LORA), F32),
                   jax.ShapeDtypeStruct((n, HEAD_SLAB), F32)],
        compiler_params=_cparams(("parallel",)),
        name="mla_inproj",
    )(*args)


def _mla_kv_kernel(ckv_ref, slab_ref, wk_ref, wv_ref, k_ref, v_ref):
    c = ckv_ref[...].astype(BF16)
    kn = _dot(c, wk_ref[...])
    slab = slab_ref[...]
    for hh in range(MLA_HEADS):
        sl = slice(hh * HEAD_SLAB, (hh + 1) * HEAD_SLAB)
        k_ref[:, sl] = (kn[:, sl] + slab).astype(BF16)
    v = _dot(c, wv_ref[...])
    lane = lax.broadcasted_iota(jnp.int32, (v.shape[0], LANES), 1)
    for p in range(MLA_HEADS // 2):
        pair = v[:, p * LANES:(p + 1) * LANES]
        v_ref[:, (2 * p) * LANES:(2 * p + 1) * LANES] = jnp.where(lane < V_HEAD, pair, 1.0).astype(BF16)
        v_ref[:, (2 * p + 1) * LANES:(2 * p + 2) * LANES] = jnp.where(lane < V_HEAD, 1.0, pair).astype(BF16)


def mla_kv(ckv, slab, wk_bf16, wv_bf16, tm=256):
    n = ckv.shape[0]
    tm = min(tm, n)
    return pl.pallas_call(
        _mla_kv_kernel,
        grid=(n // tm,),
        in_specs=[pl.BlockSpec((tm, KV_LORA), lambda i: (i, 0)),
                  pl.BlockSpec((tm, HEAD_SLAB), lambda i: (i, 0)),
                  pl.BlockSpec(wk_bf16.shape, lambda i: (0, 0)),
                  pl.BlockSpec(wv_bf16.shape, lambda i: (0, 0))],
        out_specs=[pl.BlockSpec((tm, MLA_HEADS * HEAD_SLAB), lambda i: (i, 0)),
                   pl.BlockSpec((tm, MLA_HEADS * LANES), lambda i: (i, 0))],
        out_shape=[jax.ShapeDtypeStruct((n, MLA_HEADS * HEAD_SLAB), BF16),
                   jax.ShapeDtypeStruct((n, MLA_HEADS * LANES), BF16)],
        compiler_params=_cparams(("parallel",)),
        name="mla_kv",
    )(ckv, slab, wk_bf16, wv_bf16)


MLA_KEY_CHUNK = 256


def _mla_attn_kernel(q_ref, *rest, n_sources):
    k_refs = rest[:n_sources]
    v_refs = rest[n_sources:2 * n_sources]
    o_ref = rest[2 * n_sources]
    tq = q_ref.shape[0]
    lane = lax.broadcasted_iota(jnp.int32, (tq, LANES), 1)
    sls = [slice(j * HEAD_SLAB, (j + 1) * HEAD_SLAB) for j in range(2)]
    qs = [q_ref[:, sl] for sl in sls]
    chunks = [(si, st) for si, k in enumerate(k_refs) for st in range(0, k.shape[0], MLA_KEY_CHUNK)]

    def score(j, ch):
        si, st = ch
        return _nt_dot(qs[j], k_refs[si][st:st + MLA_KEY_CHUNK, sls[j]])

    def row_max(ss):
        return jnp.max(functools.reduce(jnp.maximum, ss), axis=-1, keepdims=True)

    def weighted(j, ch, p):
        si, st = ch
        return _dot(p, v_refs[si][st:st + MLA_KEY_CHUNK, sls[j]])

    s0 = [score(0, ch) for ch in chunks]
    m0 = row_max(s0)
    s1, p0 = [], []
    for c, ch in enumerate(chunks):
        s1.append(score(1, ch))
        p0.append(jnp.exp2(s0[c] - m0).astype(BF16))
    m1 = row_max(s1)
    o0, p1 = None, []
    for c, ch in enumerate(chunks):
        t = weighted(0, ch, p0[c])
        o0 = t if o0 is None else o0 + t
        p1.append(jnp.exp2(s1[c] - m1).astype(BF16))
    o1 = functools.reduce(jnp.add, [weighted(1, ch, p1[c]) for c, ch in enumerate(chunks)])
    outs = [o / pltpu.roll(o, V_HEAD, axis=1) for o in (o0, o1)]
    o_ref[...] = jnp.where(lane < V_HEAD, outs[0], outs[1]).astype(BF16)


def mla_attention(q, k_new, v_new, k_ctx, v_ctx, n_batch, tq=512):
    n = q.shape[0]
    seq = n // n_batch
    tq = min(tq, seq)
    qt = seq // tq
    pairs = MLA_HEADS // 2
    srcs_k, srcs_v = [k_new], [v_new]
    lens = [seq]
    if k_ctx is not None:
        srcs_k.append(k_ctx)
        srcs_v.append(v_ctx)
        lens.append(k_ctx.shape[0] // n_batch)
    in_specs = [pl.BlockSpec((tq, 2 * HEAD_SLAB), lambda b, p, t: (b * qt + t, p))]
    in_specs += [pl.BlockSpec((ln, 2 * HEAD_SLAB), lambda b, p, t: (b, p)) for ln in lens]
    in_specs += [pl.BlockSpec((ln, 2 * LANES), lambda b, p, t: (b, p)) for ln in lens]
    return pl.pallas_call(
        functools.partial(_mla_attn_kernel, n_sources=len(lens)),
        grid=(n_batch, pairs, qt),
        in_specs=in_specs,
        out_specs=pl.BlockSpec((tq, 2 * V_HEAD), lambda b, p, t: (b * qt + t, p)),
        out_shape=jax.ShapeDtypeStruct((n, MLA_HEADS * V_HEAD), BF16),
        compiler_params=_cparams(("parallel", "parallel", "arbitrary")),
        name="mla_attention",
    )(q, *srcs_k, *srcs_v)


def rope_lane_tables(n):
    t = jnp.arange(n)
    row = (t // GRID_W).astype(F32)
    col = (t % GRID_W).astype(F32)
    inv = ROPE_THETA ** (-jnp.arange(ROPE_PAIRS_AXIS, dtype=F32) / ROPE_PAIRS_AXIS)
    ang = jnp.concatenate([row[:, None] * inv, col[:, None] * inv], axis=-1)
    cos, sin = jnp.cos(ang), jnp.sin(ang)
    ones = jnp.ones((n, QK_NOPE), F32)
    zeros = jnp.zeros((n, QK_NOPE), F32)
    tail1 = jnp.ones((n, HEAD_SLAB - QK_NOPE - QK_ROPE), F32)
    tail0 = jnp.zeros((n, HEAD_SLAB - QK_NOPE - QK_ROPE), F32)
    z16 = jnp.zeros_like(sin)
    cos_t = jnp.concatenate([ones, cos, cos, tail1], axis=-1)
    sin_fwd = jnp.concatenate([zeros, z16, sin, tail0], axis=-1)
    sin_bwd = jnp.concatenate([zeros, -sin, z16, tail0], axis=-1)
    return cos_t, sin_fwd, sin_bwd


def _pad_mod(m, groups):
    m = m.reshape(groups, N_MOD, D_MODEL)
    return jnp.pad(m, ((0, 0), (0, MOD_ROWS - N_MOD), (0, 0)))


def _mla_weights(w_in, w_uq, w_ukv):
    pad_tail = HEAD_SLAB - QK_NOPE - QK_ROPE
    slab_cols = jnp.pad(w_in[:, Q_LORA + KV_LORA:], ((0, 0), (QK_NOPE, pad_tail)))
    win = jnp.concatenate([w_in[:, :Q_LORA + KV_LORA], slab_cols], axis=1).astype(BF16)
    wuq = jnp.pad(w_uq.reshape(Q_LORA, MLA_HEADS, QK_NOPE + QK_ROPE), ((0, 0), (0, 0), (0, pad_tail)))
    wuq = wuq.reshape(Q_LORA, MLA_HEADS * HEAD_SLAB).astype(BF16)
    wkv = w_ukv.reshape(KV_LORA, MLA_HEADS, QK_NOPE + V_HEAD)
    wk = jnp.pad(wkv[:, :, :QK_NOPE], ((0, 0), (0, 0), (0, HEAD_SLAB - QK_NOPE)))
    wk = wk.reshape(KV_LORA, MLA_HEADS * HEAD_SLAB).astype(BF16)
    wv = wkv[:, :, QK_NOPE:].reshape(KV_LORA, MLA_HEADS * V_HEAD).astype(BF16)
    return win, wuq, wk, wv


def kernel(x_prompt, x_sample, cache_na_k, cache_na_v, cache_mla_ckv, cache_mla_krope, c, c_ctx, w_ada, b_ada, ln_mix_g, ln_mix_b, ln_ffn_g, ln_ffn_b, w_in_ab, conv_w, na_rpb, w_out_ab, w_in_mla, q_norm_g, w_uq, kv_norm_g, w_ukv, w_out_mla, peer_w_q, peer_subkeys, peer_u, peer_v):
    nb_p, seq_p, d = x_prompt.shape
    nb_s, seq_s, _ = x_sample.shape
    past = cache_na_k.shape[2]
    rows = seq_s // GRID_W
    n_p = nb_p * seq_p
    xp = x_prompt.reshape(n_p, d)
    xs = x_sample.reshape(nb_s * seq_s, d)

    n_cond = 1 + nb_s
    cond_rows = -(-n_cond // 8) * 8
    cvec = jnp.concatenate([c_ctx[None], c, jnp.zeros((cond_rows - n_cond, d), F32)], axis=0)

    outs = {}
    for i in range(DEPTH):
        j = i // 2
        mods = ada_modulation(cvec, w_ada[i], b_ada[i])
        mod_p = _pad_mod(mods[0:1], 1)
        mod_s = _pad_mod(mods[1:n_cond], nb_s)
        lg = ln_mix_g[i].reshape(1, d)
        lb = ln_mix_b[i].reshape(1, d)
        if i % 2 == 0:
            w_in = w_in_ab[j].astype(BF16)
            w_out = w_out_ab[j].astype(BF16)
            cw = jnp.pad(conv_w[j], ((0, 8 - CONV_TAPS), (0, 0)))
            qkv_p, gu_p, kv_p = inproj_ab(xp, mod_p, w_in, n_p, True)
            qkv_s, gu_s = inproj_ab(xs, mod_s, w_in, seq_s, False)
            outs["na_k"] = kv_p[:, :NA_WIDTH].reshape(nb_p, 1, seq_p, NA_HEADS, NA_HEAD_DIM)
            outs["na_v"] = kv_p[:, NA_WIDTH:].reshape(nb_p, 1, seq_p, NA_HEADS, NA_HEAD_DIM)
            att_p = ctx_attention(qkv_p, seq_p)
            kc = cache_na_k[:, j].reshape(nb_s * past, NA_WIDTH).astype(BF16)
            vc = cache_na_v[:, j].reshape(nb_s * past, NA_WIDTH).astype(BF16)
            att_s = na_attention(qkv_s, kc, vc, na_bias_table(na_rpb[j]), nb_s, rows)
            xp = outproj_ab(att_p, gu_p, xp, mod_p, w_out, cw, lg, lb, n_p, seq_p)
            xs = outproj_ab(att_s, gu_s, xs, mod_s, w_out, cw, lg, lb, seq_s, seq_s)
        else:
            win, wuq, wk, wv = _mla_weights(w_in_mla[j], w_uq[j], w_ukv[j])
            w_out = w_out_mla[j].astype(BF16)
            qg = q_norm_g[j].reshape(1, Q_LORA)
            kvg = kv_norm_g[j].reshape(1, KV_LORA)
            q_p, ckv_p, slab_p = mla_inproj(xp, mod_p, win, qg, kvg, wuq, None, n_p, seq_p)
            q_s, ckv_s, slab_s = mla_inproj(xs, mod_s, win, qg, kvg, wuq, rope_lane_tables(seq_s), seq_s, seq_s)
            outs["ckv"] = ckv_p.reshape(nb_p, 1, seq_p, KV_LORA)
            outs["kr"] = slab_p[:, QK_NOPE:QK_NOPE + QK_ROPE].reshape(nb_p, 1, seq_p, QK_ROPE)
            k_p, v_p = mla_kv(ckv_p, slab_p, wk, wv)
            k_s, v_s = mla_kv(ckv_s, slab_s, wk, wv)
            ckv_c = cache_mla_ckv[:, j].reshape(nb_s * past, KV_LORA)
            slab_c = jnp.pad(cache_mla_krope[:, j].reshape(nb_s * past, QK_ROPE),
                             ((0, 0), (QK_NOPE, HEAD_SLAB - QK_NOPE - QK_ROPE)))
            k_c, v_c = mla_kv(ckv_c, slab_c, wk, wv)
            att_p = mla_attention(q_p, k_p, v_p, None, None, nb_p)
            att_s = mla_attention(q_s, k_s, v_s, k_c, v_c, nb_s)
            xp = outproj(att_p, xp, mod_p, w_out, lg, lb, n_p)
            xs = outproj(att_s, xs, mod_s, w_out, lg, lb, seq_s)
        wq = peer_w_q[i].astype(BF16)
        sk = peer_subkeys[i].reshape(2 * PEER_HEADS, N_KEYS, PEER_HALF).astype(BF16)
        u = peer_u[i].astype(BF16)
        vt = peer_v[i].T.astype(BF16)
        fg = ln_ffn_g[i].reshape(1, d)
        fb = ln_ffn_b[i].reshape(1, d)
        xp = peer_block(xp, mod_p, wq, sk, u, vt, fg, fb, n_p)
        xs = peer_block(xs, mod_s, wq, sk, u, vt, fg, fb, seq_s)
    return (xp.reshape(nb_p, seq_p, d), xs.reshape(nb_s, seq_s, d),
            outs["na_k"], outs["na_v"], outs["ckv"], outs["kr"])
```

```python
import functools

import jax
import jax.numpy as jnp
import numpy as np
from jax import lax
from jax.experimental import pallas as pl
from jax.experimental.pallas import tpu as pltpu

F32 = jnp.float32
BF16 = jnp.bfloat16

D_MODEL = 1024
DEPTH = 2
GRID_W = 64
N_MOD = 6
ALPHA = (2.0 * DEPTH) ** 0.25
LN_EPS = 1e-5
RMS_EPS = 1e-6
NA_HEADS = 8
NA_HEAD_DIM = 64
NA_WIDTH = NA_HEADS * NA_HEAD_DIM
NA_MAX_ROWS = 8
NA_WIN_COLS = 16
NA_SCALE = NA_HEAD_DIM ** -0.5
CONV_WIDTH = 512
CONV_TAPS = 3
MLA_HEADS = 16
Q_LORA = 256
KV_LORA = 256
QK_NOPE = 64
QK_ROPE = 32
V_HEAD = 64
MLA_SCALE = (QK_NOPE + QK_ROPE) ** -0.5
ROPE_THETA = 10000.0
ROPE_PAIRS_AXIS = QK_ROPE // 4
PEER_HEADS = 8
PEER_HALF = 128
N_KEYS = 128
N_EXPERTS = N_KEYS * N_KEYS
PEER_TOPK = 16

LANES = 128
MOD_ROWS = 8
HEAD_SLAB = 128
MASK_BIAS = -1e30
VMEM_LIMIT = 52 * 1024 * 1024


def _cparams(sem, flags=None):
    return pltpu.CompilerParams(dimension_semantics=sem, vmem_limit_bytes=VMEM_LIMIT, flags=flags)


def _nt_dot(a, b):
    return lax.dot_general(a, b, (((1,), (1,)), ((), ())), preferred_element_type=F32)


def _dot(a, b):
    return jnp.dot(a, b, preferred_element_type=F32)


def _split_bf16(a):
    hi = a.astype(BF16)
    lo = (a - hi.astype(F32)).astype(BF16)
    return hi, lo


def _layer_norm(z, g, b):
    mu = jnp.mean(z, axis=-1, keepdims=True)
    zc = z - mu
    var = jnp.mean(zc * zc, axis=-1, keepdims=True)
    return zc * lax.rsqrt(var + LN_EPS) * g + b


LOG2E = 1.4426950408889634
LN2 = 0.6931471805599453
_GELU_A = -2.0 * LOG2E * 0.7978845608028654
_GELU_B = _GELU_A * 0.044715


def _gelu_tanh(x):
    return x / (1.0 + jnp.exp2(x * (_GELU_A + _GELU_B * (x * x))))


def _ada_kernel(c_ref, w_ref, b_ref, o_ref):
    c = c_ref[...]
    a = c * (1.0 / (1.0 + jnp.exp(-c)))
    a_hi, a_lo = _split_bf16(a)
    w_hi, w_lo = _split_bf16(w_ref[...])
    o_ref[...] = _dot(a_hi, w_hi) + _dot(a_hi, w_lo) + _dot(a_lo, w_hi) + b_ref[...]


def ada_modulation(cvec, w_ada, b_ada, tn=512):
    r, d = cvec.shape
    n = w_ada.shape[1]
    return pl.pallas_call(
        _ada_kernel,
        grid=(n // tn,),
        in_specs=[pl.BlockSpec((r, d), lambda j: (0, 0)),
                  pl.BlockSpec((d, tn), lambda j: (0, j)),
                  pl.BlockSpec((1, tn), lambda j: (0, j))],
        out_specs=pl.BlockSpec((r, tn), lambda j: (0, j)),
        out_shape=jax.ShapeDtypeStruct((r, n), F32),
        compiler_params=_cparams(("arbitrary",)),
        name="ada_modulation",
    )(cvec, w_ada, b_ada.reshape(1, n))


def _inproj_ab_kernel(x_ref, mod_ref, w_ref, qkv_ref, gu_ref, *kv_ref):
    m = mod_ref[0]
    h = x_ref[...] * (1.0 + m[1:2]) + m[0:1]
    y = _dot(h.astype(BF16), w_ref[...])
    w3 = 3 * NA_WIDTH
    qkv_ref[:, :NA_WIDTH] = (y[:, :NA_WIDTH] * (NA_SCALE * LOG2E)).astype(BF16)
    qkv_ref[:, NA_WIDTH:] = y[:, NA_WIDTH:w3].astype(BF16)
    if kv_ref:
        kv_ref[0][...] = y[:, NA_WIDTH:w3]
    gu_ref[:, :CONV_WIDTH] = y[:, w3:w3 + CONV_WIDTH].astype(BF16)
    gu_ref[:, CONV_WIDTH:] = (y[:, w3 + CONV_WIDTH:w3 + 2 * CONV_WIDTH]
                              * y[:, w3 + 2 * CONV_WIDTH:]).astype(BF16)


def inproj_ab(x, mod, w_bf16, tokens_per_group, emit_kv, tm=512):
    n, d = x.shape
    tm = min(tm, n)
    out_shape = [jax.ShapeDtypeStruct((n, 3 * NA_WIDTH), BF16),
                 jax.ShapeDtypeStruct((n, 2 * CONV_WIDTH), BF16)]
    out_specs = [pl.BlockSpec((tm, 3 * NA_WIDTH), lambda i: (i, 0)),
                 pl.BlockSpec((tm, 2 * CONV_WIDTH), lambda i: (i, 0))]
    if emit_kv:
        out_shape.append(jax.ShapeDtypeStruct((n, 2 * NA_WIDTH), F32))
        out_specs.append(pl.BlockSpec((tm, 2 * NA_WIDTH), lambda i: (i, 0)))
    return pl.pallas_call(
        _inproj_ab_kernel,
        grid=(n // tm,),
        in_specs=[pl.BlockSpec((tm, d), lambda i: (i, 0)),
                  pl.BlockSpec((1, MOD_ROWS, d), lambda i: ((i * tm) // tokens_per_group, 0, 0)),
                  pl.BlockSpec(w_bf16.shape, lambda i: (0, 0))],
        out_specs=out_specs,
        out_shape=out_shape,
        compiler_params=_cparams(("parallel",)),
        name="inproj_ab",
    )(x, mod, w_bf16)


def _pair_attention(instances):
    heads = []
    for n, (q_pair, _, _, _) in enumerate(instances):
        lane = lax.broadcasted_iota(jnp.int32, q_pair.shape, 1)
        qf = q_pair.astype(F32)
        for j in range(2):
            in_head = (lane >= j * NA_HEAD_DIM) & (lane < (j + 1) * NA_HEAD_DIM)
            heads.append((n, j, jnp.where(in_head, qf, 0.0).astype(BF16)))
    scores = []
    for n, j, qm in heads:
        _, keys, _, biases = instances[n]
        ss = [_nt_dot(qm, k) for k in keys]
        scores.append([s if b is None else s + b for s, b in zip(ss, biases[j])])
    mxs = [functools.reduce(jnp.maximum, [jnp.max(s, axis=-1, keepdims=True) for s in ss]) for ss in scores]
    ess = [[jnp.exp2(s - mx) for s in ss] for ss, mx in zip(scores, mxs)]
    dens = [functools.reduce(jnp.add, [jnp.sum(e, axis=-1, keepdims=True) for e in es]) for es in ess]
    outs = []
    for (n, j, _), es in zip(heads, ess):
        values = instances[n][2]
        outs.append(functools.reduce(jnp.add, [_dot(e.astype(BF16), v) for e, v in zip(es, values)]))
    outs = [o / d for o, d in zip(outs, dens)]
    lane = lax.broadcasted_iota(jnp.int32, outs[0].shape, 1)
    return [jnp.where(lane < NA_HEAD_DIM, outs[2 * n], outs[2 * n + 1]) for n in range(len(instances))]


NA_ROWS_PER_STEP = 2


def _na_attn_kernel(q_ref, k_ref, v_ref, kc_ref, vc_ref, *rest, rows):
    bias_refs = rest[:NA_ROWS_PER_STEP]
    o_ref = rest[NA_ROWS_PER_STEP]
    band = NA_MAX_ROWS * GRID_W
    instances, places = [], []
    for rr in range(NA_ROWS_PER_STEP):
        r = pl.program_id(1) * NA_ROWS_PER_STEP + rr
        rs = jnp.clip(r - NA_MAX_ROWS // 2, 0, rows - NA_MAX_ROWS)
        start = pl.multiple_of(rs * GRID_W, GRID_W)
        qrows = slice(rr * GRID_W, (rr + 1) * GRID_W)
        for p in range(NA_HEADS // 2):
            sl = slice(p * LANES, (p + 1) * LANES)
            keys = [k_ref[pl.ds(start, band), sl], kc_ref[:, sl]]
            values = [v_ref[pl.ds(start, band), sl], vc_ref[:, sl]]
            biases = [[bias_refs[rr][0, 2 * p + j], None] for j in range(2)]
            instances.append((q_ref[qrows, sl], keys, values, biases))
            places.append((qrows, sl))
    for (qrows, sl), o in zip(places, _pair_attention(instances)):
        o_ref[qrows, sl] = o.astype(BF16)


def na_attention(qkv, kc, vc, bias, n_batch, rows):
    n = qkv.shape[0]
    seq = rows * GRID_W
    lc = kc.shape[0] // n_batch
    half = NA_MAX_ROWS // 2

    steps = rows // NA_ROWS_PER_STEP
    tq = NA_ROWS_PER_STEP * GRID_W

    def bias_spec(rr):
        def idx(b, g):
            r = g * NA_ROWS_PER_STEP + rr
            return (r - jnp.clip(r - half, 0, rows - NA_MAX_ROWS), 0, 0, 0)
        return pl.BlockSpec((1, NA_HEADS, GRID_W, NA_MAX_ROWS * GRID_W), idx)

    return pl.pallas_call(
        functools.partial(_na_attn_kernel, rows=rows),
        grid=(n_batch, steps),
        in_specs=[pl.BlockSpec((tq, NA_WIDTH), lambda b, g: (b * steps + g, 0)),
                  pl.BlockSpec((seq, NA_WIDTH), lambda b, g: (b, 1)),
                  pl.BlockSpec((seq, NA_WIDTH), lambda b, g: (b, 2)),
                  pl.BlockSpec((lc, NA_WIDTH), lambda b, g: (b, 0)),
                  pl.BlockSpec((lc, NA_WIDTH), lambda b, g: (b, 0))]
        + [bias_spec(rr) for rr in range(NA_ROWS_PER_STEP)],
        out_specs=pl.BlockSpec((tq, NA_WIDTH), lambda b, g: (b * steps + g, 0)),
        out_shape=jax.ShapeDtypeStruct((n, NA_WIDTH), BF16),
        compiler_params=_cparams(("parallel", "arbitrary")),
        name="na_attention",
    )(qkv, qkv, qkv, kc, vc, *([bias] * NA_ROWS_PER_STEP))


def na_bias_table(rpb):
    c = np.arange(GRID_W)[:, None]
    kc = np.arange(GRID_W)[None, :]
    cs = np.clip(c - NA_WIN_COLS // 2, 0, GRID_W - NA_WIN_COLS)
    valid = (kc >= cs) & (kc < cs + NA_WIN_COLS)
    dc = kc - c + (NA_WIN_COLS - 1)
    cc, kk = np.nonzero(valid)
    place = np.zeros((2 * NA_WIN_COLS - 1, GRID_W, GRID_W), np.float32)
    place[dc[cc, kk], cc, kk] = 1.0
    t = jnp.einsum("hrd,dck->hrck", rpb, place, precision=lax.Precision.HIGHEST)
    t = jnp.where(valid[None, None], t * LOG2E, MASK_BIAS)
    bands = [t[:, NA_MAX_ROWS - 1 - off:2 * NA_MAX_ROWS - 1 - off] for off in range(NA_MAX_ROWS)]
    b = jnp.stack(bands, axis=0)
    b = jnp.transpose(b, (0, 1, 3, 2, 4))
    return b.reshape(NA_MAX_ROWS, NA_HEADS, GRID_W, NA_MAX_ROWS * GRID_W).astype(F32)


def _ctx_attn_kernel(q_ref, k_ref, v_ref, o_ref):
    slabs = [slice(p * LANES, (p + 1) * LANES) for p in range(NA_HEADS // 2)]
    instances = [(q_ref[:, sl], [k_ref[:, sl]], [v_ref[:, sl]], [[None], [None]]) for sl in slabs]
    for sl, o in zip(slabs, _pair_attention(instances)):
        o_ref[:, sl] = o.astype(BF16)


def ctx_attention(qkv, seq):
    n = qkv.shape[0]
    return pl.pallas_call(
        _ctx_attn_kernel,
        grid=(n // seq,),
        in_specs=[pl.BlockSpec((seq, NA_WIDTH), lambda b: (b, 0)),
                  pl.BlockSpec((seq, NA_WIDTH), lambda b: (b, 1)),
                  pl.BlockSpec((seq, NA_WIDTH), lambda b: (b, 2))],
        out_specs=pl.BlockSpec((seq, NA_WIDTH), lambda b: (b, 0)),
        out_shape=jax.ShapeDtypeStruct((n, NA_WIDTH), BF16),
        compiler_params=_cparams(("parallel",)),
        name="ctx_attention",
    )(qkv, qkv, qkv)


HALO = 16


def _outproj_ab_kernel(att_ref, gu_ref, prev_ref, next_ref, x_ref, mod_ref, w_ref, cw_ref, g_ref, b_ref,
                       o_ref, *, tiles_per_seq):
    i = pl.program_id(0)
    tm = x_ref.shape[0]
    m = mod_ref[0]
    gu = gu_ref[...].astype(F32)
    bg = gu[:, :CONV_WIDTH]
    u = gu[:, CONV_WIDTH:]
    has_prev = (i % tiles_per_seq != 0).astype(F32)
    has_next = (i % tiles_per_seq != tiles_per_seq - 1).astype(F32)
    prev_row = prev_ref[HALO - 1:HALO, CONV_WIDTH:].astype(F32) * has_prev
    next_row = next_ref[0:1, CONV_WIDTH:].astype(F32) * has_next
    row = lax.broadcasted_iota(jnp.int32, u.shape, 0)
    u_m1 = jnp.where(row == 0, prev_row, pltpu.roll(u, 1, axis=0))
    u_p1 = jnp.where(row == tm - 1, next_row, pltpu.roll(u, tm - 1, axis=0))
    cw = cw_ref[...]
    conv = bg * (u_m1 * cw[0:1] + u * cw[1:2] + u_p1 * cw[2:3])
    y = _dot(att_ref[...], w_ref[:NA_WIDTH, :]) + _dot(conv.astype(BF16), w_ref[NA_WIDTH:, :])
    z = ALPHA * x_ref[...] + m[2:3] * y
    o_ref[...] = _layer_norm(z, g_ref[...], b_ref[...])


def outproj_ab(att, gu, x, mod, w_bf16, conv_w8, ln_g, ln_b, tokens_per_group, seq, tm=256):
    n, d = x.shape
    tm = min(tm, seq)
    hb = tm // HALO
    last_halo = n // HALO - 1
    return pl.pallas_call(
        functools.partial(_outproj_ab_kernel, tiles_per_seq=seq // tm),
        grid=(n // tm,),
        in_specs=[pl.BlockSpec((tm, NA_WIDTH), lambda i: (i, 0)),
                  pl.BlockSpec((tm, 2 * CONV_WIDTH), lambda i: (i, 0)),
                  pl.BlockSpec((HALO, 2 * CONV_WIDTH), lambda i: (jnp.maximum(i * hb - 1, 0), 0)),
                  pl.BlockSpec((HALO, 2 * CONV_WIDTH), lambda i: (jnp.minimum((i + 1) * hb, last_halo), 0)),
                  pl.BlockSpec((tm, d), lambda i: (i, 0)),
                  pl.BlockSpec((1, MOD_ROWS, d), lambda i: ((i * tm) // tokens_per_group, 0, 0)),
                  pl.BlockSpec(w_bf16.shape, lambda i: (0, 0)),
                  pl.BlockSpec(conv_w8.shape, lambda i: (0, 0)),
                  pl.BlockSpec((1, d), lambda i: (0, 0)),
                  pl.BlockSpec((1, d), lambda i: (0, 0))],
        out_specs=pl.BlockSpec((tm, d), lambda i: (i, 0)),
        out_shape=jax.ShapeDtypeStruct((n, d), F32),
        compiler_params=_cparams(("parallel",)),
        name="outproj_ab",
    )(att, gu, gu, gu, x, mod, w_bf16, conv_w8, ln_g, ln_b)


def _outproj_kernel(att_ref, x_ref, mod_ref, w_ref, g_ref, b_ref, o_ref):
    m = mod_ref[0]
    y = _dot(att_ref[...], w_ref[...])
    z = ALPHA * x_ref[...] + m[2:3] * y
    o_ref[...] = _layer_norm(z, g_ref[...], b_ref[...])


def outproj(att, x, mod, w_bf16, ln_g, ln_b, tokens_per_group, tm=512):
    n, d = x.shape
    k = att.shape[1]
    return pl.pallas_call(
        _outproj_kernel,
        grid=(n // tm,),
        in_specs=[pl.BlockSpec((tm, k), lambda i: (i, 0)),
                  pl.BlockSpec((tm, d), lambda i: (i, 0)),
                  pl.BlockSpec((1, MOD_ROWS, d), lambda i: ((i * tm) // tokens_per_group, 0, 0)),
                  pl.BlockSpec(w_bf16.shape, lambda i: (0, 0)),
                  pl.BlockSpec((1, d), lambda i: (0, 0)),
                  pl.BlockSpec((1, d), lambda i: (0, 0))],
        out_specs=pl.BlockSpec((tm, d), lambda i: (i, 0)),
        out_shape=jax.ShapeDtypeStruct((n, d), F32),
        compiler_params=_cparams(("parallel",)),
        name="outproj",
    )(att, x, mod, w_bf16, ln_g, ln_b)


def _peer_scores_kernel(x_ref, mod_ref, wq_ref, sk_ref, hm_ref, st_ref):
    m = mod_ref[0]
    h = x_ref[...] * (1.0 + m[4:5]) + m[3:4]
    hb = h.astype(BF16)
    hm_ref[...] = h.T.astype(BF16)
    q = _dot(hb, wq_ref[...])
    for c in range(2 * PEER_HEADS):
        qc = q[:, c * PEER_HALF:(c + 1) * PEER_HALF].astype(BF16)
        st_ref[c] = _nt_dot(sk_ref[c], qc) * LOG2E


def peer_scores(x, mod, wq_bf16, subkeys_bf16, tokens_per_group, tm=512):
    n, d = x.shape
    nc = 2 * PEER_HEADS
    return pl.pallas_call(
        _peer_scores_kernel,
        grid=(n // tm,),
        in_specs=[pl.BlockSpec((tm, d), lambda i: (i, 0)),
                  pl.BlockSpec((1, MOD_ROWS, d), lambda i: ((i * tm) // tokens_per_group, 0, 0)),
                  pl.BlockSpec(wq_bf16.shape, lambda i: (0, 0)),
                  pl.BlockSpec(subkeys_bf16.shape, lambda i: (0, 0, 0))],
        out_specs=[pl.BlockSpec((d, tm), lambda i: (0, i)),
                   pl.BlockSpec((nc, N_KEYS, tm), lambda i: (0, 0, i))],
        out_shape=[jax.ShapeDtypeStruct((d, n), BF16),
                   jax.ShapeDtypeStruct((nc, N_KEYS, n), F32)],
        compiler_params=_cparams(("parallel",)),
        name="peer_scores",
    )(x, mod, wq_bf16, subkeys_bf16)


SUBLANES = 8
N_SLABS = N_KEYS // SUBLANES


def _sort_network(n):
    pairs = []
    p = 1
    while p < n:
        k = p
        while k >= 1:
            for j in range(k % p, n - k, 2 * k):
                for i in range(min(k, n - j - k)):
                    if (i + j) // (2 * p) == (i + j + k) // (2 * p):
                        pairs.append((i + j, i + j + k))
            k //= 2
        p *= 2
    return pairs


_SLAB_SORT = _sort_network(N_SLABS)


def _top_values(s_ref, c, n_top):
    return _top_of_slabs([s_ref[c, r] for r in range(N_SLABS)], n_top)


def _top_of_slabs(slabs, n_top):
    x = list(slabs)
    n = len(x)
    for a, b in _SLAB_SORT:
        if b < n:
            x[a], x[b] = jnp.maximum(x[a], x[b]), jnp.minimum(x[a], x[b])
    sub = lax.broadcasted_iota(jnp.int32, x[0].shape, 0).astype(F32)
    x.append(jnp.full(x[0].shape, -jnp.inf, F32))
    vals = []
    for k in range(n_top):
        mx = jnp.max(x[0], axis=0, keepdims=True)
        vals.append(mx)
        if k == n_top - 1:
            break
        first = jnp.min(jnp.where(x[0] == mx, sub, float(SUBLANES)), axis=0, keepdims=True)
        pop = sub == first
        for r in range(min(n, n_top - 1 - k)):
            x[r] = jnp.where(pop, x[r + 1], x[r])
    return vals


def _rows_to_slab(rows_list):
    t = rows_list[0].shape[1]
    ridx = lax.broadcasted_iota(jnp.int32, (SUBLANES, t), 0)
    out = jnp.full((SUBLANES, t), -jnp.inf, F32)
    for r, row in enumerate(rows_list):
        out = jnp.where(ridx == r, row, out)
    return out


def _peer_select_kernel(st_ref, thr_ref, lse_ref, m2_ref):
    k = PEER_TOPK
    for h in range(PEER_HEADS):
        v1 = _top_values(st_ref, 2 * h, k + 1)
        v2 = _top_values(st_ref, 2 * h + 1, k + 1)
        v1_lo, v1_hi = _rows_to_slab(v1[0:8]), _rows_to_slab(v1[8:16])
        v2_lo, v2_hi = _rows_to_slab(v2[0:8]), _rows_to_slab(v2[8:16])
        ridx = lax.broadcasted_iota(jnp.int32, v1_lo.shape, 0)
        v1_mid = jnp.where(ridx >= 4, v1_lo, -jnp.inf)
        cands = [v1[0] + v2_lo, v1[0] + v2_hi, v1[1] + v2_lo, v1[2] + v2_lo, v1[3] + v2_lo,
                 v1_hi + v2[0], v1_mid + v2[0], v1_mid + v2[1], v1_mid + v2[2],
                 _rows_to_slab([v1[k] + v2[0], v1[0] + v2[k]])]
        top = v1[0] + v2[0]
        best = _top_of_slabs(cands, k + 1)
        thr, nxt = best[k - 1], best[k]
        z = functools.reduce(jnp.add, [jnp.where(cd >= thr, jnp.exp2(cd - top), 0.0) for cd in cands])
        z = jnp.sum(z, axis=0, keepdims=True)
        thr_ref[h:h + 1, :] = 0.5 * (thr + nxt)
        lse_ref[h:h + 1, :] = v1[0] + jnp.log(z) * LOG2E
        m2_ref[h:h + 1, :] = v2[0]


def peer_select(st, tl=256):
    nc, _, _, n = st.shape
    stat = pl.BlockSpec((PEER_HEADS, tl), lambda i: (0, i))
    return pl.pallas_call(
        _peer_select_kernel,
        grid=(n // tl,),
        in_specs=[pl.BlockSpec((nc, N_SLABS, SUBLANES, tl), lambda i: (0, 0, 0, i))],
        out_specs=[stat, stat, stat],
        out_shape=[jax.ShapeDtypeStruct((PEER_HEADS, n), F32)] * 3,
        compiler_params=_cparams(("parallel",)),
        name="peer_select",
    )(st)


PEER_LANE_CHUNK = 128
PEER_MXU_CHUNK = 256
PEER_MXU_ROWS = 256


def _peer_dense_kernel(hm_ref, u_ref, vt_ref, st_ref, thr_ref, lse_ref, m2_ref, x_ref, mod_ref, g_ref, b_ref,
                       o_ref, acc_ref, ht_ref, at_ref, e2_ref, cut_rep, f1_rep, *, gate_row):
    e = pl.program_id(1)
    tm = hm_ref.shape[1]
    n_mxu_chunks = tm // PEER_MXU_CHUNK
    row_blocks = u_ref.shape[0] // PEER_MXU_ROWS

    @pl.when(e == 0)
    def _():
        acc_ref[...] = jnp.zeros_like(acc_ref)
        for h in range(PEER_HEADS):
            e2_ref[h] = jnp.exp2(st_ref[2 * h + 1] - m2_ref[h:h + 1, :])

    def first_matmul(cm, rb):
        ms = slice(cm * PEER_MXU_CHUNK, (cm + 1) * PEER_MXU_CHUNK)
        rs = slice(rb * PEER_MXU_ROWS, (rb + 1) * PEER_MXU_ROWS)
        ht_ref[rs, ms] = _dot(u_ref[rs, :], hm_ref[:, ms])


    for h in range(PEER_HEADS):
        s1 = st_ref[2 * h, e]
        cut_rows = thr_ref[h:h + 1, :] - s1
        f1_rows = jnp.exp2(s1 - lse_ref[h:h + 1, :])
        for ii in range(SUBLANES):
            cut_rep[h * SUBLANES + ii] = jnp.broadcast_to(cut_rows[ii:ii + 1, :], (SUBLANES, tm))
            f1_rep[h * SUBLANES + ii] = jnp.broadcast_to(f1_rows[ii:ii + 1, :], (SUBLANES, tm))
    def second_matmul(cm, kb):
        ms = slice(cm * PEER_MXU_CHUNK, (cm + 1) * PEER_MXU_CHUNK)
        ks = slice(kb * (te // 2), (kb + 1) * (te // 2))
        acc_ref[:, ms] += _dot(vt_ref[:, ks], at_ref[ks, ms])

    lanes_per_mxu = PEER_MXU_CHUNK // PEER_LANE_CHUNK
    te = u_ref.shape[0]
    for cm in range(n_mxu_chunks):
        units = [(ii, c) for ii in range(SUBLANES) for c in range(cm * lanes_per_mxu, (cm + 1) * lanes_per_mxu)]
        per_block = len(units) // row_blocks
        pieces = [functools.partial(first_matmul, cm, rb) for rb in range(row_blocks)] if cm == 0 else []
        if cm + 1 < n_mxu_chunks:
            pieces += [functools.partial(first_matmul, cm + 1, rb) for rb in range(row_blocks)]
        lead = per_block // 2 if cm == 0 else per_block
        for idx, (ii, c) in enumerate(units):
            if idx % lead == 0 and idx // lead < len(pieces):
                pieces[idx // lead]()
            if idx == len(units) // 2:
                second_matmul(cm, 0)
            rows = slice(ii * N_KEYS, (ii + 1) * N_KEYS)
            ls = slice(c * PEER_LANE_CHUNK, (c + 1) * PEER_LANE_CHUNK)
            w = jnp.zeros((N_SLABS, SUBLANES, PEER_LANE_CHUNK), F32)
            for h in range(PEER_HEADS):
                sel = st_ref[2 * h + 1, :, :, ls] >= cut_rep[h * SUBLANES + ii, :, ls][None]
                w = w + jnp.where(sel, e2_ref[h, :, :, ls] * f1_rep[h * SUBLANES + ii, :, ls][None], 0.0)
            act = _gelu_tanh(ht_ref[rows, ls])
            at_ref[rows, ls] = (act * w.reshape(N_KEYS, PEER_LANE_CHUNK)).astype(BF16)
        second_matmul(cm, 1)

    @pl.when(e == pl.num_programs(1) - 1)
    def _():
        m = mod_ref[0]
        z = ALPHA * x_ref[...] + m[gate_row:gate_row + 1] * acc_ref[...].T
        o_ref[...] = _layer_norm(z, g_ref[...], b_ref[...])


def peer_dense(hm, u_bf16, vt_bf16, st, thr, lse, m2, x, mod, ln_g, ln_b, tokens_per_group, tm=256, te=1024):
    n, d = x.shape
    ne = u_bf16.shape[0]
    nc = st.shape[0]
    assert te == N_KEYS * SUBLANES
    return pl.pallas_call(
        functools.partial(_peer_dense_kernel, gate_row=5),
        grid=(n // tm, ne // te),
        in_specs=[pl.BlockSpec((d, tm), lambda i, g: (0, i)),
                  pl.BlockSpec((te, d), lambda i, g: (g, 0)),
                  pl.BlockSpec((d, te), lambda i, g: (0, g)),
                  pl.BlockSpec((nc, N_SLABS, SUBLANES, tm), lambda i, g: (0, 0, 0, i)),
                  pl.BlockSpec((PEER_HEADS, tm), lambda i, g: (0, i)),
                  pl.BlockSpec((PEER_HEADS, tm), lambda i, g: (0, i)),
                  pl.BlockSpec((PEER_HEADS, tm), lambda i, g: (0, i)),
                  pl.BlockSpec((tm, d), lambda i, g: (i, 0)),
                  pl.BlockSpec((1, MOD_ROWS, d), lambda i, g: ((i * tm) // tokens_per_group, 0, 0)),
                  pl.BlockSpec((1, d), lambda i, g: (0, 0)),
                  pl.BlockSpec((1, d), lambda i, g: (0, 0))],
        out_specs=pl.BlockSpec((tm, d), lambda i, g: (i, 0)),
        out_shape=jax.ShapeDtypeStruct((n, d), F32),
        scratch_shapes=[pltpu.VMEM((d, tm), F32),
                        pltpu.VMEM((te, tm), F32),
                        pltpu.VMEM((te, tm), BF16),
                        pltpu.VMEM((PEER_HEADS, N_SLABS, SUBLANES, tm), F32),
                        pltpu.VMEM((PEER_HEADS * SUBLANES, SUBLANES, tm), F32),
                        pltpu.VMEM((PEER_HEADS * SUBLANES, SUBLANES, tm), F32)],
        compiler_params=_cparams(("parallel", "arbitrary")),
        name="peer_dense",
    )(hm, u_bf16, vt_bf16, st, thr, lse, m2, x, mod, ln_g, ln_b)


def peer_block(x, mod, wq, sk, u, vt, ln_g, ln_b, tokens_per_group):
    hm, st = peer_scores(x, mod, wq, sk, tokens_per_group)
    st = st.reshape(2 * PEER_HEADS, N_SLABS, SUBLANES, x.shape[0])
    cut, lse, m2 = peer_select(st)
    return peer_dense(hm, u, vt, st, cut, lse, m2, x, mod, ln_g, ln_b, tokens_per_group)


def _rope_slab(t, cos_t, sin_fwd, sin_bwd):
    return t * cos_t + pltpu.roll(t, QK_ROPE // 2, axis=1) * sin_fwd \
        + pltpu.roll(t, HEAD_SLAB - QK_ROPE // 2, axis=1) * sin_bwd


def _mla_inproj_kernel(x_ref, mod_ref, win_ref, qg_ref, kvg_ref, wuq_ref, *rest, use_rope):
    if use_rope:
        cos_ref, sf_ref, sb_ref, q_ref, ckv_ref, slab_ref = rest
    else:
        q_ref, ckv_ref, slab_ref = rest
    m = mod_ref[0]
    h = x_ref[...] * (1.0 + m[1:2]) + m[0:1]
    y = _dot(h.astype(BF16), win_ref[...])
    cq = y[:, :Q_LORA]
    ckv = y[:, Q_LORA:Q_LORA + KV_LORA]
    slab = y[:, Q_LORA + KV_LORA:]
    cqn = cq * lax.rsqrt(jnp.mean(cq * cq, axis=-1, keepdims=True) + RMS_EPS) * qg_ref[...]
    ckv_ref[...] = ckv * lax.rsqrt(jnp.mean(ckv * ckv, axis=-1, keepdims=True) + RMS_EPS) * kvg_ref[...]
    q = _dot(cqn.astype(BF16), wuq_ref[...]) * (MLA_SCALE * LOG2E)
    if use_rope:
        cos_t, sf, sb = cos_ref[...], sf_ref[...], sb_ref[...]
        slab = _rope_slab(slab, cos_t, sf, sb)
        for hh in range(MLA_HEADS):
            sl = slice(hh * HEAD_SLAB, (hh + 1) * HEAD_SLAB)
            q_ref[:, sl] = _rope_slab(q[:, sl], cos_t, sf, sb).astype(BF16)
    else:
        q_ref[...] = q.astype(BF16)
    slab_ref[...] = slab


def mla_inproj(x, mod, win_bf16, q_g, kv_g, wuq_bf16, rope, tokens_per_group, seq, tm=256):
    n, d = x.shape
    use_rope = rope is not None
    in_specs = [pl.BlockSpec((tm, d), lambda i: (i, 0)),
                pl.BlockSpec((1, MOD_ROWS, d), lambda i: ((i * tm) // tokens_per_group, 0, 0)),
                pl.BlockSpec(win_bf16.shape, lambda i: (0, 0)),
                pl.BlockSpec((1, Q_LORA), lambda i: (0, 0)),
                pl.BlockSpec((1, KV_LORA), lambda i: (0, 0)),
                pl.BlockSpec(wuq_bf16.shape, lambda i: (0, 0))]
    args = [x, mod, win_bf16, q_g, kv_g, wuq_bf16]
    if use_rope:
        tps = seq // tm
        in_specs += [pl.BlockSpec((tm, HEAD_SLAB), lambda i: (i % tps, 0))] * 3
        args += list(rope)
    return pl.pallas_call(
        functools.partial(_mla_inproj_kernel, use_rope=use_rope),
        grid=(n // tm,),
        in_specs=in_specs,
        out_specs=[pl.BlockSpec((tm, MLA_HEADS * HEAD_SLAB), lambda i: (i, 0)),
                   pl.BlockSpec((tm, KV_LORA), lambda i: (i, 0)),
                   pl.BlockSpec((tm, HEAD_SLAB), lambda i: (i, 0))],
        out_shape=[jax.ShapeDtypeStruct((n, MLA_HEADS * HEAD_SLAB), BF16),
                   jax.ShapeDtypeStruct((n, KV_LORA), F32),
                   jax.ShapeDtypeStruct((n, HEAD_SLAB), F32)],
        compiler_params=_cparams(("parallel",)),
        name="mla_inproj",
    )(*args)


def _mla_kv_kernel(ckv_ref, slab_ref, wk_ref, wv_ref, k_ref, v_ref):
    c = ckv_ref[...].astype(BF16)
    kn = _dot(c, wk_ref[...])
    slab = slab_ref[...]
    for hh in range(MLA_HEADS):
        sl = slice(hh * HEAD_SLAB, (hh + 1) * HEAD_SLAB)
        k_ref[:, sl] = (kn[:, sl] + slab).astype(BF16)
    v = _dot(c, wv_ref[...])
    lane = lax.broadcasted_iota(jnp.int32, (v.shape[0], LANES), 1)
    for p in range(MLA_HEADS // 2):
        pair = v[:, p * LANES:(p + 1) * LANES]
        v_ref[:, (2 * p) * LANES:(2 * p + 1) * LANES] = jnp.where(lane < V_HEAD, pair, 1.0).astype(BF16)
        v_ref[:, (2 * p + 1) * LANES:(2 * p + 2) * LANES] = jnp.where(lane < V_HEAD, 1.0, pair).astype(BF16)


def mla_kv(ckv, slab, wk_bf16, wv_bf16, tm=256):
    n = ckv.shape[0]
    tm = min(tm, n)
    return pl.pallas_call(
        _mla_kv_kernel,
        grid=(n // tm,),
        in_specs=[pl.BlockSpec((tm, KV_LORA), lambda i: (i, 0)),
                  pl.BlockSpec((tm, HEAD_SLAB), lambda i: (i, 0)),
                  pl.BlockSpec(wk_bf16.shape, lambda i: (0, 0)),
                  pl.BlockSpec(wv_bf16.shape, lambda i: (0, 0))],
        out_specs=[pl.BlockSpec((tm, MLA_HEADS * HEAD_SLAB), lambda i: (i, 0)),
                   pl.BlockSpec((tm, MLA_HEADS * LANES), lambda i: (i, 0))],
        out_shape=[jax.ShapeDtypeStruct((n, MLA_HEADS * HEAD_SLAB), BF16),
                   jax.ShapeDtypeStruct((n, MLA_HEADS * LANES), BF16)],
        compiler_params=_cparams(("parallel",)),
        name="mla_kv",
    )(ckv, slab, wk_bf16, wv_bf16)


MLA_KEY_CHUNK = 256


def _mla_attn_kernel(q_ref, *rest, n_sources):
    k_refs = rest[:n_sources]
    v_refs = rest[n_sources:2 * n_sources]
    o_ref = rest[2 * n_sources]
    tq = q_ref.shape[0]
    lane = lax.broadcasted_iota(jnp.int32, (tq, LANES), 1)
    sls = [slice(j * HEAD_SLAB, (j + 1) * HEAD_SLAB) for j in range(2)]
    qs = [q_ref[:, sl] for sl in sls]
    chunks = [(si, st) for si, k in enumerate(k_refs) for st in range(0, k.shape[0], MLA_KEY_CHUNK)]

    def score(j, ch):
        si, st = ch
        return _nt_dot(qs[j], k_refs[si][st:st + MLA_KEY_CHUNK, sls[j]])

    def row_max(ss):
        return jnp.max(functools.reduce(jnp.maximum, ss), axis=-1, keepdims=True)

    def weighted(j, ch, p):
        si, st = ch
        return _dot(p, v_refs[si][st:st + MLA_KEY_CHUNK, sls[j]])

    s0 = [score(0, ch) for ch in chunks]
    m0 = row_max(s0)
    s1, p0 = [], []
    for c, ch in enumerate(chunks):
        s1.append(score(1, ch))
        p0.append(jnp.exp2(s0[c] - m0).astype(BF16))
    m1 = row_max(s1)
    o0, p1 = None, []
    for c, ch in enumerate(chunks):
        t = weighted(0, ch, p0[c])
        o0 = t if o0 is None else o0 + t
        p1.append(jnp.exp2(s1[c] - m1).astype(BF16))
    o1 = functools.reduce(jnp.add, [weighted(1, ch, p1[c]) for c, ch in enumerate(chunks)])
    outs = [o / pltpu.roll(o, V_HEAD, axis=1) for o in (o0, o1)]
    o_ref[...] = jnp.where(lane < V_HEAD, outs[0], outs[1]).astype(BF16)


def mla_attention(q, k_new, v_new, k_ctx, v_ctx, n_batch, tq=512):
    n = q.shape[0]
    seq = n // n_batch
    tq = min(tq, seq)
    qt = seq // tq
    pairs = MLA_HEADS // 2
    srcs_k, srcs_v = [k_new], [v_new]
    lens = [seq]
    if k_ctx is not None:
        srcs_k.append(k_ctx)
        srcs_v.append(v_ctx)
        lens.append(k_ctx.shape[0] // n_batch)
    in_specs = [pl.BlockSpec((tq, 2 * HEAD_SLAB), lambda b, p, t: (b * qt + t, p))]
    in_specs += [pl.BlockSpec((ln, 2 * HEAD_SLAB), lambda b, p, t: (b, p)) for ln in lens]
    in_specs += [pl.BlockSpec((ln, 2 * LANES), lambda b, p, t: (b, p)) for ln in lens]
    return pl.pallas_call(
        functools.partial(_mla_attn_kernel, n_sources=len(lens)),
        grid=(n_batch, pairs, qt),
        in_specs=in_specs,
        out_specs=pl.BlockSpec((tq, 2 * V_HEAD), lambda b, p, t: (b * qt + t, p)),
        out_shape=jax.ShapeDtypeStruct((n, MLA_HEADS * V_HEAD), BF16),
        compiler_params=_cparams(("parallel", "parallel", "arbitrary")),
        name="mla_attention",
    )(q, *srcs_k, *srcs_v)


def rope_lane_tables(n):
    t = jnp.arange(n)
    row = (t // GRID_W).astype(F32)
    col = (t % GRID_W).astype(F32)
    inv = ROPE_THETA ** (-jnp.arange(ROPE_PAIRS_AXIS, dtype=F32) / ROPE_PAIRS_AXIS)
    ang = jnp.concatenate([row[:, None] * inv, col[:, None] * inv], axis=-1)
    cos, sin = jnp.cos(ang), jnp.sin(ang)
    ones = jnp.ones((n, QK_NOPE), F32)
    zeros = jnp.zeros((n, QK_NOPE), F32)
    tail1 = jnp.ones((n, HEAD_SLAB - QK_NOPE - QK_ROPE), F32)
    tail0 = jnp.zeros((n, HEAD_SLAB - QK_NOPE - QK_ROPE), F32)
    z16 = jnp.zeros_like(sin)
    cos_t = jnp.concatenate([ones, cos, cos, tail1], axis=-1)
    sin_fwd = jnp.concatenate([zeros, z16, sin, tail0], axis=-1)
    sin_bwd = jnp.concatenate([zeros, -sin, z16, tail0], axis=-1)
    return cos_t, sin_fwd, sin_bwd


def _pad_mod(m, groups):
    m = m.reshape(groups, N_MOD, D_MODEL)
    return jnp.pad(m, ((0, 0), (0, MOD_ROWS - N_MOD), (0, 0)))


def _mla_weights(w_in, w_uq, w_ukv):
    pad_tail = HEAD_SLAB - QK_NOPE - QK_ROPE
    slab_cols = jnp.pad(w_in[:, Q_LORA + KV_LORA:], ((0, 0), (QK_NOPE, pad_tail)))
    win = jnp.concatenate([w_in[:, :Q_LORA + KV_LORA], slab_cols], axis=1).astype(BF16)
    wuq = jnp.pad(w_uq.reshape(Q_LORA, MLA_HEADS, QK_NOPE + QK_ROPE), ((0, 0), (0, 0), (0, pad_tail)))
    wuq = wuq.reshape(Q_LORA, MLA_HEADS * HEAD_SLAB).astype(BF16)
    wkv = w_ukv.reshape(KV_LORA, MLA_HEADS, QK_NOPE + V_HEAD)
    wk = jnp.pad(wkv[:, :, :QK_NOPE], ((0, 0), (0, 0), (0, HEAD_SLAB - QK_NOPE)))
    wk = wk.reshape(KV_LORA, MLA_HEADS * HEAD_SLAB).astype(BF16)
    wv = wkv[:, :, QK_NOPE:].reshape(KV_LORA, MLA_HEADS * V_HEAD).astype(BF16)
    return win, wuq, wk, wv


def kernel(x_prompt, x_sample, cache_na_k, cache_na_v, cache_mla_ckv, cache_mla_krope, c, c_ctx, w_ada, b_ada, ln_mix_g, ln_mix_b, ln_ffn_g, ln_ffn_b, w_in_ab, conv_w, na_rpb, w_out_ab, w_in_mla, q_norm_g, w_uq, kv_norm_g, w_ukv, w_out_mla, peer_w_q, peer_subkeys, peer_u, peer_v):
    nb_p, seq_p, d = x_prompt.shape
    nb_s, seq_s, _ = x_sample.shape
    past = cache_na_k.shape[2]
    rows = seq_s // GRID_W
    n_p = nb_p * seq_p
    xp = x_prompt.reshape(n_p, d)
    xs = x_sample.reshape(nb_s * seq_s, d)

    n_cond = 1 + nb_s
    cond_rows = -(-n_cond // 8) * 8
    cvec = jnp.concatenate([c_ctx[None], c, jnp.zeros((cond_rows - n_cond, d), F32)], axis=0)

    outs = {}
    for i in range(DEPTH):
        j = i // 2
        mods = ada_modulation(cvec, w_ada[i], b_ada[i])
        mod_p = _pad_mod(mods[0:1], 1)
        mod_s = _pad_mod(mods[1:n_cond], nb_s)
        lg = ln_mix_g[i].reshape(1, d)
        lb = ln_mix_b[i].reshape(1, d)
        if i % 2 == 0:
            w_in = w_in_ab[j].astype(BF16)
            w_out = w_out_ab[j].astype(BF16)
            cw = jnp.pad(conv_w[j], ((0, 8 - CONV_TAPS), (0, 0)))
            qkv_p, gu_p, kv_p = inproj_ab(xp, mod_p, w_in, n_p, True)
            qkv_s, gu_s = inproj_ab(xs, mod_s, w_in, seq_s, False)
            outs["na_k"] = kv_p[:, :NA_WIDTH].reshape(nb_p, 1, seq_p, NA_HEADS, NA_HEAD_DIM)
            outs["na_v"] = kv_p[:, NA_WIDTH:].reshape(nb_p, 1, seq_p, NA_HEADS, NA_HEAD_DIM)
            att_p = ctx_attention(qkv_p, seq_p)
            kc = cache_na_k[:, j].reshape(nb_s * past, NA_WIDTH).astype(BF16)
            vc = cache_na_v[:, j].reshape(nb_s * past, NA_WIDTH).astype(BF16)
            att_s = na_attention(qkv_s, kc, vc, na_bias_table(na_rpb[j]), nb_s, rows)
            xp = outproj_ab(att_p, gu_p, xp, mod_p, w_out, cw, lg, lb, n_p, seq_p)
            xs = outproj_ab(att_s, gu_s, xs, mod_s, w_out, cw, lg, lb, seq_s, seq_s)
        else:
            win, wuq, wk, wv = _mla_weights(w_in_mla[j], w_uq[j], w_ukv[j])
            w_out = w_out_mla[j].astype(BF16)
            qg = q_norm_g[j].reshape(1, Q_LORA)
            kvg = kv_norm_g[j].reshape(1, KV_LORA)
            q_p, ckv_p, slab_p = mla_inproj(xp, mod_p, win, qg, kvg, wuq, None, n_p, seq_p)
            q_s, ckv_s, slab_s = mla_inproj(xs, mod_s, win, qg, kvg, wuq, rope_lane_tables(seq_s), seq_s, seq_s)
            outs["ckv"] = ckv_p.reshape(nb_p, 1, seq_p, KV_LORA)
            outs["kr"] = slab_p[:, QK_NOPE:QK_NOPE + QK_ROPE].reshape(nb_p, 1, seq_p, QK_ROPE)
            k_p, v_p = mla_kv(ckv_p, slab_p, wk, wv)
            k_s, v_s = mla_kv(ckv_s, slab_s, wk, wv)
            ckv_c = cache_mla_ckv[:, j].reshape(nb_s * past, KV_LORA)
            slab_c = jnp.pad(cache_mla_krope[:, j].reshape(nb_s * past, QK_ROPE),
                             ((0, 0), (QK_NOPE, HEAD_SLAB - QK_NOPE - QK_ROPE)))
            k_c, v_c = mla_kv(ckv_c, slab_c, wk, wv)
            att_p = mla_attention(q_p, k_p, v_p, None, None, nb_p)
            att_s = mla_attention(q_s, k_s, v_s, k_c, v_c, nb_s)
            xp = outproj(att_p, xp, mod_p, w_out, lg, lb, n_p)
            xs = outproj(att_s, xs, mod_s, w_out, lg, lb, seq_s)
        wq = peer_w_q[i].astype(BF16)
        sk = peer_subkeys[i].reshape(2 * PEER_HEADS, N_KEYS, PEER_HALF).astype(BF16)
        u = peer_u[i].astype(BF16)
        vt = peer_v[i].T.astype(BF16)
        fg = ln_ffn_g[i].reshape(1, d)
        fb = ln_ffn_b[i].reshape(1, d)
        xp = peer_block(xp, mod_p, wq, sk, u, vt, fg, fb, n_p)
        xs = peer_block(xs, mod_s, wq, sk, u, vt, fg, fb, seq_s)
    return (xp.reshape(nb_p, seq_p, d), xs.reshape(nb_s, seq_s, d),
            outs["na_k"], outs["na_v"], outs["ckv"], outs["kr"])
```
